```python
import math
import jax, jax.numpy as jnp
from jax import lax
import numpy as np

D_MODEL = 1024
BATCH = 2
SEQ = 8192
DEPTH = 2

GRID_W = 64
CTX_LEN = 256
EPS = 1e-6

CONV_WIDTH = 512
CONV_K = 3
DA_HEADS = 4
DA_HEAD_DIM = 64
DA_V_DIM = 2 * DA_HEAD_DIM
DA_WIDTH = DA_HEADS * DA_V_DIM
EVEN_IN = 3 * CONV_WIDTH + 3 * DA_WIDTH
EVEN_MIX = CONV_WIDTH + DA_WIDTH
Q_BLOCK = 128
ROPE_BASE = 10000.0

LRU_WIDTH = 1024
LRU_BLOCKS = 8
LRU_BLOCK = LRU_WIDTH // LRU_BLOCKS
LRU_CONV_K = 4
LRU_C = 8.0

N_GROUPS = 4
EXPERTS_PER_GROUP = 4
N_EXPERTS = N_GROUPS * EXPERTS_PER_GROUP
D_EXPERT = 512
TOP_K = 2

N_EVEN = (DEPTH + 1) // 2
N_ODD = DEPTH // 2

kernel_name = "hybrid_conv_diffattn_rglru_hmoe_dit"


def rms_norm(x, g):
    xf = x.astype(jnp.float32)
    y = xf * lax.rsqrt(jnp.mean(xf * xf, axis=-1, keepdims=True) + EPS)
    return (y * g.astype(jnp.float32)).astype(x.dtype)


def modulate(h, shift, scale):
    return h * (1.0 + scale) + shift


def axial_rope(n_rows):
    rows = jnp.repeat(jnp.arange(n_rows), GRID_W).astype(jnp.float32)
    cols = jnp.tile(jnp.arange(GRID_W), n_rows).astype(jnp.float32)
    n_freq = DA_HEAD_DIM // 4
    inv = ROPE_BASE ** (-jnp.arange(n_freq, dtype=jnp.float32) / n_freq)
    ang = jnp.concatenate([rows[:, None] * inv, cols[:, None] * inv], axis=-1)
    return jnp.cos(ang), jnp.sin(ang)


def apply_rope(t, cos, sin):
    half = t.shape[-1] // 2
    tf = t.astype(jnp.float32)
    t1, t2 = tf[..., :half], tf[..., half:]
    cs = cos[None, :, None, None, :]
    sn = sin[None, :, None, None, :]
    return jnp.concatenate([t1 * cs - t2 * sn, t2 * cs + t1 * sn], axis=-1).astype(t.dtype)


def depthwise_conv(u, w, b, pad_left, pad_right):
    n = u.shape[1]
    up = jnp.pad(u, ((0, 0), (pad_left, pad_right), (0, 0)))
    y = b
    for k in range(w.shape[0]):
        y = y + up[:, k:k + n] * w[k]
    return y


def diff_attn(q, k, v, lam):
    s = jnp.einsum('bhmqd,bhmkd->bhmqk', q, k, preferred_element_type=jnp.float32) * (DA_HEAD_DIM ** -0.5)
    p = jax.nn.softmax(s, axis=-1)
    w = p[:, :, 0] - lam * p[:, :, 1]
    return jnp.einsum('bhqk,bhkv->bhqv', w, v.astype(jnp.float32)).astype(v.dtype)


def even_mixer(h_lat, h_ctx, w_in, conv_w, conv_b, q_norm, k_norm, lq1, lk1, lq2, lk2,
               sub_norm, w_out, lam_init, cos, sin, with_ctx):
    bsz, n = h_lat.shape[:2]
    c_len = h_ctx.shape[1]
    kv_col = 3 * CONV_WIDTH + DA_WIDTH

    def to_heads_qk(t, g, m):
        return rms_norm(t.reshape(bsz, m, DA_HEADS, 2, DA_HEAD_DIM), g)

    def to_heads_v(t, m):
        return t.reshape(bsz, m, DA_HEADS, DA_V_DIM).transpose(0, 2, 1, 3)

    lam = (jnp.exp(jnp.sum(lq1.astype(jnp.float32) * lk1.astype(jnp.float32)))
           - jnp.exp(jnp.sum(lq2.astype(jnp.float32) * lk2.astype(jnp.float32))) + lam_init)

    z = h_lat @ w_in
    cb, cc, cx, q, k, v = jnp.split(z, 6, axis=-1)
    q_l = apply_rope(to_heads_qk(q, q_norm, n), cos, sin).transpose(0, 2, 3, 1, 4)
    k_l = apply_rope(to_heads_qk(k, k_norm, n), cos, sin).transpose(0, 2, 3, 1, 4)
    v_l = to_heads_v(v, n)

    if with_ctx:
        zc = h_ctx @ w_in
        ccb, ccc, ccx, qc, kc, vc = jnp.split(zc, 6, axis=-1)
    else:
        kc, vc = jnp.split(h_ctx @ w_in[:, kv_col:], 2, axis=-1)
    k_c = to_heads_qk(kc, k_norm, c_len).transpose(0, 2, 3, 1, 4)
    v_c = to_heads_v(vc, c_len)

    out_a = cb * depthwise_conv(cc * cx, conv_w, conv_b, 1, 1)

    k_all = jnp.concatenate([k_l, k_c], axis=3)
    v_all = jnp.concatenate([v_l, v_c], axis=2)
    nb = n // Q_BLOCK
    qb = q_l.reshape(bsz, DA_HEADS, 2, nb, Q_BLOCK, DA_HEAD_DIM).transpose(3, 0, 1, 2, 4, 5)
    o = lax.map(lambda q_blk: diff_attn(q_blk, k_all, v_all, lam), qb)
    o = o.transpose(1, 0, 3, 2, 4).reshape(bsz, n, DA_HEADS, DA_V_DIM)
    out_b = (rms_norm(o, sub_norm) * (1.0 - lam_init)).reshape(bsz, n, DA_WIDTH)
    y_lat = jnp.concatenate([out_a, out_b], axis=-1) @ w_out

    if not with_ctx:
        return y_lat, None
    out_ac = ccb * depthwise_conv(ccc * ccx, conv_w, conv_b, 1, 1)
    q_c = to_heads_qk(qc, q_norm, c_len).transpose(0, 2, 3, 1, 4)
    oc = diff_attn(q_c, k_c, v_c, lam).transpose(0, 2, 1, 3)
    out_bc = (rms_norm(oc, sub_norm) * (1.0 - lam_init)).reshape(bsz, c_len, DA_WIDTH)
    y_ctx = jnp.concatenate([out_ac, out_bc], axis=-1) @ w_out
    return y_lat, y_ctx


def block_diag(u, w, b):
    ub = u.reshape(u.shape[:-1] + (LRU_BLOCKS, LRU_BLOCK))
    return jnp.einsum('bshi,hij->bshj', ub, w).reshape(u.shape) + b


def scan_from(a, b, h0, reverse):
    def combine(e1, e2):
        a1, b1 = e1
        a2, b2 = e2
        return a1 * a2, a2 * b1 + b2
    a_cum, b_cum = lax.associative_scan(combine, (a, b), reverse=reverse, axis=1)
    return a_cum * h0[:, None, :] + b_cum


def rglru_direction(u_lat, u_ctx, conv_w, conv_b, w_a, b_a, w_x, b_x, lam, reverse):
    pads = (0, LRU_CONV_K - 1) if reverse else (LRU_CONV_K - 1, 0)

    def coeffs(u):
        uc = depthwise_conv(u, conv_w, conv_b, pads[0], pads[1]).astype(jnp.float32)
        r = jax.nn.sigmoid(block_diag(uc, w_a.astype(jnp.float32), b_a.astype(jnp.float32)))
        i = jax.nn.sigmoid(block_diag(uc, w_x.astype(jnp.float32), b_x.astype(jnp.float32)))
        log_a = -LRU_C * r * jax.nn.softplus(-lam.astype(jnp.float32))
        return jnp.exp(log_a), jnp.sqrt(-jnp.expm1(2.0 * log_a)) * (i * uc)

    a_c, b_c = coeffs(u_ctx)
    h_ctx = scan_from(a_c, b_c, jnp.zeros((u_ctx.shape[0], LRU_WIDTH), jnp.float32), reverse)
    h0 = h_ctx[:, 0] if reverse else h_ctx[:, -1]
    a_l, b_l = coeffs(u_lat)
    h_lat = scan_from(a_l, b_l, h0, reverse)
    return h_lat, h_ctx


def odd_mixer(h_lat, h_ctx, w_in, conv_w, conv_b, w_a, b_a, w_x, b_x, lam, w_out, with_ctx):
    z = h_lat @ w_in
    y_l = jax.nn.gelu(z[..., :LRU_WIDTH], approximate=True)
    u_l = z[..., LRU_WIDTH:]
    if with_ctx:
        zc = h_ctx @ w_in
        y_c = jax.nn.gelu(zc[..., :LRU_WIDTH], approximate=True)
        u_c = zc[..., LRU_WIDTH:]
    else:
        u_c = h_ctx @ w_in[:, LRU_WIDTH:]
    hf_l, hf_c = rglru_direction(u_l, u_c, conv_w[0], conv_b[0], w_a[0], b_a[0], w_x[0], b_x[0], lam[0], False)
    hb_l, hb_c = rglru_direction(u_l, u_c, conv_w[1], conv_b[1], w_a[1], b_a[1], w_x[1], b_x[1], lam[1], True)
    y_lat = (y_l * (hf_l + hb_l).astype(y_l.dtype)) @ w_out
    if not with_ctx:
        return y_lat, None
    y_ctx = (y_c * (hf_c + hb_c).astype(y_c.dtype)) @ w_out
    return y_lat, y_ctx


def hier_moe(h, w_grp, b_grp, w_rt, b_rt, w_gate, w_up, w_down):
    f32 = jnp.float32
    g_logit = jnp.einsum('bsd,dg->bsg', h, w_grp, preferred_element_type=f32) + b_grp.astype(f32)
    p_grp = jax.nn.softmax(g_logit, axis=-1)
    g_idx = jnp.argmax(g_logit, axis=-1)
    p_sel = jnp.max(p_grp, axis=-1, keepdims=True)
    e_logit = jnp.einsum('bsd,dge->bsge', h, w_rt, preferred_element_type=f32) + b_rt.astype(f32)
    e_sel = jnp.einsum('bsge,bsg->bse', e_logit, jax.nn.one_hot(g_idx, N_GROUPS, dtype=f32))
    top_v, top_i = lax.top_k(e_sel, TOP_K)
    w_sel = jax.nn.softmax(top_v, axis=-1) * p_sel
    eid = g_idx[..., None] * EXPERTS_PER_GROUP + top_i
    gate = jnp.sum(jax.nn.one_hot(eid, N_EXPERTS, dtype=f32) * w_sel[..., None], axis=-2)
    hg = jnp.einsum('bsd,edf->bsef', h, w_gate)
    hu = jnp.einsum('bsd,edf->bsef', h, w_up)
    act = jax.nn.silu(hg) * hu * gate[..., None].astype(h.dtype)
    return jnp.einsum('bsef,efd->bsd', act, w_down)


def setup_inputs(seed: int = 0) -> dict:
    key = jax.random.key(seed)
    ks = iter(jax.random.split(key, 64))
    D = D_MODEL

    def nrm(shape, scale):
        return jax.random.normal(next(ks), shape, jnp.float32) * scale

    def gain(shape):
        return 1.0 + nrm(shape, 0.02)

    u = jax.random.uniform(next(ks), (N_ODD, 2, LRU_WIDTH), jnp.float32, minval=0.9, maxval=0.999)
    s = u ** (1.0 / LRU_C)
    od_lam = jnp.log(s) - jnp.log1p(-s)
    return {
        "x": nrm((BATCH, SEQ, D), 1.0),
        "c": nrm((BATCH, D), 1.0),
        "ctx": nrm((BATCH, CTX_LEN, D), 1.0),
        "c_ctx": nrm((D,), 1.0),
        "ada_w": nrm((DEPTH, D, 6 * D), 0.5 * D ** -0.5),
        "ada_b": nrm((DEPTH, 6 * D), 0.02),
        "norm_mix": gain((DEPTH, D)),
        "norm_ffn": gain((DEPTH, D)),
        "ev_w_in": nrm((N_EVEN, D, EVEN_IN), D ** -0.5),
        "ev_conv_w": nrm((N_EVEN, CONV_K, CONV_WIDTH), CONV_K ** -0.5),
        "ev_conv_b": nrm((N_EVEN, CONV_WIDTH), 0.02),
        "ev_q_norm": gain((N_EVEN, DA_HEAD_DIM)),
        "ev_k_norm": gain((N_EVEN, DA_HEAD_DIM)),
        "ev_lam_q1": nrm((N_EVEN, DA_HEAD_DIM), 0.1),
        "ev_lam_k1": nrm((N_EVEN, DA_HEAD_DIM), 0.1),
        "ev_lam_q2": nrm((N_EVEN, DA_HEAD_DIM), 0.1),
        "ev_lam_k2": nrm((N_EVEN, DA_HEAD_DIM), 0.1),
        "ev_sub_norm": gain((N_EVEN, DA_V_DIM)),
        "ev_w_out": nrm((N_EVEN, EVEN_MIX, D), EVEN_MIX ** -0.5),
        "od_w_in": nrm((N_ODD, D, 2 * LRU_WIDTH), D ** -0.5),
        "od_conv_w": nrm((N_ODD, 2, LRU_CONV_K, LRU_WIDTH), LRU_CONV_K ** -0.5),
        "od_conv_b": nrm((N_ODD, 2, LRU_WIDTH), 0.02),
        "od_w_a": nrm((N_ODD, 2, LRU_BLOCKS, LRU_BLOCK, LRU_BLOCK), LRU_BLOCK ** -0.5),
        "od_b_a": nrm((N_ODD, 2, LRU_WIDTH), 0.02),
        "od_w_x": nrm((N_ODD, 2, LRU_BLOCKS, LRU_BLOCK, LRU_BLOCK), LRU_BLOCK ** -0.5),
        "od_b_x": nrm((N_ODD, 2, LRU_WIDTH), 0.02),
        "od_lam": od_lam,
        "od_w_out": nrm((N_ODD, LRU_WIDTH, D), LRU_WIDTH ** -0.5),
        "moe_w_grp": nrm((DEPTH, D, N_GROUPS), D ** -0.5),
        "moe_b_grp": nrm((DEPTH, N_GROUPS), 0.01),
        "moe_w_rt": nrm((DEPTH, D, N_GROUPS, EXPERTS_PER_GROUP), D ** -0.5),
        "moe_b_rt": nrm((DEPTH, N_GROUPS, EXPERTS_PER_GROUP), 0.01),
        "moe_w_gate": nrm((DEPTH, N_EXPERTS, D, D_EXPERT), D ** -0.5),
        "moe_w_up": nrm((DEPTH, N_EXPERTS, D, D_EXPERT), D ** -0.5),
        "moe_w_down": nrm((DEPTH, N_EXPERTS, D_EXPERT, D), D_EXPERT ** -0.5),
    }


def reference(x, c, ctx, c_ctx, ada_w, ada_b, norm_mix, norm_ffn,
              ev_w_in, ev_conv_w, ev_conv_b, ev_q_norm, ev_k_norm,
              ev_lam_q1, ev_lam_k1, ev_lam_q2, ev_lam_k2, ev_sub_norm, ev_w_out,
              od_w_in, od_conv_w, od_conv_b, od_w_a, od_b_a, od_w_x, od_b_x, od_lam, od_w_out,
              moe_w_grp, moe_b_grp, moe_w_rt, moe_b_rt, moe_w_gate, moe_w_up, moe_w_down):
    n = x.shape[1]
    ROWS = n // GRID_W
    cos, sin = axial_rope(ROWS)
    for l in range(DEPTH):
        with_ctx = l < DEPTH - 1
        mod = jax.nn.silu(c) @ ada_w[l] + ada_b[l]
        mod_c = jax.nn.silu(c_ctx) @ ada_w[l] + ada_b[l]
        sh1, sc1, g1, sh2, sc2, g2 = jnp.split(mod[:, None, :], 6, axis=-1)
        csh1, csc1, cg1, csh2, csc2, cg2 = jnp.split(mod_c, 6, axis=-1)
        h = modulate(rms_norm(x, norm_mix[l]), sh1, sc1)
        hc = modulate(rms_norm(ctx, norm_mix[l]), csh1, csc1)
        j = l // 2
        if l % 2 == 0:
            lam_init = 0.8 - 0.6 * math.exp(-0.3 * l)
            y, yc = even_mixer(h, hc, ev_w_in[j], ev_conv_w[j], ev_conv_b[j], ev_q_norm[j], ev_k_norm[j],
                               ev_lam_q1[j], ev_lam_k1[j], ev_lam_q2[j], ev_lam_k2[j], ev_sub_norm[j],
                               ev_w_out[j], lam_init, cos, sin, with_ctx)
        else:
            y, yc = odd_mixer(h, hc, od_w_in[j], od_conv_w[j], od_conv_b[j], od_w_a[j], od_b_a[j],
                              od_w_x[j], od_b_x[j], od_lam[j], od_w_out[j], with_ctx)
        x = x + g1 * y
        hf = modulate(rms_norm(x, norm_ffn[l]), sh2, sc2)
        x = x + g2 * hier_moe(hf, moe_w_grp[l], moe_b_grp[l], moe_w_rt[l], moe_b_rt[l],
                              moe_w_gate[l], moe_w_up[l], moe_w_down[l])
        if with_ctx:
            ctx = ctx + cg1 * yc
            hfc = modulate(rms_norm(ctx, norm_ffn[l]), csh2, csc2)
            ctx = ctx + cg2 * hier_moe(hfc, moe_w_grp[l], moe_b_grp[l], moe_w_rt[l], moe_b_rt[l],
                                       moe_w_gate[l], moe_w_up[l], moe_w_down[l])
    return x
```

```python
import functools
import math

import jax
import jax.numpy as jnp
from jax import lax
from jax.experimental import pallas as pl
from jax.experimental.pallas import tpu as pltpu

F32 = jnp.float32
BF16 = jnp.bfloat16
HIGHEST = lax.Precision.HIGHEST

D_MODEL = 1024
SEQ = 8192
CTX_LEN = 256
TOK = SEQ + CTX_LEN
GRID_W = 64
EPS = 1e-6

CONV_WIDTH = 512
DA_HEADS = 4
DA_HEAD_DIM = 64
DA_V_DIM = 2 * DA_HEAD_DIM
DA_WIDTH = DA_HEADS * DA_V_DIM
EVEN_IN = 3 * CONV_WIDTH + 3 * DA_WIDTH
ROPE_BASE = 10000.0

LRU_WIDTH = 1024
LRU_BLOCKS = 8
LRU_BLOCK = LRU_WIDTH // LRU_BLOCKS
LRU_CONV_K = 4
LRU_C = 8.0

N_GROUPS = 4
EXPERTS_PER_GROUP = 4
N_EXPERTS = N_GROUPS * EXPERTS_PER_GROUP
D_EXPERT = 512

LANES = 128
SUBLANES = 8
TM = 256
N_LAT_TILES = SEQ // TM
N_TILES = TOK // TM
NEG = -1e30
MIB = 2 ** 20


def _cparams(semantics, vmem_mib):
    return pltpu.CompilerParams(dimension_semantics=semantics, vmem_limit_bytes=vmem_mib * MIB)


def _sigmoid(x):
    return 1.0 / (1.0 + jnp.exp(-x))


def _norm_mod(x, g, shift, scale):
    ms = jnp.mean(x * x, axis=-1, keepdims=True)
    return (x * lax.rsqrt(ms + EPS) * g) * (1.0 + scale) + shift


ADA_TN = 1536


def _ada_kernel(c_ref, w_ref, b_ref, o_ref):
    c = c_ref[...]
    a = c * _sigmoid(c)
    o_ref[...] = jnp.dot(a, w_ref[...], precision=HIGHEST, preferred_element_type=F32) + b_ref[...]


def _ada_mod(cvec, ada_w, ada_b):
    depth, d, n = ada_w.shape
    return pl.pallas_call(
        _ada_kernel,
        grid=(depth, n // ADA_TN),
        in_specs=[
            pl.BlockSpec((SUBLANES, d), lambda l, j: (0, 0)),
            pl.BlockSpec((None, d, ADA_TN), lambda l, j: (l, 0, j)),
            pl.BlockSpec((None, 1, ADA_TN), lambda l, j: (l, 0, j)),
        ],
        out_specs=pl.BlockSpec((None, SUBLANES, ADA_TN), lambda l, j: (l, 0, j)),
        out_shape=jax.ShapeDtypeStruct((depth, SUBLANES, n), F32),
        compiler_params=_cparams(("arbitrary", "arbitrary"), 40),
        name="ada_mod",
    )(cvec, ada_w, ada_b.reshape(depth, 1, n))


def _mod_index(b, i):
    return (b * 2 + i // N_LAT_TILES, 0, 0)


def _inproj_even_kernel(x_ref, mod_ref, g_ref, w_ref, qg_ref, kg_ref, cos_ref, sin_ref, bd_ref,
                        cb_ref, p_ref, q_ref, k_ref, v_ref):
    h = _norm_mod(x_ref[...], g_ref[...], mod_ref[0:1, :], mod_ref[1:2, :])
    z = jnp.dot(h.astype(BF16), w_ref[...], preferred_element_type=F32)
    cw = CONV_WIDTH
    cb_ref[...] = z[:, :cw]
    p_ref[...] = z[:, cw:2 * cw] * z[:, 2 * cw:3 * cw]

    reps = DA_WIDTH // LANES
    cosf = jnp.concatenate([cos_ref[...]] * reps, axis=1)
    sinf = jnp.concatenate([sin_ref[...]] * reps, axis=1)
    lane = lax.broadcasted_iota(jnp.int32, (TM, DA_WIDTH), 1)
    first_half = (lane & (DA_HEAD_DIM - 1)) < DA_HEAD_DIM // 2
    bd = bd_ref[...]

    def head_norm_rope(t, gain):
        tt = t * t
        hi = tt.astype(BF16)
        lo = (tt - hi.astype(F32)).astype(BF16)
        ms = (jnp.dot(hi, bd, preferred_element_type=F32) + jnp.dot(lo, bd, preferred_element_type=F32))
        y = t * lax.rsqrt(ms + EPS) * gain
        fwd = pltpu.roll(y, DA_WIDTH - DA_HEAD_DIM // 2, axis=1)
        bwd = pltpu.roll(y, DA_HEAD_DIM // 2, axis=1)
        return y * cosf + jnp.where(first_half, fwd, bwd) * sinf

    base = 3 * cw
    q = head_norm_rope(z[:, base:base + DA_WIDTH], qg_ref[...])
    q_ref[...] = (q * (DA_HEAD_DIM ** -0.5)).astype(BF16)
    k = head_norm_rope(z[:, base + DA_WIDTH:base + 2 * DA_WIDTH], kg_ref[...])
    k_ref[...] = k.astype(BF16)
    v_ref[...] = z[:, base + 2 * DA_WIDTH:].astype(BF16)


def _inproj_even(xs, mod, g, w_in, qg, kg, cos, sin, bd):
    bsz = xs.shape[0]
    tok_spec = lambda width: pl.BlockSpec((None, TM, width), lambda b, i: (b, i, 0))
    const = lambda shape: pl.BlockSpec(shape, lambda b, i: (0,) * len(shape))
    return pl.pallas_call(
        _inproj_even_kernel,
        grid=(bsz, N_TILES),
        in_specs=[
            tok_spec(D_MODEL),
            pl.BlockSpec((None, SUBLANES, D_MODEL), _mod_index),
            const((1, D_MODEL)),
            const((D_MODEL, EVEN_IN)),
            const((1, DA_WIDTH)),
            const((1, DA_WIDTH)),
            pl.BlockSpec((TM, LANES), lambda b, i: (i, 0)),
            pl.BlockSpec((TM, LANES), lambda b, i: (i, 0)),
            const((DA_WIDTH, DA_WIDTH)),
        ],
        out_specs=[tok_spec(CONV_WIDTH), tok_spec(CONV_WIDTH), tok_spec(DA_WIDTH), tok_spec(DA_WIDTH),
                   tok_spec(DA_WIDTH)],
        out_shape=[
            jax.ShapeDtypeStruct((bsz, TOK, CONV_WIDTH), F32),
            jax.ShapeDtypeStruct((bsz, TOK, CONV_WIDTH), F32),
            jax.ShapeDtypeStruct((bsz, TOK, DA_WIDTH), BF16),
            jax.ShapeDtypeStruct((bsz, TOK, DA_WIDTH), BF16),
            jax.ShapeDtypeStruct((bsz, TOK, DA_WIDTH), BF16),
        ],
        compiler_params=_cparams(("parallel", "arbitrary"), 48),
        name="inproj_even",
    )(xs, mod, g, w_in, qg, kg, cos, sin, bd)


def _attn_kernel(lam_ref, subg_ref, q_ref, k_ref, v_ref, o_ref, qz_ref, m_ref, l_ref, acc_ref,
                 *, tq, tk, nk, lam_init):
    q = q_ref[...]
    lane = lax.broadcasted_iota(jnp.int32, (tq, LANES), 1)
    zero = jnp.zeros_like(q)
    qz_ref[0:tq, :] = jnp.where(lane < DA_HEAD_DIM, q, zero)
    qz_ref[tq:2 * tq, :] = jnp.where(lane >= DA_HEAD_DIM, q, zero)
    m_ref[...] = jnp.full(m_ref.shape, NEG, F32)
    l_ref[...] = jnp.zeros(l_ref.shape, F32)
    acc_ref[...] = jnp.zeros(acc_ref.shape, F32)

    def body(j, carry):
        off = pl.multiple_of(j * tk, tk)
        kk = k_ref[pl.ds(off, tk), :]
        vv = v_ref[pl.ds(off, tk), :]
        s = lax.dot_general(qz_ref[...], kk, (((1,), (1,)), ((), ())), preferred_element_type=F32)
        m_prev = m_ref[...]
        m_new = jnp.maximum(m_prev, jnp.max(s, axis=1, keepdims=True))
        alpha = jnp.exp(m_prev - m_new)
        p = jnp.exp(s - m_new[:, :1])
        l_ref[...] = alpha * l_ref[...] + jnp.sum(p, axis=1, keepdims=True)
        acc_ref[...] = alpha * acc_ref[...] + jnp.dot(p.astype(BF16), vv, preferred_element_type=F32)
        m_ref[...] = m_new
        return carry

    lax.fori_loop(0, nk, body, 0)

    o = acc_ref[...] / l_ref[...]
    lv = lam_ref[...]
    lam = (jnp.exp(jnp.sum(lv[0:1, :] * lv[1:2, :], axis=1, keepdims=True))
           - jnp.exp(jnp.sum(lv[2:3, :] * lv[3:4, :], axis=1, keepdims=True)) + lam_init)
    d = o[:tq] - lam * o[tq:]
    ms = jnp.mean(d * d, axis=1, keepdims=True)
    o_ref[...] = (d * lax.rsqrt(ms + EPS) * subg_ref[...] * (1.0 - lam_init)).astype(BF16)


def _attention(q, k, v, lamv, subg, o_prev, *, tq, q_start, n_q, kv_start, kv_len, tk, lam_init):
    bsz = q.shape[0]
    q_blk0 = q_start // tq
    kv_blk = kv_start // kv_len
    q_spec = pl.BlockSpec((None, tq, LANES), lambda b, h, i: (b, q_blk0 + i, h))
    kv_spec = pl.BlockSpec((None, kv_len, LANES), lambda b, h, i: (b, kv_blk, h))
    const = lambda shape: pl.BlockSpec(shape, lambda b, h, i: (0,) * len(shape))
    kernel = functools.partial(_attn_kernel, tq=tq, tk=tk, nk=kv_len // tk, lam_init=lam_init)
    in_specs = [const((SUBLANES, LANES)), const((1, LANES)), q_spec, kv_spec, kv_spec]
    args = [lamv, subg, q, k, v]
    aliases = {}
    if o_prev is not None:
        in_specs.append(pl.BlockSpec(memory_space=pl.ANY))
        args.append(o_prev)
        aliases = {5: 0}
        kernel = _drop_extra_input(kernel, 5)
    return pl.pallas_call(
        kernel,
        grid=(bsz, DA_HEADS, n_q),
        in_specs=in_specs,
        out_specs=q_spec,
        out_shape=jax.ShapeDtypeStruct(q.shape, BF16),
        scratch_shapes=[
            pltpu.VMEM((2 * tq, LANES), BF16),
            pltpu.VMEM((2 * tq, LANES), F32),
            pltpu.VMEM((2 * tq, LANES), F32),
            pltpu.VMEM((2 * tq, LANES), F32),
        ],
        input_output_aliases=aliases,
        compiler_params=_cparams(("parallel", "parallel", "arbitrary"), 48),
        name="diff_attn",
    )(*args)


def _drop_extra_input(kernel, pos):
    def wrapped(*refs):
        return kernel(*refs[:pos], *refs[pos + 1:])
    return wrapped


def _post_tail(y, x_ref, mod_ref, g_ref, wr_ref, br_ref, x1_ref, hf_ref, gate_ref):
    x1 = x_ref[...] + mod_ref[2:3, :] * y
    x1_ref[...] = x1
    hf = _norm_mod(x1, g_ref[...], mod_ref[3:4, :], mod_ref[4:5, :])
    hf_ref[...] = hf.astype(BF16)

    logits = jnp.dot(hf, wr_ref[...], precision=HIGHEST, preferred_element_type=F32) + br_ref[...]
    lane = lax.broadcasted_iota(jnp.int32, logits.shape, 1).astype(F32)
    big = float(LANES)
    is_g = (lane >= N_EXPERTS) & (lane < N_EXPERTS + N_GROUPS)
    gl = jnp.where(is_g, logits, NEG)
    gm = jnp.max(gl, axis=1, keepdims=True)
    g_idx = jnp.min(jnp.where(gl == gm, lane, big), axis=1, keepdims=True) - N_EXPERTS
    p_sel = 1.0 / jnp.sum(jnp.exp(gl - gm), axis=1, keepdims=True)
    lo = g_idx * EXPERTS_PER_GROUP
    el = jnp.where((lane >= lo) & (lane < lo + EXPERTS_PER_GROUP), logits, NEG)
    v1 = jnp.max(el, axis=1, keepdims=True)
    i1 = jnp.min(jnp.where(el == v1, lane, big), axis=1, keepdims=True)
    el2 = jnp.where(lane == i1, NEG, el)
    v2 = jnp.max(el2, axis=1, keepdims=True)
    i2 = jnp.min(jnp.where(el2 == v2, lane, big), axis=1, keepdims=True)
    t = jnp.exp(v2 - v1)
    w1 = p_sel / (1.0 + t)
    gate_ref[...] = jnp.where(lane == i1, w1, 0.0) + jnp.where(lane == i2, t * w1, 0.0)


def _post_even_kernel(cb_ref, p_ref, pprev_ref, pnext_ref, cw_ref, cbias_ref, o_ref, w_ref,
                      x_ref, mod_ref, g_ref, wr_ref, br_ref, x1_ref, hf_ref, gate_ref):
    i = pl.program_id(1)
    pc = p_ref[...]
    row = lax.broadcasted_iota(jnp.int32, pc.shape, 0)
    has_prev = jnp.logical_and(i != 0, i != N_LAT_TILES)
    has_next = i < N_LAT_TILES - 1
    prev_row = jnp.where(has_prev, pprev_ref[SUBLANES - 1:SUBLANES, :], 0.0)
    next_row = jnp.where(has_next, pnext_ref[0:1, :], 0.0)
    up = jnp.where(row == 0, prev_row, pltpu.roll(pc, 1, axis=0))
    dn = jnp.where(row == TM - 1, next_row, pltpu.roll(pc, TM - 1, axis=0))
    conv = cbias_ref[...] + cw_ref[0:1, :] * up + cw_ref[1:2, :] * pc + cw_ref[2:3, :] * dn
    out_a = (cb_ref[...] * conv).astype(BF16)
    y = (jnp.dot(out_a, w_ref[0:CONV_WIDTH, :], preferred_element_type=F32)
         + jnp.dot(o_ref[...], w_ref[CONV_WIDTH:, :], preferred_element_type=F32))
    _post_tail(y, x_ref, mod_ref, g_ref, wr_ref, br_ref, x1_ref, hf_ref, gate_ref)


def _post_odd_kernel(y_ref, hf_in_ref, hb_in_ref, w_ref, x_ref, mod_ref, g_ref, wr_ref, br_ref,
                     x1_ref, hf_ref, gate_ref):
    a = (y_ref[...] * (hf_in_ref[...] + hb_in_ref[...])).astype(BF16)
    y = jnp.dot(a, w_ref[...], preferred_element_type=F32)
    _post_tail(y, x_ref, mod_ref, g_ref, wr_ref, br_ref, x1_ref, hf_ref, gate_ref)


def _post_specs(bsz, rows):
    tok_spec = lambda width: pl.BlockSpec((None, TM, width), lambda b, i: (b, i, 0))
    const = lambda shape: pl.BlockSpec(shape, lambda b, i: (0,) * len(shape))
    tail_in = [tok_spec(D_MODEL), pl.BlockSpec((None, SUBLANES, D_MODEL), _mod_index), const((1, D_MODEL)),
               const((D_MODEL, LANES)), const((1, LANES))]
    out_specs = [tok_spec(D_MODEL), tok_spec(D_MODEL), tok_spec(LANES)]
    out_shape = [jax.ShapeDtypeStruct((bsz, rows, D_MODEL), F32),
                 jax.ShapeDtypeStruct((bsz, rows, D_MODEL), BF16),
                 jax.ShapeDtypeStruct((bsz, rows, LANES), F32)]
    return tok_spec, const, tail_in, out_specs, out_shape


def _post_even(cb, p, conv_w, conv_b, o, w_out, xs, mod, g, wr, br):
    bsz = xs.shape[0]
    tok_spec, const, tail_in, out_specs, out_shape = _post_specs(bsz, TOK)
    halo_blocks = TM // SUBLANES
    last_halo = TOK // SUBLANES - 1
    prev_spec = pl.BlockSpec((None, SUBLANES, CONV_WIDTH),
                             lambda b, i: (b, jnp.maximum(i * halo_blocks - 1, 0), 0))
    next_spec = pl.BlockSpec((None, SUBLANES, CONV_WIDTH),
                             lambda b, i: (b, jnp.minimum((i + 1) * halo_blocks, last_halo), 0))
    return pl.pallas_call(
        _post_even_kernel,
        grid=(bsz, N_TILES),
        in_specs=[tok_spec(CONV_WIDTH), tok_spec(CONV_WIDTH), prev_spec, next_spec,
                  const((SUBLANES, CONV_WIDTH)), const((1, CONV_WIDTH)), tok_spec(DA_WIDTH),
                  const((D_MODEL, D_MODEL))] + tail_in,
        out_specs=out_specs,
        out_shape=out_shape,
        compiler_params=_cparams(("parallel", "arbitrary"), 48),
        name="post_even",
    )(cb, p, p, p, conv_w, conv_b, o, w_out, xs, mod, g, wr, br)


def _post_odd(y, hfw, hbw, w_out, xs, mod, g, wr, br):
    bsz = xs.shape[0]
    tok_spec, const, tail_in, out_specs, out_shape = _post_specs(bsz, SEQ)
    return pl.pallas_call(
        _post_odd_kernel,
        grid=(bsz, N_LAT_TILES),
        in_specs=[tok_spec(LRU_WIDTH), tok_spec(LRU_WIDTH), tok_spec(LRU_WIDTH),
                  const((LRU_WIDTH, D_MODEL))] + tail_in,
        out_specs=out_specs,
        out_shape=out_shape,
        compiler_params=_cparams(("parallel", "arbitrary"), 48),
        name="post_odd",
    )(y, hfw, hbw, w_out, xs, mod, g, wr, br)


def _moe_dense_kernel(h_ref, gate_ref, wg_ref, wu_ref, wd_ref, x_ref, mod_ref, o_ref, acc_ref, *, tm):
    i = pl.program_id(1)
    e = pl.program_id(2)

    @pl.when(e == 0)
    def _():
        acc_ref[...] = jnp.zeros(acc_ref.shape, F32)

    h = h_ref[...]
    hg = jnp.dot(h, wg_ref[...].astype(BF16), preferred_element_type=F32)
    hu = jnp.dot(h, wu_ref[...].astype(BF16), preferred_element_type=F32)
    gate = gate_ref[...]
    lane = lax.broadcasted_iota(jnp.int32, gate.shape, 1)
    ge = jnp.sum(jnp.where(lane == e, gate, 0.0), axis=1, keepdims=True)
    act = (hg * _sigmoid(hg)) * hu * ge
    acc_ref[...] += jnp.dot(act.astype(BF16), wd_ref[...].astype(BF16), preferred_element_type=F32)

    @pl.when(e == N_EXPERTS - 1)
    def _():
        rows = i * tm + lax.broadcasted_iota(jnp.int32, (tm, 1), 0)
        g2 = jnp.where(rows >= SEQ, mod_ref[1, 5:6, :], mod_ref[0, 5:6, :])
        o_ref[...] = x_ref[...] + g2 * acc_ref[...]


def _moe_dense(hf, gate, wg, wu, wd, x1, mod2, *, tm, n_tiles, out_rows):
    bsz = hf.shape[0]
    tok_spec = lambda width: pl.BlockSpec((None, tm, width), lambda b, i, e: (b, i, 0))
    return pl.pallas_call(
        functools.partial(_moe_dense_kernel, tm=tm),
        grid=(bsz, n_tiles, N_EXPERTS),
        in_specs=[
            tok_spec(D_MODEL), tok_spec(LANES),
            pl.BlockSpec((None, D_MODEL, D_EXPERT), lambda b, i, e: (e, 0, 0)),
            pl.BlockSpec((None, D_MODEL, D_EXPERT), lambda b, i, e: (e, 0, 0)),
            pl.BlockSpec((None, D_EXPERT, D_MODEL), lambda b, i, e: (e, 0, 0)),
            tok_spec(D_MODEL),
            pl.BlockSpec((None, 2, SUBLANES, D_MODEL), lambda b, i, e: (b, 0, 0, 0)),
        ],
        out_specs=tok_spec(D_MODEL),
        out_shape=jax.ShapeDtypeStruct((bsz, out_rows, D_MODEL), F32),
        scratch_shapes=[pltpu.VMEM((tm, D_MODEL), F32)],
        compiler_params=_cparams(("parallel", "parallel", "arbitrary"), 56),
        name="moe_dense",
    )(hf, gate, wg, wu, wd, x1, mod2)


def _inproj_odd_kernel(x_ref, mod_ref, g_ref, w_ref, y_ref, u_ref):
    h = _norm_mod(x_ref[...], g_ref[...], mod_ref[0:1, :], mod_ref[1:2, :])
    z = jnp.dot(h.astype(BF16), w_ref[...], preferred_element_type=F32)
    zy = z[:, :LRU_WIDTH]
    c0 = math.sqrt(2.0 / math.pi)
    y_ref[...] = 0.5 * zy * (1.0 + jnp.tanh(c0 * (zy + 0.044715 * (zy * zy * zy))))
    u_ref[...] = z[:, LRU_WIDTH:]


def _inproj_odd(xs, mod, g, w_in):
    bsz = xs.shape[0]
    tok_spec = lambda width: pl.BlockSpec((None, TM, width), lambda b, i: (b, i, 0))
    const = lambda shape: pl.BlockSpec(shape, lambda b, i: (0,) * len(shape))
    return pl.pallas_call(
        _inproj_odd_kernel,
        grid=(bsz, N_TILES),
        in_specs=[tok_spec(D_MODEL), pl.BlockSpec((None, SUBLANES, D_MODEL), _mod_index),
                  const((1, D_MODEL)), const((D_MODEL, 2 * LRU_WIDTH))],
        out_specs=[tok_spec(LRU_WIDTH), tok_spec(LRU_WIDTH)],
        out_shape=[jax.ShapeDtypeStruct((bsz, TOK, LRU_WIDTH), F32),
                   jax.ShapeDtypeStruct((bsz, TOK, LRU_WIDTH), F32)],
        compiler_params=_cparams(("parallel", "arbitrary"), 48),
        name="inproj_odd",
    )(xs, mod, g, w_in)


def _scan_kernel(u_ref, cw_ref, cbias_ref, wa_ref, ba_ref, wx_ref, bx_ref, lam_ref, h_ref,
                 halo_ref, carry_ref, a_ref, b_ref, *, reverse):
    i = pl.program_id(1)
    tt, w = u_ref.shape
    n_groups = tt // SUBLANES

    @pl.when(i <= 1)
    def _():
        halo_ref[...] = jnp.zeros(halo_ref.shape, F32)

    @pl.when(i == 0)
    def _():
        carry_ref[...] = jnp.zeros(carry_ref.shape, F32)

    u = u_ref[...]
    halo = halo_ref[...]
    row8 = lax.broadcasted_iota(jnp.int32, (SUBLANES, w), 0)
    k_self = 0 if reverse else LRU_CONV_K - 1
    uc = cbias_ref[...] + cw_ref[k_self:k_self + 1, :] * u
    for k in range(1, LRU_CONV_K):
        if reverse:
            tmp = pltpu.roll(u, tt - k, axis=0)
            hr = pltpu.roll(halo, SUBLANES - k, axis=0)
            edge = jnp.where(row8 >= SUBLANES - k, hr, tmp[tt - SUBLANES:, :])
            shifted = jnp.concatenate([tmp[:tt - SUBLANES, :], edge], axis=0)
            wk = cw_ref[k:k + 1, :]
        else:
            tmp = pltpu.roll(u, k, axis=0)
            hr = pltpu.roll(halo, k, axis=0)
            edge = jnp.where(row8 < k, hr, tmp[:SUBLANES, :])
            shifted = jnp.concatenate([edge, tmp[SUBLANES:, :]], axis=0)
            wk = cw_ref[LRU_CONV_K - 1 - k:LRU_CONV_K - k, :]
        uc = uc + wk * shifted
    halo_ref[...] = u[:SUBLANES, :] if reverse else u[tt - SUBLANES:, :]

    ucb = uc.astype(BF16)

    def block_diag(w_blocks):
        return jnp.concatenate(
            [jnp.dot(ucb[:, j * LRU_BLOCK:(j + 1) * LRU_BLOCK], w_blocks[j], preferred_element_type=F32)
             for j in range(LRU_BLOCKS)], axis=1)

    r = _sigmoid(block_diag(wa_ref) + ba_ref[...])
    gate_i = _sigmoid(block_diag(wx_ref) + bx_ref[...])
    neg_lam = -lam_ref[...]
    softplus = jnp.maximum(neg_lam, 0.0) + jnp.log1p(jnp.exp(-jnp.abs(neg_lam)))
    a = jnp.exp((-LRU_C * softplus) * r)
    a_ref[...] = a
    b_ref[...] = jnp.sqrt(1.0 - a * a) * (gate_i * uc)

    def group(gi, carry):
        g = n_groups - 1 - gi if reverse else gi
        off = pl.multiple_of(g * SUBLANES, SUBLANES)
        av = a_ref[pl.ds(off, SUBLANES), :]
        bv = b_ref[pl.ds(off, SUBLANES), :]
        for s in (1, 2, 4):
            if reverse:
                outside = row8 >= SUBLANES - s
                shift = SUBLANES - s
            else:
                outside = row8 < s
                shift = s
            a_sh = jnp.where(outside, 1.0, pltpu.roll(av, shift, axis=0))
            b_sh = jnp.where(outside, 0.0, pltpu.roll(bv, shift, axis=0))
            bv = av * b_sh + bv
            av = av * a_sh
        h = av * carry + bv
        h_ref[pl.ds(off, SUBLANES), :] = h
        last = h[0:1, :] if reverse else h[SUBLANES - 1:SUBLANES, :]
        return jnp.broadcast_to(last, (SUBLANES, w))

    carry_ref[...] = lax.fori_loop(0, n_groups, group, carry_ref[...])


def _rglru_scan(u, conv_w, conv_b, w_a, b_a, w_x, b_x, lam, *, reverse):
    bsz = u.shape[0]
    if reverse:
        tile_of = lambda i: N_LAT_TILES - i
    else:
        tile_of = lambda i: (i + N_LAT_TILES) % N_TILES
    tok_spec = pl.BlockSpec((None, TM, LRU_WIDTH), lambda b, i: (b, tile_of(i), 0))
    const = lambda shape: pl.BlockSpec(shape, lambda b, i: (0,) * len(shape))
    return pl.pallas_call(
        functools.partial(_scan_kernel, reverse=reverse),
        grid=(bsz, N_TILES),
        in_specs=[tok_spec, const((SUBLANES, LRU_WIDTH)), const((1, LRU_WIDTH)),
                  const((LRU_BLOCKS, LRU_BLOCK, LRU_BLOCK)), const((1, LRU_WIDTH)),
                  const((LRU_BLOCKS, LRU_BLOCK, LRU_BLOCK)), const((1, LRU_WIDTH)), const((1, LRU_WIDTH))],
        out_specs=tok_spec,
        out_shape=jax.ShapeDtypeStruct((bsz, TOK, LRU_WIDTH), F32),
        scratch_shapes=[pltpu.VMEM((SUBLANES, LRU_WIDTH), F32), pltpu.VMEM((SUBLANES, LRU_WIDTH), F32),
                        pltpu.VMEM((TM, LRU_WIDTH), F32), pltpu.VMEM((TM, LRU_WIDTH), F32)],
        compiler_params=_cparams(("parallel", "arbitrary"), 32),
        name="rglru_rev" if reverse else "rglru_fwd",
    )(u, conv_w, conv_b, w_a, b_a, w_x, b_x, lam)


def _pad_rows(a, rows):
    return jnp.pad(a, ((0, rows - a.shape[0]), (0, 0)))


def _rope_tables():
    t = jnp.arange(SEQ)
    n_freq = DA_HEAD_DIM // 4
    inv = ROPE_BASE ** (-jnp.arange(n_freq, dtype=F32) / n_freq)
    ang = jnp.concatenate([(t // GRID_W).astype(F32)[:, None] * inv,
                           (t % GRID_W).astype(F32)[:, None] * inv], axis=-1)
    cos, sin = jnp.cos(ang), jnp.sin(ang)
    cos64 = jnp.concatenate([cos, cos], axis=-1)
    sin64 = jnp.concatenate([-sin, sin], axis=-1)
    cos_t = jnp.concatenate([jnp.tile(cos64, (1, 2)), jnp.ones((CTX_LEN, LANES), F32)], axis=0)
    sin_t = jnp.concatenate([jnp.tile(sin64, (1, 2)), jnp.zeros((CTX_LEN, LANES), F32)], axis=0)
    return cos_t, sin_t


def _router_params(w_grp, b_grp, w_rt, b_rt):
    wr = jnp.concatenate([w_rt.reshape(D_MODEL, N_EXPERTS), w_grp], axis=1)
    br = jnp.concatenate([b_rt.reshape(N_EXPERTS), b_grp])
    pad = LANES - wr.shape[1]
    return jnp.pad(wr, ((0, 0), (0, pad))), jnp.pad(br, (0, pad)).reshape(1, LANES)


def kernel(x, c, ctx, c_ctx, ada_w, ada_b, norm_mix, norm_ffn, ev_w_in, ev_conv_w, ev_conv_b, ev_q_norm, ev_k_norm, ev_lam_q1, ev_lam_k1, ev_lam_q2, ev_lam_k2, ev_sub_norm, ev_w_out, od_w_in, od_conv_w, od_conv_b, od_w_a, od_b_a, od_w_x, od_b_x, od_lam, od_w_out, moe_w_grp, moe_b_grp, moe_w_rt, moe_b_rt, moe_w_gate, moe_w_up, moe_w_down):
    bsz = x.shape[0]
    assert x.shape == (bsz, SEQ, D_MODEL) and ctx.shape == (bsz, CTX_LEN, D_MODEL) and bsz == 2
    depth = ada_w.shape[0]
    assert depth == 2

    xs = jnp.concatenate([x, ctx], axis=1)

    cvec = _pad_rows(jnp.stack([c[0], c_ctx, c[1], c_ctx]), SUBLANES)
    mod_all = _ada_mod(cvec, ada_w, ada_b)
    mod_all = mod_all[:, :2 * bsz].reshape(depth, 2 * bsz, 6, D_MODEL)
    mod_all = jnp.pad(mod_all, ((0, 0), (0, 0), (0, SUBLANES - 6), (0, 0)))

    l = 0
    lam_init = 0.8 - 0.6 * math.exp(-0.3 * l)
    mod = mod_all[l]
    cos_t, sin_t = _rope_tables()
    blk = jnp.arange(DA_WIDTH) // DA_HEAD_DIM
    bd = jnp.where(blk[:, None] == blk[None, :], 1.0 / DA_HEAD_DIM, 0.0).astype(BF16)
    n_rep = DA_WIDTH // DA_HEAD_DIM
    cb, p, q, k, v = _inproj_even(
        xs, mod, norm_mix[l].reshape(1, D_MODEL), ev_w_in[0].astype(BF16),
        jnp.tile(ev_q_norm[0], n_rep).reshape(1, DA_WIDTH), jnp.tile(ev_k_norm[0], n_rep).reshape(1, DA_WIDTH),
        cos_t, sin_t, bd)

    lamv = _pad_rows(jnp.pad(jnp.stack([ev_lam_q1[0], ev_lam_k1[0], ev_lam_q2[0], ev_lam_k2[0]]),
                             ((0, 0), (0, LANES - DA_HEAD_DIM))), SUBLANES)
    subg = ev_sub_norm[0].reshape(1, DA_V_DIM)
    o = _attention(q, k, v, lamv, subg, None, tq=TM, q_start=0, n_q=N_LAT_TILES,
                   kv_start=0, kv_len=TOK, tk=768, lam_init=lam_init)
    o = _attention(q, k, v, lamv, subg, o, tq=TM, q_start=SEQ, n_q=CTX_LEN // TM,
                   kv_start=SEQ, kv_len=CTX_LEN, tk=CTX_LEN, lam_init=lam_init)

    wr, br = _router_params(moe_w_grp[l], moe_b_grp[l], moe_w_rt[l], moe_b_rt[l])
    x1, hf, gate = _post_even(cb, p, _pad_rows(ev_conv_w[0], SUBLANES), ev_conv_b[0].reshape(1, CONV_WIDTH),
                              o, ev_w_out[0].astype(BF16), xs, mod, norm_ffn[l].reshape(1, D_MODEL), wr, br)
    mod2 = mod.reshape(bsz, 2, SUBLANES, D_MODEL)
    xs = _moe_dense(hf, gate, moe_w_gate[l], moe_w_up[l], moe_w_down[l], x1, mod2,
                    tm=768, n_tiles=TOK // 768, out_rows=TOK)

    l = 1
    mod = mod_all[l]
    y, u = _inproj_odd(xs, mod, norm_mix[l].reshape(1, D_MODEL), od_w_in[0].astype(BF16))
    h_dirs = []
    for d in range(2):
        h_dirs.append(_rglru_scan(
            u, _pad_rows(od_conv_w[0, d], SUBLANES), od_conv_b[0, d].reshape(1, LRU_WIDTH),
            od_w_a[0, d].astype(BF16), od_b_a[0, d].reshape(1, LRU_WIDTH),
            od_w_x[0, d].astype(BF16), od_b_x[0, d].reshape(1, LRU_WIDTH),
            od_lam[0, d].reshape(1, LRU_WIDTH), reverse=bool(d)))
    wr, br = _router_params(moe_w_grp[l], moe_b_grp[l], moe_w_rt[l], moe_b_rt[l])
    x1, hf, gate = _post_odd(y, h_dirs[0], h_dirs[1], od_w_out[0].astype(BF16), xs, mod,
                             norm_ffn[l].reshape(1, D_MODEL), wr, br)
    mod2 = mod.reshape(bsz, 2, SUBLANES, D_MODEL)
    return _moe_dense(hf, gate, moe_w_gate[l], moe_w_up[l], moe_w_down[l], x1, mod2,
                      tm=1024, n_tiles=SEQ // 1024, out_rows=SEQ)
```

```python
import functools
import math

import jax
import jax.numpy as jnp
from jax import lax
from jax.experimental import pallas as pl
from jax.experimental.pallas import tpu as pltpu

F32 = jnp.float32
BF16 = jnp.bfloat16
HIGHEST = lax.Precision.HIGHEST

D_MODEL = 1024
SEQ = 8192
CTX_LEN = 256
TOK = SEQ + CTX_LEN
GRID_W = 64
EPS = 1e-6

CONV_WIDTH = 512
DA_HEADS = 4
DA_HEAD_DIM = 64
DA_V_DIM = 2 * DA_HEAD_DIM
DA_WIDTH = DA_HEADS * DA_V_DIM
EVEN_IN = 3 * CONV_WIDTH + 3 * DA_WIDTH
ROPE_BASE = 10000.0

LRU_WIDTH = 1024
LRU_BLOCKS = 8
LRU_BLOCK = LRU_WIDTH // LRU_BLOCKS
LRU_CONV_K = 4
LRU_C = 8.0

N_GROUPS = 4
EXPERTS_PER_GROUP = 4
N_EXPERTS = N_GROUPS * EXPERTS_PER_GROUP
D_EXPERT = 512

LANES = 128
SUBLANES = 8
TM = 256
N_LAT_TILES = SEQ // TM
N_TILES = TOK // TM
NEG = -1e30
MIB = 2 ** 20


def _cparams(semantics, vmem_mib):
    return pltpu.CompilerParams(dimension_semantics=semantics, vmem_limit_bytes=vmem_mib * MIB)


def _sigmoid(x):
    return 1.0 / (1.0 + jnp.exp(-x))


def _norm_mod(x, g, shift, scale):
    ms = jnp.mean(x * x, axis=-1, keepdims=True)
    return (x * lax.rsqrt(ms + EPS) * g) * (1.0 + scale) + shift


ADA_TN = 1536


def _ada_kernel(c_ref, w_ref, b_ref, o_ref):
    c = c_ref[...]
    a = c * _sigmoid(c)
    o_ref[...] = jnp.dot(a, w_ref[...], precision=HIGHEST, preferred_element_type=F32) + b_ref[...]


def _ada_mod(cvec, ada_w, ada_b):
    depth, d, n = ada_w.shape
    return pl.pallas_call(
        _ada_kernel,
        grid=(depth, n // ADA_TN),
        in_specs=[
            pl.BlockSpec((SUBLANES, d), lambda l, j: (0, 0)),
            pl.BlockSpec((None, d, ADA_TN), lambda l, j: (l, 0, j)),
            pl.BlockSpec((None, 1, ADA_TN), lambda l, j: (l, 0, j)),
        ],
        out_specs=pl.BlockSpec((None, SUBLANES, ADA_TN), lambda l, j: (l, 0, j)),
        out_shape=jax.ShapeDtypeStruct((depth, SUBLANES, n), F32),
        compiler_params=_cparams(("arbitrary", "arbitrary"), 40),
        name="ada_mod",
    )(cvec, ada_w, ada_b.reshape(depth, 1, n))


def _mod_index(b, i):
    return (b * 2 + i // N_LAT_TILES, 0, 0)


def _inproj_even_kernel(x_ref, mod_ref, g_ref, w_ref, qg_ref, kg_ref, cos_ref, sin_ref, bd_ref,
                        cb_ref, p_ref, q_ref, k_ref, v_ref):
    h = _norm_mod(x_ref[...], g_ref[...], mod_ref[0:1, :], mod_ref[1:2, :])
    z = jnp.dot(h.astype(BF16), w_ref[...], preferred_element_type=F32)
    cw = CONV_WIDTH
    cb_ref[...] = z[:, :cw]
    p_ref[...] = z[:, cw:2 * cw] * z[:, 2 * cw:3 * cw]

    reps = DA_WIDTH // LANES
    cosf = jnp.concatenate([cos_ref[...]] * reps, axis=1)
    sinf = jnp.concatenate([sin_ref[...]] * reps, axis=1)
    lane = lax.broadcasted_iota(jnp.int32, (TM, DA_WIDTH), 1)
    first_half = (lane & (DA_HEAD_DIM - 1)) < DA_HEAD_DIM // 2
    bd = bd_ref[...]

    def head_norm_rope(t, gain):
        tt = t * t
        hi = tt.astype(BF16)
        lo = (tt - hi.astype(F32)).astype(BF16)
        ms = (jnp.dot(hi, bd, preferred_element_type=F32) + jnp.dot(lo, bd, preferred_element_type=F32))
        y = t * lax.rsqrt(ms + EPS) * gain
        fwd = pltpu.roll(y, DA_WIDTH - DA_HEAD_DIM // 2, axis=1)
        bwd = pltpu.roll(y, DA_HEAD_DIM // 2, axis=1)
        return y * cosf + jnp.where(first_half, fwd, bwd) * sinf

    base = 3 * cw
    q = head_norm_rope(z[:, base:base + DA_WIDTH], qg_ref[...])
    q_ref[...] = (q * (DA_HEAD_DIM ** -0.5 * math.log2(math.e))).astype(BF16)
    k = head_norm_rope(z[:, base + DA_WIDTH:base + 2 * DA_WIDTH], kg_ref[...])
    k_ref[...] = k.astype(BF16)
    v_ref[...] = z[:, base + 2 * DA_WIDTH:].astype(BF16)


def _inproj_even(xs, mod, g, w_in, qg, kg, cos, sin, bd):
    bsz = xs.shape[0]
    tok_spec = lambda width: pl.BlockSpec((None, TM, width), lambda b, i: (b, i, 0))
    const = lambda shape: pl.BlockSpec(shape, lambda b, i: (0,) * len(shape))
    return pl.pallas_call(
        _inproj_even_kernel,
        grid=(bsz, N_TILES),
        in_specs=[
            tok_spec(D_MODEL),
            pl.BlockSpec((None, SUBLANES, D_MODEL), _mod_index),
            const((1, D_MODEL)),
            const((D_MODEL, EVEN_IN)),
            const((1, DA_WIDTH)),
            const((1, DA_WIDTH)),
            pl.BlockSpec((TM, LANES), lambda b, i: (i, 0)),
            pl.BlockSpec((TM, LANES), lambda b, i: (i, 0)),
            const((DA_WIDTH, DA_WIDTH)),
        ],
        out_specs=[tok_spec(CONV_WIDTH), tok_spec(CONV_WIDTH), tok_spec(DA_WIDTH), tok_spec(DA_WIDTH),
                   tok_spec(DA_WIDTH)],
        out_shape=[
            jax.ShapeDtypeStruct((bsz, TOK, CONV_WIDTH), F32),
            jax.ShapeDtypeStruct((bsz, TOK, CONV_WIDTH), F32),
            jax.ShapeDtypeStruct((bsz, TOK, DA_WIDTH), BF16),
            jax.ShapeDtypeStruct((bsz, TOK, DA_WIDTH), BF16),
            jax.ShapeDtypeStruct((bsz, TOK, DA_WIDTH), BF16),
        ],
        compiler_params=_cparams(("parallel", "arbitrary"), 48),
        name="inproj_even",
    )(xs, mod, g, w_in, qg, kg, cos, sin, bd)


def _attn_kernel(lam_ref, subg_ref, q_ref, k_ref, v_ref, o_ref, qz_ref, m_ref, l_ref, acc_ref,
                 sa_ref, sb_ref, *, tq, tk, nk, lam_init):
    q = q_ref[...]
    lane = lax.broadcasted_iota(jnp.int32, (tq, LANES), 1)
    zero = jnp.zeros_like(q)
    qz_ref[0:tq, :] = jnp.where(lane < DA_HEAD_DIM, q, zero)
    qz_ref[tq:2 * tq, :] = jnp.where(lane >= DA_HEAD_DIM, q, zero)
    m_ref[...] = jnp.full(m_ref.shape, NEG, F32)
    l_ref[...] = jnp.zeros(l_ref.shape, F32)
    acc_ref[...] = jnp.zeros(acc_ref.shape, F32)

    def scores(j, dst_ref):
        off = pl.multiple_of(j * tk, tk)
        dst_ref[...] = lax.dot_general(qz_ref[...], k_ref[pl.ds(off, tk), :], (((1,), (1,)), ((), ())),
                                       preferred_element_type=F32)

    def update(j, src_ref):
        off = pl.multiple_of(j * tk, tk)
        s = src_ref[...]
        m_prev = m_ref[...]
        m_new = jnp.maximum(m_prev, jnp.max(s, axis=1, keepdims=True))
        alpha = jnp.exp2(m_prev - m_new)
        p = jnp.exp2(s - m_new[:, :1])
        l_ref[...] = alpha * l_ref[...] + jnp.sum(p, axis=1, keepdims=True)
        acc_ref[...] = alpha * acc_ref[...] + jnp.dot(p.astype(BF16), v_ref[pl.ds(off, tk), :],
                                                      preferred_element_type=F32)
        m_ref[...] = m_new

    scores(0, sa_ref)

    def pair(i, carry):
        j = 2 * i
        scores(j + 1, sb_ref)
        update(j, sa_ref)
        scores(jnp.minimum(j + 2, nk - 1), sa_ref)
        update(j + 1, sb_ref)
        return carry

    lax.fori_loop(0, nk // 2, pair, 0)
    if nk % 2:
        update(nk - 1, sa_ref)

    o = acc_ref[...] / l_ref[...]
    lv = lam_ref[...]
    lam = (jnp.exp(jnp.sum(lv[0:1, :] * lv[1:2, :], axis=1, keepdims=True))
           - jnp.exp(jnp.sum(lv[2:3, :] * lv[3:4, :], axis=1, keepdims=True)) + lam_init)
    d = o[:tq] - lam * o[tq:]
    ms = jnp.mean(d * d, axis=1, keepdims=True)
    o_ref[...] = (d * lax.rsqrt(ms + EPS) * subg_ref[...] * (1.0 - lam_init)).astype(BF16)


def _attention(q, k, v, lamv, subg, o_prev, *, tq, q_start, n_q, kv_start, kv_len, tk, lam_init):
    bsz = q.shape[0]
    q_blk0 = q_start // tq
    kv_blk = kv_start // kv_len
    q_spec = pl.BlockSpec((None, tq, LANES), lambda b, h, i: (b, q_blk0 + i, h))
    kv_spec = pl.BlockSpec((None, kv_len, LANES), lambda b, h, i: (b, kv_blk, h))
    const = lambda shape: pl.BlockSpec(shape, lambda b, h, i: (0,) * len(shape))
    kernel = functools.partial(_attn_kernel, tq=tq, tk=tk, nk=kv_len // tk, lam_init=lam_init)
    in_specs = [const((SUBLANES, LANES)), const((1, LANES)), q_spec, kv_spec, kv_spec]
    args = [lamv, subg, q, k, v]
    aliases = {}
    if o_prev is not None:
        in_specs.append(pl.BlockSpec(memory_space=pl.ANY))
        args.append(o_prev)
        aliases = {5: 0}
        kernel = _drop_extra_input(kernel, 5)
    return pl.pallas_call(
        kernel,
        grid=(bsz, DA_HEADS, n_q),
        in_specs=in_specs,
        out_specs=q_spec,
        out_shape=jax.ShapeDtypeStruct(q.shape, BF16),
        scratch_shapes=[
            pltpu.VMEM((2 * tq, LANES), BF16),
            pltpu.VMEM((2 * tq, LANES), F32),
            pltpu.VMEM((2 * tq, LANES), F32),
            pltpu.VMEM((2 * tq, LANES), F32),
            pltpu.VMEM((2 * tq, tk), F32),
            pltpu.VMEM((2 * tq, tk), F32),
        ],
        input_output_aliases=aliases,
        compiler_params=_cparams(("parallel", "parallel", "arbitrary"), 48),
        name="diff_attn",
    )(*args)


def _drop_extra_input(kernel, pos):
    def wrapped(*refs):
        return kernel(*refs[:pos], *refs[pos + 1:])
    return wrapped


def _post_tail(y, x_ref, mod_ref, g_ref, wr_ref, br_ref, x1_ref, hf_ref, gate_ref):
    x1 = x_ref[...] + mod_ref[2:3, :] * y
    x1_ref[...] = x1
    hf = _norm_mod(x1, g_ref[...], mod_ref[3:4, :], mod_ref[4:5, :])
    hf_ref[...] = hf.astype(BF16)

    logits = jnp.dot(hf, wr_ref[...], precision=HIGHEST, preferred_element_type=F32) + br_ref[...]
    lane = lax.broadcasted_iota(jnp.int32, logits.shape, 1).astype(F32)
    big = float(LANES)
    is_g = (lane >= N_EXPERTS) & (lane < N_EXPERTS + N_GROUPS)
    gl = jnp.where(is_g, logits, NEG)
    gm = jnp.max(gl, axis=1, keepdims=True)
    g_idx = jnp.min(jnp.where(gl == gm, lane, big), axis=1, keepdims=True) - N_EXPERTS
    p_sel = 1.0 / jnp.sum(jnp.exp(gl - gm), axis=1, keepdims=True)
    lo = g_idx * EXPERTS_PER_GROUP
    el = jnp.where((lane >= lo) & (lane < lo + EXPERTS_PER_GROUP), logits, NEG)
    v1 = jnp.max(el, axis=1, keepdims=True)
    i1 = jnp.min(jnp.where(el == v1, lane, big), axis=1, keepdims=True)
    el2 = jnp.where(lane == i1, NEG, el)
    v2 = jnp.max(el2, axis=1, keepdims=True)
    i2 = jnp.min(jnp.where(el2 == v2, lane, big), axis=1, keepdims=True)
    t = jnp.exp(v2 - v1)
    w1 = p_sel / (1.0 + t)
    gate_ref[...] = jnp.where(lane == i1, w1, 0.0) + jnp.where(lane == i2, t * w1, 0.0)


def _post_even_kernel(cb_ref, p_ref, pprev_ref, pnext_ref, cw_ref, cbias_ref, o_ref, w_ref,
                      x_ref, mod_ref, g_ref, wr_ref, br_ref, x1_ref, hf_ref, gate_ref):
    i = pl.program_id(1)
    pc = p_ref[...]
    row = lax.broadcasted_iota(jnp.int32, pc.shape, 0)
    has_prev = jnp.logical_and(i != 0, i != N_LAT_TILES)
    has_next = i < N_LAT_TILES - 1
    prev_row = jnp.where(has_prev, pprev_ref[SUBLANES - 1:SUBLANES, :], 0.0)
    next_row = jnp.where(has_next, pnext_ref[0:1, :], 0.0)
    up = jnp.where(row == 0, prev_row, pltpu.roll(pc, 1, axis=0))
    dn = jnp.where(row == TM - 1, next_row, pltpu.roll(pc, TM - 1, axis=0))
    conv = cbias_ref[...] + cw_ref[0:1, :] * up + cw_ref[1:2, :] * pc + cw_ref[2:3, :] * dn
    out_a = (cb_ref[...] * conv).astype(BF16)
    y = (jnp.dot(out_a, w_ref[0:CONV_WIDTH, :], preferred_element_type=F32)
         + jnp.dot(o_ref[...], w_ref[CONV_WIDTH:, :], preferred_element_type=F32))
    _post_tail(y, x_ref, mod_ref, g_ref, wr_ref, br_ref, x1_ref, hf_ref, gate_ref)


def _post_odd_kernel(y_ref, hf_in_ref, hb_in_ref, w_ref, x_ref, mod_ref, g_ref, wr_ref, br_ref,
                     x1_ref, hf_ref, gate_ref):
    a = (y_ref[...] * (hf_in_ref[...] + hb_in_ref[...])).astype(BF16)
    y = jnp.dot(a, w_ref[...], preferred_element_type=F32)
    _post_tail(y, x_ref, mod_ref, g_ref, wr_ref, br_ref, x1_ref, hf_ref, gate_ref)


def _post_specs(bsz, rows):
    tok_spec = lambda width: pl.BlockSpec((None, TM, width), lambda b, i: (b, i, 0))
    const = lambda shape: pl.BlockSpec(shape, lambda b, i: (0,) * len(shape))
    tail_in = [tok_spec(D_MODEL), pl.BlockSpec((None, SUBLANES, D_MODEL), _mod_index), const((1, D_MODEL)),
               const((D_MODEL, LANES)), const((1, LANES))]
    out_specs = [tok_spec(D_MODEL), tok_spec(D_MODEL), tok_spec(LANES)]
    out_shape = [jax.ShapeDtypeStruct((bsz, rows, D_MODEL), F32),
                 jax.ShapeDtypeStruct((bsz, rows, D_MODEL), BF16),
                 jax.ShapeDtypeStruct((bsz, rows, LANES), F32)]
    return tok_spec, const, tail_in, out_specs, out_shape


def _post_even(cb, p, conv_w, conv_b, o, w_out, xs, mod, g, wr, br):
    bsz = xs.shape[0]
    tok_spec, const, tail_in, out_specs, out_shape = _post_specs(bsz, TOK)
    halo_blocks = TM // SUBLANES
    last_halo = TOK // SUBLANES - 1
    prev_spec = pl.BlockSpec((None, SUBLANES, CONV_WIDTH),
                             lambda b, i: (b, jnp.maximum(i * halo_blocks - 1, 0), 0))
    next_spec = pl.BlockSpec((None, SUBLANES, CONV_WIDTH),
                             lambda b, i: (b, jnp.minimum((i + 1) * halo_blocks, last_halo), 0))
    return pl.pallas_call(
        _post_even_kernel,
        grid=(bsz, N_TILES),
        in_specs=[tok_spec(CONV_WIDTH), tok_spec(CONV_WIDTH), prev_spec, next_spec,
                  const((SUBLANES, CONV_WIDTH)), const((1, CONV_WIDTH)), tok_spec(DA_WIDTH),
                  const((D_MODEL, D_MODEL))] + tail_in,
        out_specs=out_specs,
        out_shape=out_shape,
        compiler_params=_cparams(("parallel", "arbitrary"), 48),
        name="post_even",
    )(cb, p, p, p, conv_w, conv_b, o, w_out, xs, mod, g, wr, br)


def _post_odd(y, hfw, hbw, w_out, xs, mod, g, wr, br):
    bsz = xs.shape[0]
    tok_spec, const, tail_in, out_specs, out_shape = _post_specs(bsz, SEQ)
    return pl.pallas_call(
        _post_odd_kernel,
        grid=(bsz, N_LAT_TILES),
        in_specs=[tok_spec(LRU_WIDTH), tok_spec(LRU_WIDTH), tok_spec(LRU_WIDTH),
                  const((LRU_WIDTH, D_MODEL))] + tail_in,
        out_specs=out_specs,
        out_shape=out_shape,
        compiler_params=_cparams(("parallel", "arbitrary"), 48),
        name="post_odd",
    )(y, hfw, hbw, w_out, xs, mod, g, wr, br)


def _moe_dense_kernel(h_ref, gate_ref, wg_ref, wu_ref, wd_ref, x_ref, mod_ref, o_ref, acc_ref, *, tm):
    i = pl.program_id(1)
    e = pl.program_id(2)

    @pl.when(e == 0)
    def _():
        acc_ref[...] = jnp.zeros(acc_ref.shape, F32)

    h = h_ref[...]
    hg = jnp.dot(h, wg_ref[...].astype(BF16), preferred_element_type=F32)
    hu = jnp.dot(h, wu_ref[...].astype(BF16), preferred_element_type=F32)
    gate = gate_ref[...]
    lane = lax.broadcasted_iota(jnp.int32, gate.shape, 1)
    ge = jnp.sum(jnp.where(lane == e, gate, 0.0), axis=1, keepdims=True)
    act = (hg * _sigmoid(hg)) * hu * ge
    acc_ref[...] += jnp.dot(act.astype(BF16), wd_ref[...].astype(BF16), preferred_element_type=F32)

    @pl.when(e == N_EXPERTS - 1)
    def _():
        rows = i * tm + lax.broadcasted_iota(jnp.int32, (tm, 1), 0)
        g2 = jnp.where(rows >= SEQ, mod_ref[1, 5:6, :], mod_ref[0, 5:6, :])
        o_ref[...] = x_ref[...] + g2 * acc_ref[...]


def _moe_dense(hf, gate, wg, wu, wd, x1, mod2, *, tm, n_tiles, out_rows):
    bsz = hf.shape[0]
    tok_spec = lambda width: pl.BlockSpec((None, tm, width), lambda b, i, e: (b, i, 0))
    return pl.pallas_call(
        functools.partial(_moe_dense_kernel, tm=tm),
        grid=(bsz, n_tiles, N_EXPERTS),
        in_specs=[
            tok_spec(D_MODEL), tok_spec(LANES),
            pl.BlockSpec((None, D_MODEL, D_EXPERT), lambda b, i, e: (e, 0, 0)),
            pl.BlockSpec((None, D_MODEL, D_EXPERT), lambda b, i, e: (e, 0, 0)),
            pl.BlockSpec((None, D_EXPERT, D_MODEL), lambda b, i, e: (e, 0, 0)),
            tok_spec(D_MODEL),
            pl.BlockSpec((None, 2, SUBLANES, D_MODEL), lambda b, i, e: (b, 0, 0, 0)),
        ],
        out_specs=tok_spec(D_MODEL),
        out_shape=jax.ShapeDtypeStruct((bsz, out_rows, D_MODEL), F32),
        scratch_shapes=[pltpu.VMEM((tm, D_MODEL), F32)],
        compiler_params=_cparams(("parallel", "parallel", "arbitrary"), 56),
        name="moe_dense",
    )(hf, gate, wg, wu, wd, x1, mod2)


def _inproj_odd_kernel(x_ref, mod_ref, g_ref, w_ref, y_ref, u_ref):
    h = _norm_mod(x_ref[...], g_ref[...], mod_ref[0:1, :], mod_ref[1:2, :])
    z = jnp.dot(h.astype(BF16), w_ref[...], preferred_element_type=F32)
    zy = z[:, :LRU_WIDTH]
    c0 = math.sqrt(2.0 / math.pi)
    y_ref[...] = 0.5 * zy * (1.0 + jnp.tanh(c0 * (zy + 0.044715 * (zy * zy * zy))))
    u_ref[...] = z[:, LRU_WIDTH:]


def _inproj_odd(xs, mod, g, w_in):
    bsz = xs.shape[0]
    tok_spec = lambda width: pl.BlockSpec((None, TM, width), lambda b, i: (b, i, 0))
    const = lambda shape: pl.BlockSpec(shape, lambda b, i: (0,) * len(shape))
    return pl.pallas_call(
        _inproj_odd_kernel,
        grid=(bsz, N_TILES),
        in_specs=[tok_spec(D_MODEL), pl.BlockSpec((None, SUBLANES, D_MODEL), _mod_index),
                  const((1, D_MODEL)), const((D_MODEL, 2 * LRU_WIDTH))],
        out_specs=[tok_spec(LRU_WIDTH), tok_spec(LRU_WIDTH)],
        out_shape=[jax.ShapeDtypeStruct((bsz, TOK, LRU_WIDTH), F32),
                   jax.ShapeDtypeStruct((bsz, TOK, LRU_WIDTH), F32)],
        compiler_params=_cparams(("parallel", "arbitrary"), 48),
        name="inproj_odd",
    )(xs, mod, g, w_in)


def _scan_kernel(u_ref, cw_ref, cbias_ref, wa_ref, ba_ref, wx_ref, bx_ref, lam_ref, h_ref,
                 halo_ref, carry_ref, a_ref, b_ref, *, reverse):
    i = pl.program_id(1)
    tt, w = u_ref.shape
    n_groups = tt // SUBLANES

    @pl.when(i <= 1)
    def _():
        halo_ref[...] = jnp.zeros(halo_ref.shape, F32)

    @pl.when(i == 0)
    def _():
        carry_ref[...] = jnp.zeros(carry_ref.shape, F32)

    u = u_ref[...]
    halo = halo_ref[...]
    row8 = lax.broadcasted_iota(jnp.int32, (SUBLANES, w), 0)
    k_self = 0 if reverse else LRU_CONV_K - 1
    uc = cbias_ref[...] + cw_ref[k_self:k_self + 1, :] * u
    for k in range(1, LRU_CONV_K):
        if reverse:
            tmp = pltpu.roll(u, tt - k, axis=0)
            hr = pltpu.roll(halo, SUBLANES - k, axis=0)
            edge = jnp.where(row8 >= SUBLANES - k, hr, tmp[tt - SUBLANES:, :])
            shifted = jnp.concatenate([tmp[:tt - SUBLANES, :], edge], axis=0)
            wk = cw_ref[k:k + 1, :]
        else:
            tmp = pltpu.roll(u, k, axis=0)
            hr = pltpu.roll(halo, k, axis=0)
            edge = jnp.where(row8 < k, hr, tmp[:SUBLANES, :])
            shifted = jnp.concatenate([edge, tmp[SUBLANES:, :]], axis=0)
            wk = cw_ref[LRU_CONV_K - 1 - k:LRU_CONV_K - k, :]
        uc = uc + wk * shifted
    halo_ref[...] = u[:SUBLANES, :] if reverse else u[tt - SUBLANES:, :]

    ucb = uc.astype(BF16)

    def block_diag(w_blocks):
        return jnp.concatenate(
            [jnp.dot(ucb[:, j * LRU_BLOCK:(j + 1) * LRU_BLOCK], w_blocks[j], preferred_element_type=F32)
             for j in range(LRU_BLOCKS)], axis=1)

    r = _sigmoid(block_diag(wa_ref) + ba_ref[...])
    gate_i = _sigmoid(block_diag(wx_ref) + bx_ref[...])
    neg_lam = -lam_ref[...]
    softplus = jnp.maximum(neg_lam, 0.0) + jnp.log1p(jnp.exp(-jnp.abs(neg_lam)))
    a = jnp.exp((-LRU_C * softplus) * r)
    a_ref[...] = a
    b_ref[...] = jnp.sqrt(1.0 - a * a) * (gate_i * uc)

    def group(gi, carry):
        g = n_groups - 1 - gi if reverse else gi
        off = pl.multiple_of(g * SUBLANES, SUBLANES)
        av = a_ref[pl.ds(off, SUBLANES), :]
        bv = b_ref[pl.ds(off, SUBLANES), :]
        for s in (1, 2, 4):
            if reverse:
                outside = row8 >= SUBLANES - s
                shift = SUBLANES - s
            else:
                outside = row8 < s
                shift = s
            a_sh = jnp.where(outside, 1.0, pltpu.roll(av, shift, axis=0))
            b_sh = jnp.where(outside, 0.0, pltpu.roll(bv, shift, axis=0))
            bv = av * b_sh + bv
            av = av * a_sh
        h = av * carry + bv
        h_ref[pl.ds(off, SUBLANES), :] = h
        last = h[0:1, :] if reverse else h[SUBLANES - 1:SUBLANES, :]
        return jnp.broadcast_to(last, (SUBLANES, w))

    carry_ref[...] = lax.fori_loop(0, n_groups, group, carry_ref[...])


def _rglru_scan(u, conv_w, conv_b, w_a, b_a, w_x, b_x, lam, *, reverse):
    bsz = u.shape[0]
    if reverse:
        tile_of = lambda i: N_LAT_TILES - i
    else:
        tile_of = lambda i: (i + N_LAT_TILES) % N_TILES
    tok_spec = pl.BlockSpec((None, TM, LRU_WIDTH), lambda b, i: (b, tile_of(i), 0))
    const = lambda shape: pl.BlockSpec(shape, lambda b, i: (0,) * len(shape))
    return pl.pallas_call(
        functools.partial(_scan_kernel, reverse=reverse),
        grid=(bsz, N_TILES),
        in_specs=[tok_spec, const((SUBLANES, LRU_WIDTH)), const((1, LRU_WIDTH)),
                  const((LRU_BLOCKS, LRU_BLOCK, LRU_BLOCK)), const((1, LRU_WIDTH)),
                  const((LRU_BLOCKS, LRU_BLOCK, LRU_BLOCK)), const((1, LRU_WIDTH)), const((1, LRU_WIDTH))],
        out_specs=tok_spec,
        out_shape=jax.ShapeDtypeStruct((bsz, TOK, LRU_WIDTH), F32),
        scratch_shapes=[pltpu.VMEM((SUBLANES, LRU_WIDTH), F32), pltpu.VMEM((SUBLANES, LRU_WIDTH), F32),
                        pltpu.VMEM((TM, LRU_WIDTH), F32), pltpu.VMEM((TM, LRU_WIDTH), F32)],
        compiler_params=_cparams(("parallel", "arbitrary"), 32),
        name="rglru_rev" if reverse else "rglru_fwd",
    )(u, conv_w, conv_b, w_a, b_a, w_x, b_x, lam)


def _pad_rows(a, rows):
    return jnp.pad(a, ((0, rows - a.shape[0]), (0, 0)))


def _rope_tables():
    t = jnp.arange(SEQ)
    n_freq = DA_HEAD_DIM // 4
    inv = ROPE_BASE ** (-jnp.arange(n_freq, dtype=F32) / n_freq)
    ang = jnp.concatenate([(t // GRID_W).astype(F32)[:, None] * inv,
                           (t % GRID_W).astype(F32)[:, None] * inv], axis=-1)
    cos, sin = jnp.cos(ang), jnp.sin(ang)
    cos64 = jnp.concatenate([cos, cos], axis=-1)
    sin64 = jnp.concatenate([-sin, sin], axis=-1)
    cos_t = jnp.concatenate([jnp.tile(cos64, (1, 2)), jnp.ones((CTX_LEN, LANES), F32)], axis=0)
    sin_t = jnp.concatenate([jnp.tile(sin64, (1, 2)), jnp.zeros((CTX_LEN, LANES), F32)], axis=0)
    return cos_t, sin_t


def _router_params(w_grp, b_grp, w_rt, b_rt):
    wr = jnp.concatenate([w_rt.reshape(D_MODEL, N_EXPERTS), w_grp], axis=1)
    br = jnp.concatenate([b_rt.reshape(N_EXPERTS), b_grp])
    pad = LANES - wr.shape[1]
    return jnp.pad(wr, ((0, 0), (0, pad))), jnp.pad(br, (0, pad)).reshape(1, LANES)


def kernel(x, c, ctx, c_ctx, ada_w, ada_b, norm_mix, norm_ffn, ev_w_in, ev_conv_w, ev_conv_b, ev_q_norm, ev_k_norm, ev_lam_q1, ev_lam_k1, ev_lam_q2, ev_lam_k2, ev_sub_norm, ev_w_out, od_w_in, od_conv_w, od_conv_b, od_w_a, od_b_a, od_w_x, od_b_x, od_lam, od_w_out, moe_w_grp, moe_b_grp, moe_w_rt, moe_b_rt, moe_w_gate, moe_w_up, moe_w_down):
    bsz = x.shape[0]
    assert x.shape == (bsz, SEQ, D_MODEL) and ctx.shape == (bsz, CTX_LEN, D_MODEL) and bsz == 2
    depth = ada_w.shape[0]
    assert depth == 2

    xs = jnp.concatenate([x, ctx], axis=1)

    cvec = _pad_rows(jnp.stack([c[0], c_ctx, c[1], c_ctx]), SUBLANES)
    mod_all = _ada_mod(cvec, ada_w, ada_b)
    mod_all = mod_all[:, :2 * bsz].reshape(depth, 2 * bsz, 6, D_MODEL)
    mod_all = jnp.pad(mod_all, ((0, 0), (0, 0), (0, SUBLANES - 6), (0, 0)))

    l = 0
    lam_init = 0.8 - 0.6 * math.exp(-0.3 * l)
    mod = mod_all[l]
    cos_t, sin_t = _rope_tables()
    blk = jnp.arange(DA_WIDTH) // DA_HEAD_DIM
    bd = jnp.where(blk[:, None] == blk[None, :], 1.0 / DA_HEAD_DIM, 0.0).astype(BF16)
    n_rep = DA_WIDTH // DA_HEAD_DIM
    cb, p, q, k, v = _inproj_even(
        xs, mod, norm_mix[l].reshape(1, D_MODEL), ev_w_in[0].astype(BF16),
        jnp.tile(ev_q_norm[0], n_rep).reshape(1, DA_WIDTH), jnp.tile(ev_k_norm[0], n_rep).reshape(1, DA_WIDTH),
        cos_t, sin_t, bd)

    lamv = _pad_rows(jnp.pad(jnp.stack([ev_lam_q1[0], ev_lam_k1[0], ev_lam_q2[0], ev_lam_k2[0]]),
                             ((0, 0), (0, LANES - DA_HEAD_DIM))), SUBLANES)
    subg = ev_sub_norm[0].reshape(1, DA_V_DIM)
    o = _attention(q, k, v, lamv, subg, None, tq=256, q_start=0, n_q=SEQ // 256,
                   kv_start=0, kv_len=TOK, tk=768, lam_init=lam_init)
    o = _attention(q, k, v, lamv, subg, o, tq=TM, q_start=SEQ, n_q=CTX_LEN // TM,
                   kv_start=SEQ, kv_len=CTX_LEN, tk=CTX_LEN, lam_init=lam_init)

    wr, br = _router_params(moe_w_grp[l], moe_b_grp[l], moe_w_rt[l], moe_b_rt[l])
    x1, hf, gate = _post_even(cb, p, _pad_rows(ev_conv_w[0], SUBLANES), ev_conv_b[0].reshape(1, CONV_WIDTH),
                              o, ev_w_out[0].astype(BF16), xs, mod, norm_ffn[l].reshape(1, D_MODEL), wr, br)
    mod2 = mod.reshape(bsz, 2, SUBLANES, D_MODEL)
    xs = _moe_dense(hf, gate, moe_w_gate[l], moe_w_up[l], moe_w_down[l], x1, mod2,
                    tm=768, n_tiles=TOK // 768, out_rows=TOK)

    l = 1
    mod = mod_all[l]
    y, u = _inproj_odd(xs, mod, norm_mix[l].reshape(1, D_MODEL), od_w_in[0].astype(BF16))
    h_dirs = []
    for d in range(2):
        h_dirs.append(_rglru_scan(
            u, _pad_rows(od_conv_w[0, d], SUBLANES), od_conv_b[0, d].reshape(1, LRU_WIDTH),
            od_w_a[0, d].astype(BF16), od_b_a[0, d].reshape(1, LRU_WIDTH),
            od_w_x[0, d].astype(BF16), od_b_x[0, d].reshape(1, LRU_WIDTH),
            od_lam[0, d].reshape(1, LRU_WIDTH), reverse=bool(d)))
    wr, br = _router_params(moe_w_grp[l], moe_b_grp[l], moe_w_rt[l], moe_b_rt[l])
    x1, hf, gate = _post_odd(y, h_dirs[0], h_dirs[1], od_w_out[0].astype(BF16), xs, mod,
                             norm_ffn[l].reshape(1, D_MODEL), wr, br)
    mod2 = mod.reshape(bsz, 2, SUBLANES, D_MODEL)
    return _moe_dense(hf, gate, moe_w_gate[l], moe_w_up[l], moe_w_down[l], x1, mod2,
                      tm=1024, n_tiles=SEQ // 1024, out_rows=SEQ)
```

```python
import functools
import math

import jax
import jax.numpy as jnp
from jax import lax
from jax.experimental import pallas as pl
from jax.experimental.pallas import tpu as pltpu

F32 = jnp.float32
BF16 = jnp.bfloat16
HIGHEST = lax.Precision.HIGHEST

D_MODEL = 1024
SEQ = 8192
CTX_LEN = 256
TOK = SEQ + CTX_LEN
GRID_W = 64
EPS = 1e-6

CONV_WIDTH = 512
DA_HEADS = 4
DA_HEAD_DIM = 64
DA_V_DIM = 2 * DA_HEAD_DIM
DA_WIDTH = DA_HEADS * DA_V_DIM
EVEN_IN = 3 * CONV_WIDTH + 3 * DA_WIDTH
ROPE_BASE = 10000.0

LRU_WIDTH = 1024
LRU_BLOCKS = 8
LRU_BLOCK = LRU_WIDTH // LRU_BLOCKS
LRU_CONV_K = 4
LRU_C = 8.0

N_GROUPS = 4
EXPERTS_PER_GROUP = 4
N_EXPERTS = N_GROUPS * EXPERTS_PER_GROUP
D_EXPERT = 512

LANES = 128
SUBLANES = 8
TM = 256
N_LAT_TILES = SEQ // TM
N_TILES = TOK // TM
NEG = -1e30
MIB = 2 ** 20


def _cparams(semantics, vmem_mib):
    return pltpu.CompilerParams(dimension_semantics=semantics, vmem_limit_bytes=vmem_mib * MIB)


def _sigmoid(x):
    return 1.0 / (1.0 + jnp.exp(-x))


def _norm_mod(x, g, shift, scale):
    ms = jnp.mean(x * x, axis=-1, keepdims=True)
    return (x * lax.rsqrt(ms + EPS) * g) * (1.0 + scale) + shift


ADA_TN = 1536


def _ada_kernel(c_ref, w_ref, b_ref, o_ref):
    c = c_ref[...]
    a = c * _sigmoid(c)
    o_ref[...] = jnp.dot(a, w_ref[...], precision=HIGHEST, preferred_element_type=F32) + b_ref[...]


def _ada_mod(cvec, ada_w, ada_b):
    depth, d, n = ada_w.shape
    return pl.pallas_call(
        _ada_kernel,
        grid=(depth, n // ADA_TN),
        in_specs=[
            pl.BlockSpec((SUBLANES, d), lambda l, j: (0, 0)),
            pl.BlockSpec((None, d, ADA_TN), lambda l, j: (l, 0, j)),
            pl.BlockSpec((None, 1, ADA_TN), lambda l, j: (l, 0, j)),
        ],
        out_specs=pl.BlockSpec((None, SUBLANES, ADA_TN), lambda l, j: (l, 0, j)),
        out_shape=jax.ShapeDtypeStruct((depth, SUBLANES, n), F32),
        compiler_params=_cparams(("arbitrary", "arbitrary"), 40),
        name="ada_mod",
    )(cvec, ada_w, ada_b.reshape(depth, 1, n))


def _mod_index(b, i):
    return (b * 2 + i // N_LAT_TILES, 0, 0)


def _inproj_even_kernel(x_ref, mod_ref, g_ref, w_ref, qg_ref, kg_ref, cos_ref, sin_ref, bd_ref,
                        cb_ref, p_ref, q_ref, k_ref, v_ref):
    h = _norm_mod(x_ref[...], g_ref[...], mod_ref[0:1, :], mod_ref[1:2, :])
    z = jnp.dot(h.astype(BF16), w_ref[...], preferred_element_type=F32)
    cw = CONV_WIDTH
    cb_ref[...] = z[:, :cw]
    p_ref[...] = z[:, cw:2 * cw] * z[:, 2 * cw:3 * cw]

    reps = DA_WIDTH // LANES
    cosf = jnp.concatenate([cos_ref[...]] * reps, axis=1)
    sinf = jnp.concatenate([sin_ref[...]] * reps, axis=1)
    lane = lax.broadcasted_iota(jnp.int32, (TM, DA_WIDTH), 1)
    first_half = (lane & (DA_HEAD_DIM - 1)) < DA_HEAD_DIM // 2
    bd = bd_ref[...]

    def head_norm_rope(t, gain):
        tt = t * t
        hi = tt.astype(BF16)
        lo = (tt - hi.astype(F32)).astype(BF16)
        ms = (jnp.dot(hi, bd, preferred_element_type=F32) + jnp.dot(lo, bd, preferred_element_type=F32))
        y = t * lax.rsqrt(ms + EPS) * gain
        fwd = pltpu.roll(y, DA_WIDTH - DA_HEAD_DIM // 2, axis=1)
        bwd = pltpu.roll(y, DA_HEAD_DIM // 2, axis=1)
        return y * cosf + jnp.where(first_half, fwd, bwd) * sinf

    base = 3 * cw
    q = head_norm_rope(z[:, base:base + DA_WIDTH], qg_ref[...])
    q_ref[...] = (q * (DA_HEAD_DIM ** -0.5 * math.log2(math.e))).astype(BF16)
    k = head_norm_rope(z[:, base + DA_WIDTH:base + 2 * DA_WIDTH], kg_ref[...])
    k_ref[...] = k.astype(BF16)
    v_ref[...] = z[:, base + 2 * DA_WIDTH:].astype(BF16)


def _inproj_even(xs, mod, g, w_in, qg, kg, cos, sin, bd):
    bsz = xs.shape[0]
    tok_spec = lambda width: pl.BlockSpec((None, TM, width), lambda b, i: (b, i, 0))
    const = lambda shape: pl.BlockSpec(shape, lambda b, i: (0,) * len(shape))
    return pl.pallas_call(
        _inproj_even_kernel,
        grid=(bsz, N_TILES),
        in_specs=[
            tok_spec(D_MODEL),
            pl.BlockSpec((None, SUBLANES, D_MODEL), _mod_index),
            const((1, D_MODEL)),
            const((D_MODEL, EVEN_IN)),
            const((1, DA_WIDTH)),
            const((1, DA_WIDTH)),
            pl.BlockSpec((TM, LANES), lambda b, i: (i, 0)),
            pl.BlockSpec((TM, LANES), lambda b, i: (i, 0)),
            const((DA_WIDTH, DA_WIDTH)),
        ],
        out_specs=[tok_spec(CONV_WIDTH), tok_spec(CONV_WIDTH), tok_spec(DA_WIDTH), tok_spec(DA_WIDTH),
                   tok_spec(DA_WIDTH)],
        out_shape=[
            jax.ShapeDtypeStruct((bsz, TOK, CONV_WIDTH), F32),
            jax.ShapeDtypeStruct((bsz, TOK, CONV_WIDTH), F32),
            jax.ShapeDtypeStruct((bsz, TOK, DA_WIDTH), BF16),
            jax.ShapeDtypeStruct((bsz, TOK, DA_WIDTH), BF16),
            jax.ShapeDtypeStruct((bsz, TOK, DA_WIDTH), BF16),
        ],
        compiler_params=_cparams(("parallel", "arbitrary"), 48),
        name="inproj_even",
    )(xs, mod, g, w_in, qg, kg, cos, sin, bd)


def _attn_kernel(lam_ref, subg_ref, q_ref, k_ref, v_ref, o_ref, qz_ref, m_ref, l_ref, acc_ref,
                 sa_ref, sb_ref, *, tq, tk, nk, lam_init):
    q = q_ref[...]
    lane = lax.broadcasted_iota(jnp.int32, (tq, LANES), 1)
    zero = jnp.zeros_like(q)
    qz_ref[0:tq, :] = jnp.where(lane < DA_HEAD_DIM, q, zero)
    qz_ref[tq:2 * tq, :] = jnp.where(lane >= DA_HEAD_DIM, q, zero)
    m_ref[...] = jnp.full(m_ref.shape, NEG, F32)
    l_ref[...] = jnp.zeros(l_ref.shape, F32)
    acc_ref[...] = jnp.zeros(acc_ref.shape, F32)

    def scores(j, dst_ref):
        off = pl.multiple_of(j * tk, tk)
        dst_ref[...] = lax.dot_general(qz_ref[...], k_ref[pl.ds(off, tk), :], (((1,), (1,)), ((), ())),
                                       preferred_element_type=F32)

    def update(j, src_ref):
        off = pl.multiple_of(j * tk, tk)
        s = src_ref[...]
        m_prev = m_ref[...]
        m_new = jnp.maximum(m_prev, jnp.max(s, axis=1, keepdims=True))
        alpha = jnp.exp2(m_prev - m_new)
        p = jnp.exp2(s - m_new[:, :1])
        l_ref[...] = alpha * l_ref[...] + jnp.sum(p, axis=1, keepdims=True)
        acc_ref[...] = alpha * acc_ref[...] + jnp.dot(p.astype(BF16), v_ref[pl.ds(off, tk), :],
                                                      preferred_element_type=F32)
        m_ref[...] = m_new

    scores(0, sa_ref)

    def pair(i, carry):
        j = 2 * i
        scores(j + 1, sb_ref)
        update(j, sa_ref)
        scores(jnp.minimum(j + 2, nk - 1), sa_ref)
        update(j + 1, sb_ref)
        return carry

    lax.fori_loop(0, nk // 2, pair, 0)
    if nk % 2:
        update(nk - 1, sa_ref)

    o = acc_ref[...] / l_ref[...]
    lv = lam_ref[...]
    lam = (jnp.exp(jnp.sum(lv[0:1, :] * lv[1:2, :], axis=1, keepdims=True))
           - jnp.exp(jnp.sum(lv[2:3, :] * lv[3:4, :], axis=1, keepdims=True)) + lam_init)
    d = o[:tq] - lam * o[tq:]
    ms = jnp.mean(d * d, axis=1, keepdims=True)
    o_ref[...] = (d * lax.rsqrt(ms + EPS) * subg_ref[...] * (1.0 - lam_init)).astype(BF16)


def _attention(q, k, v, lamv, subg, *, tq, q_start, n_q, kv_start, kv_len, tk, lam_init):
    bsz = q.shape[0]
    q_blk0 = q_start // tq
    kv_blk = kv_start // kv_len
    q_spec = pl.BlockSpec((None, tq, LANES), lambda b, h, i: (b, q_blk0 + i, h))
    o_spec = pl.BlockSpec((None, tq, LANES), lambda b, h, i: (b, i, h))
    kv_spec = pl.BlockSpec((None, kv_len, LANES), lambda b, h, i: (b, kv_blk, h))
    const = lambda shape: pl.BlockSpec(shape, lambda b, h, i: (0,) * len(shape))
    return pl.pallas_call(
        functools.partial(_attn_kernel, tq=tq, tk=tk, nk=kv_len // tk, lam_init=lam_init),
        grid=(bsz, DA_HEADS, n_q),
        in_specs=[const((SUBLANES, LANES)), const((1, LANES)), q_spec, kv_spec, kv_spec],
        out_specs=o_spec,
        out_shape=jax.ShapeDtypeStruct((bsz, n_q * tq, DA_WIDTH), BF16),
        scratch_shapes=[
            pltpu.VMEM((2 * tq, LANES), BF16),
            pltpu.VMEM((2 * tq, LANES), F32),
            pltpu.VMEM((2 * tq, LANES), F32),
            pltpu.VMEM((2 * tq, LANES), F32),
            pltpu.VMEM((2 * tq, tk), F32),
            pltpu.VMEM((2 * tq, tk), F32),
        ],
        compiler_params=_cparams(("parallel", "parallel", "arbitrary"), 48),
        name="diff_attn",
    )(lamv, subg, q, k, v)


PAIRS_PER_GROUP = EXPERTS_PER_GROUP * (EXPERTS_PER_GROUP - 1) // 2
N_BUCKETS = N_GROUPS * PAIRS_PER_GROUP
META_BUCKET, META_RANK, META_W_LO, META_W_HI = 0, 1, 2, 3


def _post_tail(y, first_step, x_ref, mod_ref, g_ref, wr_ref, br_ref, x1_ref, hf_ref, meta_ref, cnt_ref,
               cnt_scr):
    x1 = x_ref[...] + mod_ref[2:3, :] * y
    x1_ref[...] = x1
    hf = _norm_mod(x1, g_ref[...], mod_ref[3:4, :], mod_ref[4:5, :])
    hf_ref[...] = hf

    logits = jnp.dot(hf, wr_ref[...], precision=HIGHEST, preferred_element_type=F32) + br_ref[...]
    lane = lax.broadcasted_iota(jnp.int32, logits.shape, 1).astype(F32)
    big = float(LANES)
    is_g = (lane >= N_EXPERTS) & (lane < N_EXPERTS + N_GROUPS)
    gl = jnp.where(is_g, logits, NEG)
    gm = jnp.max(gl, axis=1, keepdims=True)
    g_idx = jnp.min(jnp.where(gl == gm, lane, big), axis=1, keepdims=True) - N_EXPERTS
    p_sel = 1.0 / jnp.sum(jnp.exp(gl - gm), axis=1, keepdims=True)
    lo = g_idx * EXPERTS_PER_GROUP
    el = jnp.where((lane >= lo) & (lane < lo + EXPERTS_PER_GROUP), logits, NEG)
    v1 = jnp.max(el, axis=1, keepdims=True)
    i1 = jnp.min(jnp.where(el == v1, lane, big), axis=1, keepdims=True)
    el2 = jnp.where(lane == i1, NEG, el)
    v2 = jnp.max(el2, axis=1, keepdims=True)
    i2 = jnp.min(jnp.where(el2 == v2, lane, big), axis=1, keepdims=True)
    t = jnp.exp(v2 - v1)
    w1 = p_sel / (1.0 + t)
    w2 = t * w1

    first_lower = i1 < i2
    a = jnp.minimum(i1, i2) - lo
    b = jnp.maximum(i1, i2) - lo
    bucket = g_idx * PAIRS_PER_GROUP + a * (7.0 - a) * 0.5 + (b - a - 1.0)
    w_lo = jnp.where(first_lower, w1, w2)
    w_hi = jnp.where(first_lower, w2, w1)

    @pl.when(first_step)
    def _():
        cnt_scr[...] = jnp.zeros(cnt_scr.shape, F32)

    tm = logits.shape[0]
    onehot = jnp.where(lane == bucket, 1.0, 0.0)
    r_i = lax.broadcasted_iota(jnp.int32, (tm, tm), 0)
    c_i = lax.broadcasted_iota(jnp.int32, (tm, tm), 1)
    earlier = jnp.where(c_i < r_i, 1.0, 0.0).astype(BF16)
    prefix = jnp.dot(earlier, onehot.astype(BF16), preferred_element_type=F32)
    base = cnt_scr[0:1, :]
    rank = jnp.sum(onehot * (prefix + base), axis=1, keepdims=True)
    counts = jnp.broadcast_to(base + jnp.sum(onehot, axis=0, keepdims=True), cnt_scr.shape)
    cnt_scr[...] = counts
    cnt_ref[...] = counts
    meta_ref[...] = (jnp.where(lane == META_BUCKET, bucket, 0.0) + jnp.where(lane == META_RANK, rank, 0.0)
                     + jnp.where(lane == META_W_LO, w_lo, 0.0) + jnp.where(lane == META_W_HI, w_hi, 0.0))


def _post_even_kernel(cb_ref, p_ref, pprev_ref, pnext_ref, cw_ref, cbias_ref, olat_ref, octx_ref, w_ref,
                      x_ref, mod_ref, g_ref, wr_ref, br_ref, x1_ref, hf_ref, meta_ref, cnt_ref, cnt_scr):
    i = pl.program_id(1)
    first_step = jnp.logical_and(pl.program_id(0) == 0, i == 0)
    pc = p_ref[...]
    row = lax.broadcasted_iota(jnp.int32, pc.shape, 0)
    has_prev = jnp.logical_and(i != 0, i != N_LAT_TILES)
    has_next = i < N_LAT_TILES - 1
    prev_row = jnp.where(has_prev, pprev_ref[SUBLANES - 1:SUBLANES, :], 0.0)
    next_row = jnp.where(has_next, pnext_ref[0:1, :], 0.0)
    up = jnp.where(row == 0, prev_row, pltpu.roll(pc, 1, axis=0))
    dn = jnp.where(row == TM - 1, next_row, pltpu.roll(pc, TM - 1, axis=0))
    conv = cbias_ref[...] + cw_ref[0:1, :] * up + cw_ref[1:2, :] * pc + cw_ref[2:3, :] * dn
    out_a = (cb_ref[...] * conv).astype(BF16)
    o = jnp.where(i == N_LAT_TILES, octx_ref[...], olat_ref[...])
    y = (jnp.dot(out_a, w_ref[0:CONV_WIDTH, :], preferred_element_type=F32)
         + jnp.dot(o, w_ref[CONV_WIDTH:, :], preferred_element_type=F32))
    _post_tail(y, first_step, x_ref, mod_ref, g_ref, wr_ref, br_ref, x1_ref, hf_ref, meta_ref, cnt_ref, cnt_scr)


def _post_odd_kernel(y_ref, hf_in_ref, hb_in_ref, w_ref, x_ref, mod_ref, g_ref, wr_ref, br_ref,
                     x1_ref, hf_ref, meta_ref, cnt_ref, cnt_scr):
    first_step = jnp.logical_and(pl.program_id(0) == 0, pl.program_id(1) == 0)
    a = (y_ref[...] * (hf_in_ref[...] + hb_in_ref[...])).astype(BF16)
    y = jnp.dot(a, w_ref[...], preferred_element_type=F32)
    _post_tail(y, first_step, x_ref, mod_ref, g_ref, wr_ref, br_ref, x1_ref, hf_ref, meta_ref, cnt_ref, cnt_scr)


def _post_specs(bsz, rows):
    tok_spec = lambda width: pl.BlockSpec((None, TM, width), lambda b, i: (b, i, 0))
    const = lambda shape: pl.BlockSpec(shape, lambda b, i: (0,) * len(shape))
    tail_in = [tok_spec(D_MODEL), pl.BlockSpec((None, SUBLANES, D_MODEL), _mod_index), const((1, D_MODEL)),
               const((D_MODEL, LANES)), const((1, LANES))]
    out_specs = [tok_spec(D_MODEL), tok_spec(D_MODEL), tok_spec(LANES), const((SUBLANES, LANES))]
    out_shape = [jax.ShapeDtypeStruct((bsz, rows, D_MODEL), F32),
                 jax.ShapeDtypeStruct((bsz, rows, D_MODEL), F32),
                 jax.ShapeDtypeStruct((bsz, rows, LANES), F32),
                 jax.ShapeDtypeStruct((SUBLANES, LANES), F32)]
    scratch = [pltpu.VMEM((SUBLANES, LANES), F32)]
    return tok_spec, const, tail_in, out_specs, out_shape, scratch


def _post_even(cb, p, conv_w, conv_b, o_lat, o_ctx, w_out, xs, mod, g, wr, br):
    bsz = xs.shape[0]
    tok_spec, const, tail_in, out_specs, out_shape, scratch = _post_specs(bsz, TOK)
    olat_spec = pl.BlockSpec((None, TM, DA_WIDTH), lambda b, i: (b, jnp.minimum(i, N_LAT_TILES - 1), 0))
    octx_spec = pl.BlockSpec((None, TM, DA_WIDTH), lambda b, i: (b, 0, 0))
    halo_blocks = TM // SUBLANES
    last_halo = TOK // SUBLANES - 1
    prev_spec = pl.BlockSpec((None, SUBLANES, CONV_WIDTH),
                             lambda b, i: (b, jnp.maximum(i * halo_blocks - 1, 0), 0))
    next_spec = pl.BlockSpec((None, SUBLANES, CONV_WIDTH),
                             lambda b, i: (b, jnp.minimum((i + 1) * halo_blocks, last_halo), 0))
    return pl.pallas_call(
        _post_even_kernel,
        grid=(bsz, N_TILES),
        in_specs=[tok_spec(CONV_WIDTH), tok_spec(CONV_WIDTH), prev_spec, next_spec,
                  const((SUBLANES, CONV_WIDTH)), const((1, CONV_WIDTH)), olat_spec, octx_spec,
                  const((D_MODEL, D_MODEL))] + tail_in,
        out_specs=out_specs,
        out_shape=out_shape,
        scratch_shapes=scratch,
        compiler_params=_cparams(("arbitrary", "arbitrary"), 48),
        name="post_even",
    )(cb, p, p, p, conv_w, conv_b, o_lat, o_ctx, w_out, xs, mod, g, wr, br)


def _post_odd(y, hfw, hbw, w_out, xs, mod, g, wr, br):
    bsz = xs.shape[0]
    tok_spec, const, tail_in, out_specs, out_shape, scratch = _post_specs(bsz, SEQ)
    return pl.pallas_call(
        _post_odd_kernel,
        grid=(bsz, N_LAT_TILES),
        in_specs=[tok_spec(LRU_WIDTH), tok_spec(LRU_WIDTH), tok_spec(LRU_WIDTH),
                  const((LRU_WIDTH, D_MODEL))] + tail_in,
        out_specs=out_specs,
        out_shape=out_shape,
        scratch_shapes=scratch,
        compiler_params=_cparams(("arbitrary", "arbitrary"), 48),
        name="post_odd",
    )(y, hfw, hbw, w_out, xs, mod, g, wr, br)


_PAIRS = [(a, b) for a in range(EXPERTS_PER_GROUP) for b in range(a + 1, EXPERTS_PER_GROUP)]
_BUCKET_LO = [g * EXPERTS_PER_GROUP + a for g in range(N_GROUPS) for a, _ in _PAIRS]
_BUCKET_HI = [g * EXPERTS_PER_GROUP + b for g in range(N_GROUPS) for _, b in _PAIRS]


def _sorted_tiles(n_tokens):
    return n_tokens // TM + N_BUCKETS


def _route_plan(meta, counts, n_tokens):
    n_tiles = _sorted_tiles(n_tokens)
    bucket = meta[..., META_BUCKET].astype(jnp.int32).reshape(n_tokens)
    rank = meta[..., META_RANK].astype(jnp.int32).reshape(n_tokens)
    cnt = counts[0, :N_BUCKETS].astype(jnp.int32)
    tiles_per = (cnt + TM - 1) // TM
    tile_end = jnp.cumsum(tiles_per)
    row_start = (tile_end - tiles_per) * TM
    dest = (row_start[bucket] + rank).reshape(n_tokens // TM, 1, TM)
    tile_bucket = jnp.minimum(jnp.sum(jnp.arange(n_tiles)[:, None] >= tile_end[None, :], axis=1), N_BUCKETS - 1)
    e_lo = jnp.asarray(_BUCKET_LO, jnp.int32)[tile_bucket]
    e_hi = jnp.asarray(_BUCKET_HI, jnp.int32)[tile_bucket]
    return dest, e_lo, e_hi, tile_end[-1:].astype(jnp.int32)


def _row_copies_wait(src_rows_ref, dst_rows_ref, sem):
    pltpu.make_async_copy(src_rows_ref, dst_rows_ref, sem).wait()


def _dispatch_kernel(dest_ref, hf_hbm, hs_in_hbm, hs_hbm, sem):
    del hs_in_hbm
    base = pl.program_id(0) * TM

    def issue(r, carry):
        pltpu.make_async_copy(hf_hbm.at[pl.ds(base + r, 1), :],
                              hs_hbm.at[pl.ds(dest_ref[0, r], 1), :], sem).start()
        return carry

    lax.fori_loop(0, TM, issue, 0, unroll=8)
    _row_copies_wait(hf_hbm.at[pl.ds(0, TM), :], hs_hbm.at[pl.ds(0, TM), :], sem)


def _dispatch(dest, hf, n_tokens):
    rows = _sorted_tiles(n_tokens) * TM
    hs0 = jnp.zeros((rows, D_MODEL), F32)
    return pl.pallas_call(
        _dispatch_kernel,
        grid=(n_tokens // TM,),
        in_specs=[pl.BlockSpec((None, 1, TM), lambda i: (i, 0, 0), memory_space=pltpu.SMEM),
                  pl.BlockSpec(memory_space=pl.ANY), pl.BlockSpec(memory_space=pl.ANY)],
        out_specs=pl.BlockSpec(memory_space=pl.ANY),
        out_shape=jax.ShapeDtypeStruct((rows, D_MODEL), F32),
        scratch_shapes=[pltpu.SemaphoreType.DMA(())],
        input_output_aliases={2: 0},
        compiler_params=_cparams(("arbitrary",), 16),
        name="moe_dispatch",
    )(dest, hf.reshape(n_tokens, D_MODEL), hs0)


def _moe_routed_kernel(elo_ref, ehi_ref, nused_ref, h_ref, wg_lo, wu_lo, wd_lo, wg_hi, wu_hi, wd_hi, y_ref):
    del elo_ref, ehi_ref

    @pl.when(pl.program_id(0) < nused_ref[0])
    def _():
        h = h_ref[...].astype(BF16)

        def expert(wg_ref, wu_ref, wd_ref):
            hg = jnp.dot(h, wg_ref[...].astype(BF16), preferred_element_type=F32)
            hu = jnp.dot(h, wu_ref[...].astype(BF16), preferred_element_type=F32)
            act = (hg * _sigmoid(hg)) * hu
            return jnp.dot(act.astype(BF16), wd_ref[...].astype(BF16), preferred_element_type=F32)

        y_ref[:, :D_MODEL] = expert(wg_lo, wu_lo, wd_lo)
        y_ref[:, D_MODEL:] = expert(wg_hi, wu_hi, wd_hi)

    @pl.when(pl.program_id(0) >= nused_ref[0])
    def _():
        y_ref[...] = jnp.zeros(y_ref.shape, F32)


def _moe_routed(hs, e_lo, e_hi, n_used, wg, wu, wd):
    n_tiles = hs.shape[0] // TM
    up_spec = lambda tbl: pl.BlockSpec((None, D_MODEL, D_EXPERT), lambda j, lo, hi, nu: ((lo, hi)[tbl][j], 0, 0))
    dn_spec = lambda tbl: pl.BlockSpec((None, D_EXPERT, D_MODEL), lambda j, lo, hi, nu: ((lo, hi)[tbl][j], 0, 0))
    grid_spec = pltpu.PrefetchScalarGridSpec(
        num_scalar_prefetch=3,
        grid=(n_tiles,),
        in_specs=[pl.BlockSpec((TM, D_MODEL), lambda j, lo, hi, nu: (j, 0)),
                  up_spec(0), up_spec(0), dn_spec(0), up_spec(1), up_spec(1), dn_spec(1)],
        out_specs=pl.BlockSpec((TM, 2 * D_MODEL), lambda j, lo, hi, nu: (j, 0)),
    )
    return pl.pallas_call(
        _moe_routed_kernel,
        grid_spec=grid_spec,
        out_shape=jax.ShapeDtypeStruct((hs.shape[0], 2 * D_MODEL), F32),
        compiler_params=_cparams(("arbitrary",), 56),
        name="moe_routed",
    )(e_lo, e_hi, n_used, hs, wg, wu, wd, wg, wu, wd)


def _combine_kernel(dest_ref, y_hbm, x_ref, meta_ref, mod_ref, o_ref, buf_ref, sem):
    def issue(r, carry):
        pltpu.make_async_copy(y_hbm.at[pl.ds(dest_ref[0, r], 1), :], buf_ref.at[pl.ds(r, 1), :], sem).start()
        return carry

    lax.fori_loop(0, TM, issue, 0, unroll=8)
    _row_copies_wait(y_hbm.at[pl.ds(0, TM), :], buf_ref, sem)
    meta = meta_ref[...]
    moe = (meta[:, META_W_LO:META_W_LO + 1] * buf_ref[:, :D_MODEL]
           + meta[:, META_W_HI:META_W_HI + 1] * buf_ref[:, D_MODEL:])
    o_ref[...] = x_ref[...] + mod_ref[5:6, :] * moe


def _combine(dest, y, x1, meta, mod, *, tiles_per_batch):
    bsz = x1.shape[0]
    tok_spec = lambda width: pl.BlockSpec((None, TM, width), lambda b, i: (b, i, 0))
    return pl.pallas_call(
        _combine_kernel,
        grid=(bsz, tiles_per_batch),
        in_specs=[pl.BlockSpec((None, 1, TM), lambda b, i: (b * tiles_per_batch + i, 0, 0),
                               memory_space=pltpu.SMEM),
                  pl.BlockSpec(memory_space=pl.ANY), tok_spec(D_MODEL), tok_spec(LANES),
                  pl.BlockSpec((None, SUBLANES, D_MODEL), _mod_index)],
        out_specs=tok_spec(D_MODEL),
        out_shape=jax.ShapeDtypeStruct(x1.shape, F32),
        scratch_shapes=[pltpu.VMEM((TM, 2 * D_MODEL), F32), pltpu.SemaphoreType.DMA(())],
        compiler_params=_cparams(("arbitrary", "arbitrary"), 32),
        name="moe_combine",
    )(dest, y, x1, meta, mod)


def _moe(hf, meta, counts, x1, mod, wg, wu, wd, *, tiles_per_batch):
    n_tokens = hf.shape[0] * hf.shape[1]
    dest, e_lo, e_hi, n_used = _route_plan(meta, counts, n_tokens)
    hs = _dispatch(dest, hf, n_tokens)
    y = _moe_routed(hs, e_lo, e_hi, n_used, wg, wu, wd)
    return _combine(dest, y, x1, meta, mod, tiles_per_batch=tiles_per_batch)


def _inproj_odd_kernel(x_ref, mod_ref, g_ref, w_ref, y_ref, u_ref):
    h = _norm_mod(x_ref[...], g_ref[...], mod_ref[0:1, :], mod_ref[1:2, :])
    z = jnp.dot(h.astype(BF16), w_ref[...], preferred_element_type=F32)
    zy = z[:, :LRU_WIDTH]
    c0 = math.sqrt(2.0 / math.pi)
    y_ref[...] = 0.5 * zy * (1.0 + jnp.tanh(c0 * (zy + 0.044715 * (zy * zy * zy))))
    u_ref[...] = z[:, LRU_WIDTH:]


def _inproj_odd(xs, mod, g, w_in):
    bsz = xs.shape[0]
    tok_spec = lambda width: pl.BlockSpec((None, TM, width), lambda b, i: (b, i, 0))
    const = lambda shape: pl.BlockSpec(shape, lambda b, i: (0,) * len(shape))
    return pl.pallas_call(
        _inproj_odd_kernel,
        grid=(bsz, N_TILES),
        in_specs=[tok_spec(D_MODEL), pl.BlockSpec((None, SUBLANES, D_MODEL), _mod_index),
                  const((1, D_MODEL)), const((D_MODEL, 2 * LRU_WIDTH))],
        out_specs=[tok_spec(LRU_WIDTH), tok_spec(LRU_WIDTH)],
        out_shape=[jax.ShapeDtypeStruct((bsz, TOK, LRU_WIDTH), F32),
                   jax.ShapeDtypeStruct((bsz, TOK, LRU_WIDTH), F32)],
        compiler_params=_cparams(("parallel", "arbitrary"), 48),
        name="inproj_odd",
    )(xs, mod, g, w_in)


def _scan_kernel(u_ref, cw_ref, cbias_ref, wa_ref, ba_ref, wx_ref, bx_ref, lam_ref, h_ref,
                 halo_ref, carry_ref, a_ref, b_ref, *, reverse):
    i = pl.program_id(1)
    tt, w = u_ref.shape
    n_groups = tt // SUBLANES

    @pl.when(i <= 1)
    def _():
        halo_ref[...] = jnp.zeros(halo_ref.shape, F32)

    @pl.when(i == 0)
    def _():
        carry_ref[...] = jnp.zeros(carry_ref.shape, F32)

    u = u_ref[...]
    halo = halo_ref[...]
    row8 = lax.broadcasted_iota(jnp.int32, (SUBLANES, w), 0)
    k_self = 0 if reverse else LRU_CONV_K - 1
    uc = cbias_ref[...] + cw_ref[k_self:k_self + 1, :] * u
    for k in range(1, LRU_CONV_K):
        if reverse:
            tmp = pltpu.roll(u, tt - k, axis=0)
            hr = pltpu.roll(halo, SUBLANES - k, axis=0)
            edge = jnp.where(row8 >= SUBLANES - k, hr, tmp[tt - SUBLANES:, :])
            shifted = jnp.concatenate([tmp[:tt - SUBLANES, :], edge], axis=0)
            wk = cw_ref[k:k + 1, :]
        else:
            tmp = pltpu.roll(u, k, axis=0)
            hr = pltpu.roll(halo, k, axis=0)
            edge = jnp.where(row8 < k, hr, tmp[:SUBLANES, :])
            shifted = jnp.concatenate([edge, tmp[SUBLANES:, :]], axis=0)
            wk = cw_ref[LRU_CONV_K - 1 - k:LRU_CONV_K - k, :]
        uc = uc + wk * shifted
    halo_ref[...] = u[:SUBLANES, :] if reverse else u[tt - SUBLANES:, :]

    ucb = uc.astype(BF16)

    def block_diag(w_blocks):
        return jnp.concatenate(
            [jnp.dot(ucb[:, j * LRU_BLOCK:(j + 1) * LRU_BLOCK], w_blocks[j], preferred_element_type=F32)
             for j in range(LRU_BLOCKS)], axis=1)

    r = _sigmoid(block_diag(wa_ref) + ba_ref[...])
    gate_i = _sigmoid(block_diag(wx_ref) + bx_ref[...])
    neg_lam = -lam_ref[...]
    softplus = jnp.maximum(neg_lam, 0.0) + jnp.log1p(jnp.exp(-jnp.abs(neg_lam)))
    a = jnp.exp((-LRU_C * softplus) * r)
    a_ref[...] = a
    b_ref[...] = jnp.sqrt(1.0 - a * a) * (gate_i * uc)

    def group(gi, carry):
        g = n_groups - 1 - gi if reverse else gi
        off = pl.multiple_of(g * SUBLANES, SUBLANES)
        av = a_ref[pl.ds(off, SUBLANES), :]
        bv = b_ref[pl.ds(off, SUBLANES), :]
        for s in (1, 2, 4):
            if reverse:
                outside = row8 >= SUBLANES - s
                shift = SUBLANES - s
            else:
                outside = row8 < s
                shift = s
            a_sh = jnp.where(outside, 1.0, pltpu.roll(av, shift, axis=0))
            b_sh = jnp.where(outside, 0.0, pltpu.roll(bv, shift, axis=0))
            bv = av * b_sh + bv
            av = av * a_sh
        h = av * carry + bv
        h_ref[pl.ds(off, SUBLANES), :] = h
        last = h[0:1, :] if reverse else h[SUBLANES - 1:SUBLANES, :]
        return jnp.broadcast_to(last, (SUBLANES, w))

    carry_ref[...] = lax.fori_loop(0, n_groups, group, carry_ref[...])


def _rglru_scan(u, conv_w, conv_b, w_a, b_a, w_x, b_x, lam, *, reverse):
    bsz = u.shape[0]
    if reverse:
        tile_of = lambda i: N_LAT_TILES - i
    else:
        tile_of = lambda i: (i + N_LAT_TILES) % N_TILES
    tok_spec = pl.BlockSpec((None, TM, LRU_WIDTH), lambda b, i: (b, tile_of(i), 0))
    const = lambda shape: pl.BlockSpec(shape, lambda b, i: (0,) * len(shape))
    return pl.pallas_call(
        functools.partial(_scan_kernel, reverse=reverse),
        grid=(bsz, N_TILES),
        in_specs=[tok_spec, const((SUBLANES, LRU_WIDTH)), const((1, LRU_WIDTH)),
                  const((LRU_BLOCKS, LRU_BLOCK, LRU_BLOCK)), const((1, LRU_WIDTH)),
                  const((LRU_BLOCKS, LRU_BLOCK, LRU_BLOCK)), const((1, LRU_WIDTH)), const((1, LRU_WIDTH))],
        out_specs=tok_spec,
        out_shape=jax.ShapeDtypeStruct((bsz, TOK, LRU_WIDTH), F32),
        scratch_shapes=[pltpu.VMEM((SUBLANES, LRU_WIDTH), F32), pltpu.VMEM((SUBLANES, LRU_WIDTH), F32),
                        pltpu.VMEM((TM, LRU_WIDTH), F32), pltpu.VMEM((TM, LRU_WIDTH), F32)],
        compiler_params=_cparams(("parallel", "arbitrary"), 32),
        name="rglru_rev" if reverse else "rglru_fwd",
    )(u, conv_w, conv_b, w_a, b_a, w_x, b_x, lam)


def _pad_rows(a, rows):
    return jnp.pad(a, ((0, rows - a.shape[0]), (0, 0)))


def _rope_tables():
    t = jnp.arange(SEQ)
    n_freq = DA_HEAD_DIM // 4
    inv = ROPE_BASE ** (-jnp.arange(n_freq, dtype=F32) / n_freq)
    ang = jnp.concatenate([(t // GRID_W).astype(F32)[:, None] * inv,
                           (t % GRID_W).astype(F32)[:, None] * inv], axis=-1)
    cos, sin = jnp.cos(ang), jnp.sin(ang)
    cos64 = jnp.concatenate([cos, cos], axis=-1)
    sin64 = jnp.concatenate([-sin, sin], axis=-1)
    cos_t = jnp.concatenate([jnp.tile(cos64, (1, 2)), jnp.ones((CTX_LEN, LANES), F32)], axis=0)
    sin_t = jnp.concatenate([jnp.tile(sin64, (1, 2)), jnp.zeros((CTX_LEN, LANES), F32)], axis=0)
    return cos_t, sin_t


def _router_params(w_grp, b_grp, w_rt, b_rt):
    wr = jnp.concatenate([w_rt.reshape(D_MODEL, N_EXPERTS), w_grp], axis=1)
    br = jnp.concatenate([b_rt.reshape(N_EXPERTS), b_grp])
    pad = LANES - wr.shape[1]
    return jnp.pad(wr, ((0, 0), (0, pad))), jnp.pad(br, (0, pad)).reshape(1, LANES)


def kernel(x, c, ctx, c_ctx, ada_w, ada_b, norm_mix, norm_ffn, ev_w_in, ev_conv_w, ev_conv_b, ev_q_norm, ev_k_norm, ev_lam_q1, ev_lam_k1, ev_lam_q2, ev_lam_k2, ev_sub_norm, ev_w_out, od_w_in, od_conv_w, od_conv_b, od_w_a, od_b_a, od_w_x, od_b_x, od_lam, od_w_out, moe_w_grp, moe_b_grp, moe_w_rt, moe_b_rt, moe_w_gate, moe_w_up, moe_w_down):
    bsz = x.shape[0]
    assert x.shape == (bsz, SEQ, D_MODEL) and ctx.shape == (bsz, CTX_LEN, D_MODEL) and bsz == 2
    depth = ada_w.shape[0]
    assert depth == 2

    xs = jnp.concatenate([x, ctx], axis=1)

    cvec = _pad_rows(jnp.stack([c[0], c_ctx, c[1], c_ctx]), SUBLANES)
    mod_all = _ada_mod(cvec, ada_w, ada_b)
    mod_all = mod_all[:, :2 * bsz].reshape(depth, 2 * bsz, 6, D_MODEL)
    mod_all = jnp.pad(mod_all, ((0, 0), (0, 0), (0, SUBLANES - 6), (0, 0)))

    l = 0
    lam_init = 0.8 - 0.6 * math.exp(-0.3 * l)
    mod = mod_all[l]
    cos_t, sin_t = _rope_tables()
    blk = jnp.arange(DA_WIDTH) // DA_HEAD_DIM
    bd = jnp.where(blk[:, None] == blk[None, :], 1.0 / DA_HEAD_DIM, 0.0).astype(BF16)
    n_rep = DA_WIDTH // DA_HEAD_DIM
    cb, p, q, k, v = _inproj_even(
        xs, mod, norm_mix[l].reshape(1, D_MODEL), ev_w_in[0].astype(BF16),
        jnp.tile(ev_q_norm[0], n_rep).reshape(1, DA_WIDTH), jnp.tile(ev_k_norm[0], n_rep).reshape(1, DA_WIDTH),
        cos_t, sin_t, bd)

    lamv = _pad_rows(jnp.pad(jnp.stack([ev_lam_q1[0], ev_lam_k1[0], ev_lam_q2[0], ev_lam_k2[0]]),
                             ((0, 0), (0, LANES - DA_HEAD_DIM))), SUBLANES)
    subg = ev_sub_norm[0].reshape(1, DA_V_DIM)
    o_lat = _attention(q, k, v, lamv, subg, tq=256, q_start=0, n_q=SEQ // 256,
                       kv_start=0, kv_len=TOK, tk=768, lam_init=lam_init)
    o_ctx = _attention(q, k, v, lamv, subg, tq=TM, q_start=SEQ, n_q=CTX_LEN // TM,
                       kv_start=SEQ, kv_len=CTX_LEN, tk=CTX_LEN, lam_init=lam_init)

    wr, br = _router_params(moe_w_grp[l], moe_b_grp[l], moe_w_rt[l], moe_b_rt[l])
    x1, hf, meta, counts = _post_even(
        cb, p, _pad_rows(ev_conv_w[0], SUBLANES), ev_conv_b[0].reshape(1, CONV_WIDTH), o_lat, o_ctx,
        ev_w_out[0].astype(BF16), xs, mod, norm_ffn[l].reshape(1, D_MODEL), wr, br)
    xs = _moe(hf, meta, counts, x1, mod, moe_w_gate[l], moe_w_up[l], moe_w_down[l], tiles_per_batch=N_TILES)

    l = 1
    mod = mod_all[l]
    y, u = _inproj_odd(xs, mod, norm_mix[l].reshape(1, D_MODEL), od_w_in[0].astype(BF16))
    h_dirs = []
    for d in range(2):
        h_dirs.append(_rglru_scan(
            u, _pad_rows(od_conv_w[0, d], SUBLANES), od_conv_b[0, d].reshape(1, LRU_WIDTH),
            od_w_a[0, d].astype(BF16), od_b_a[0, d].reshape(1, LRU_WIDTH),
            od_w_x[0, d].astype(BF16), od_b_x[0, d].reshape(1, LRU_WIDTH),
            od_lam[0, d].reshape(1, LRU_WIDTH), reverse=bool(d)))
    wr, br = _router_params(moe_w_grp[l], moe_b_grp[l], moe_w_rt[l], moe_b_rt[l])
    x1, hf, meta, counts = _post_odd(y, h_dirs[0], h_dirs[1], od_w_out[0].astype(BF16), xs, mod,
                                     norm_ffn[l].reshape(1, D_MODEL), wr, br)
    return _moe(hf, meta, counts, x1, mod, moe_w_gate[l], moe_w_up[l], moe_w_down[l],
                tiles_per_batch=N_LAT_TILES)
```

```python
import functools
import math

import jax
import jax.numpy as jnp
from jax import lax
from jax.experimental import pallas as pl
from jax.experimental.pallas import tpu as pltpu

F32 = jnp.float32
BF16 = jnp.bfloat16
HIGHEST = lax.Precision.HIGHEST

D_MODEL = 1024
SEQ = 8192
CTX_LEN = 256
TOK = SEQ + CTX_LEN
GRID_W = 64
EPS = 1e-6

CONV_WIDTH = 512
DA_HEADS = 4
DA_HEAD_DIM = 64
DA_V_DIM = 2 * DA_HEAD_DIM
DA_WIDTH = DA_HEADS * DA_V_DIM
EVEN_IN = 3 * CONV_WIDTH + 3 * DA_WIDTH
ROPE_BASE = 10000.0

LRU_WIDTH = 1024
LRU_BLOCKS = 8
LRU_BLOCK = LRU_WIDTH // LRU_BLOCKS
LRU_CONV_K = 4
LRU_C = 8.0

N_GROUPS = 4
EXPERTS_PER_GROUP = 4
N_EXPERTS = N_GROUPS * EXPERTS_PER_GROUP
D_EXPERT = 512

LANES = 128
SUBLANES = 8
TM = 256
N_LAT_TILES = SEQ // TM
N_TILES = TOK // TM
NEG = -1e30
MIB = 2 ** 20


def _cparams(semantics, vmem_mib):
    return pltpu.CompilerParams(dimension_semantics=semantics, vmem_limit_bytes=vmem_mib * MIB)


def _sigmoid(x):
    return 1.0 / (1.0 + jnp.exp(-x))


def _norm_mod(x, g, shift, scale):
    ms = jnp.mean(x * x, axis=-1, keepdims=True)
    return (x * lax.rsqrt(ms + EPS) * g) * (1.0 + scale) + shift


ADA_TN = 1536


def _ada_kernel(c_ref, w_ref, b_ref, o_ref):
    c = c_ref[...]
    a = c * _sigmoid(c)
    o_ref[...] = jnp.dot(a, w_ref[...], precision=HIGHEST, preferred_element_type=F32) + b_ref[...]


def _ada_mod(cvec, ada_w, ada_b):
    depth, d, n = ada_w.shape
    return pl.pallas_call(
        _ada_kernel,
        grid=(depth, n // ADA_TN),
        in_specs=[
            pl.BlockSpec((SUBLANES, d), lambda l, j: (0, 0)),
            pl.BlockSpec((None, d, ADA_TN), lambda l, j: (l, 0, j)),
            pl.BlockSpec((None, 1, ADA_TN), lambda l, j: (l, 0, j)),
        ],
        out_specs=pl.BlockSpec((None, SUBLANES, ADA_TN), lambda l, j: (l, 0, j)),
        out_shape=jax.ShapeDtypeStruct((depth, SUBLANES, n), F32),
        compiler_params=_cparams(("arbitrary", "arbitrary"), 40),
        name="ada_mod",
    )(cvec, ada_w, ada_b.reshape(depth, 1, n))


def _mod_index(b, i):
    return (b * 2 + i // N_LAT_TILES, 0, 0)


def _inproj_even_kernel(x_ref, mod_ref, g_ref, w_ref, qg_ref, kg_ref, cos_ref, sin_ref, bd_ref,
                        cb_ref, p_ref, q_ref, k_ref, v_ref):
    h = _norm_mod(x_ref[...], g_ref[...], mod_ref[0:1, :], mod_ref[1:2, :])
    z = jnp.dot(h.astype(BF16), w_ref[...], preferred_element_type=F32)
    cw = CONV_WIDTH
    cb_ref[...] = z[:, :cw]
    p_ref[...] = z[:, cw:2 * cw] * z[:, 2 * cw:3 * cw]

    reps = DA_WIDTH // LANES
    cosf = jnp.concatenate([cos_ref[...]] * reps, axis=1)
    sinf = jnp.concatenate([sin_ref[...]] * reps, axis=1)
    lane = lax.broadcasted_iota(jnp.int32, (TM, DA_WIDTH), 1)
    first_half = (lane & (DA_HEAD_DIM - 1)) < DA_HEAD_DIM // 2
    bd = bd_ref[...]

    def head_norm_rope(t, gain):
        tt = t * t
        hi = tt.astype(BF16)
        lo = (tt - hi.astype(F32)).astype(BF16)
        ms = (jnp.dot(hi, bd, preferred_element_type=F32) + jnp.dot(lo, bd, preferred_element_type=F32))
        y = t * lax.rsqrt(ms + EPS) * gain
        fwd = pltpu.roll(y, DA_WIDTH - DA_HEAD_DIM // 2, axis=1)
        bwd = pltpu.roll(y, DA_HEAD_DIM // 2, axis=1)
        return y * cosf + jnp.where(first_half, fwd, bwd) * sinf

    base = 3 * cw
    q = head_norm_rope(z[:, base:base + DA_WIDTH], qg_ref[...])
    q_ref[...] = (q * (DA_HEAD_DIM ** -0.5 * math.log2(math.e))).astype(BF16)
    k = head_norm_rope(z[:, base + DA_WIDTH:base + 2 * DA_WIDTH], kg_ref[...])
    k_ref[...] = k.astype(BF16)
    v_ref[...] = z[:, base + 2 * DA_WIDTH:].astype(BF16)


def _inproj_even(xs, mod, g, w_in, qg, kg, cos, sin, bd):
    bsz = xs.shape[0]
    tok_spec = lambda width: pl.BlockSpec((None, TM, width), lambda b, i: (b, i, 0))
    const = lambda shape: pl.BlockSpec(shape, lambda b, i: (0,) * len(shape))
    return pl.pallas_call(
        _inproj_even_kernel,
        grid=(bsz, N_TILES),
        in_specs=[
            tok_spec(D_MODEL),
            pl.BlockSpec((None, SUBLANES, D_MODEL), _mod_index),
            const((1, D_MODEL)),
            const((D_MODEL, EVEN_IN)),
            const((1, DA_WIDTH)),
            const((1, DA_WIDTH)),
            pl.BlockSpec((TM, LANES), lambda b, i: (i, 0)),
            pl.BlockSpec((TM, LANES), lambda b, i: (i, 0)),
            const((DA_WIDTH, DA_WIDTH)),
        ],
        out_specs=[tok_spec(CONV_WIDTH), tok_spec(CONV_WIDTH), tok_spec(DA_WIDTH), tok_spec(DA_WIDTH),
                   tok_spec(DA_WIDTH)],
        out_shape=[
            jax.ShapeDtypeStruct((bsz, TOK, CONV_WIDTH), F32),
            jax.ShapeDtypeStruct((bsz, TOK, CONV_WIDTH), F32),
            jax.ShapeDtypeStruct((bsz, TOK, DA_WIDTH), BF16),
            jax.ShapeDtypeStruct((bsz, TOK, DA_WIDTH), BF16),
            jax.ShapeDtypeStruct((bsz, TOK, DA_WIDTH), BF16),
        ],
        compiler_params=_cparams(("parallel", "arbitrary"), 48),
        name="inproj_even",
    )(xs, mod, g, w_in, qg, kg, cos, sin, bd)


def _attn_kernel(lam_ref, subg_ref, q_ref, k_ref, v_ref, o_ref, qz_ref, m_ref, l_ref, acc_ref,
                 sa_ref, sb_ref, *, tq, tk, nk, lam_init):
    q = q_ref[...]
    lane = lax.broadcasted_iota(jnp.int32, (tq, LANES), 1)
    zero = jnp.zeros_like(q)
    qz_ref[0:tq, :] = jnp.where(lane < DA_HEAD_DIM, q, zero)
    qz_ref[tq:2 * tq, :] = jnp.where(lane >= DA_HEAD_DIM, q, zero)
    m_ref[...] = jnp.full(m_ref.shape, NEG, F32)
    l_ref[...] = jnp.zeros(l_ref.shape, F32)
    acc_ref[...] = jnp.zeros(acc_ref.shape, F32)

    def scores(j, dst_ref):
        off = pl.multiple_of(j * tk, tk)
        dst_ref[...] = lax.dot_general(qz_ref[...], k_ref[pl.ds(off, tk), :], (((1,), (1,)), ((), ())),
                                       preferred_element_type=F32)

    def update(j, src_ref):
        off = pl.multiple_of(j * tk, tk)
        s = src_ref[...]
        m_prev = m_ref[...]
        m_new = jnp.maximum(m_prev, jnp.max(s, axis=1, keepdims=True))
        alpha = jnp.exp2(m_prev - m_new)
        p = jnp.exp2(s - m_new[:, :1])
        l_ref[...] = alpha * l_ref[...] + jnp.sum(p, axis=1, keepdims=True)
        acc_ref[...] = alpha * acc_ref[...] + jnp.dot(p.astype(BF16), v_ref[pl.ds(off, tk), :],
                                                      preferred_element_type=F32)
        m_ref[...] = m_new

    scores(0, sa_ref)

    def pair(i, carry):
        j = 2 * i
        scores(j + 1, sb_ref)
        update(j, sa_ref)
        scores(jnp.minimum(j + 2, nk - 1), sa_ref)
        update(j + 1, sb_ref)
        return carry

    lax.fori_loop(0, nk // 2, pair, 0)
    if nk % 2:
        update(nk - 1, sa_ref)

    o = acc_ref[...] / l_ref[...]
    lv = lam_ref[...]
    lam = (jnp.exp(jnp.sum(lv[0:1, :] * lv[1:2, :], axis=1, keepdims=True))
           - jnp.exp(jnp.sum(lv[2:3, :] * lv[3:4, :], axis=1, keepdims=True)) + lam_init)
    d = o[:tq] - lam * o[tq:]
    ms = jnp.mean(d * d, axis=1, keepdims=True)
    o_ref[...] = (d * lax.rsqrt(ms + EPS) * subg_ref[...] * (1.0 - lam_init)).astype(BF16)


def _attention(q, k, v, lamv, subg, *, tq, q_start, n_q, kv_start, kv_len, tk, lam_init):
    bsz = q.shape[0]
    q_blk0 = q_start // tq
    kv_blk = kv_start // kv_len
    q_spec = pl.BlockSpec((None, tq, LANES), lambda b, h, i: (b, q_blk0 + i, h))
    o_spec = pl.BlockSpec((None, tq, LANES), lambda b, h, i: (b, i, h))
    kv_spec = pl.BlockSpec((None, kv_len, LANES), lambda b, h, i: (b, kv_blk, h))
    const = lambda shape: pl.BlockSpec(shape, lambda b, h, i: (0,) * len(shape))
    return pl.pallas_call(
        functools.partial(_attn_kernel, tq=tq, tk=tk, nk=kv_len // tk, lam_init=lam_init),
        grid=(bsz, DA_HEADS, n_q),
        in_specs=[const((SUBLANES, LANES)), const((1, LANES)), q_spec, kv_spec, kv_spec],
        out_specs=o_spec,
        out_shape=jax.ShapeDtypeStruct((bsz, n_q * tq, DA_WIDTH), BF16),
        scratch_shapes=[
            pltpu.VMEM((2 * tq, LANES), BF16),
            pltpu.VMEM((2 * tq, LANES), F32),
            pltpu.VMEM((2 * tq, LANES), F32),
            pltpu.VMEM((2 * tq, LANES), F32),
            pltpu.VMEM((2 * tq, tk), F32),
            pltpu.VMEM((2 * tq, tk), F32),
        ],
        compiler_params=_cparams(("parallel", "parallel", "arbitrary"), 48),
        name="diff_attn",
    )(lamv, subg, q, k, v)


PAIRS_PER_GROUP = EXPERTS_PER_GROUP * (EXPERTS_PER_GROUP - 1) // 2
N_BUCKETS = N_GROUPS * PAIRS_PER_GROUP
META_BUCKET, META_RANK, META_W_LO, META_W_HI = 0, 1, 2, 3


def _post_tail(y, first_step, x_ref, mod_ref, g_ref, wr_ref, br_ref, x1_ref, hf_ref, meta_ref, cnt_ref,
               cnt_scr):
    x1 = x_ref[...] + mod_ref[2:3, :] * y
    x1_ref[...] = x1
    hf = _norm_mod(x1, g_ref[...], mod_ref[3:4, :], mod_ref[4:5, :])
    hf_ref[...] = hf

    logits = jnp.dot(hf, wr_ref[...], precision=HIGHEST, preferred_element_type=F32) + br_ref[...]
    lane = lax.broadcasted_iota(jnp.int32, logits.shape, 1).astype(F32)
    big = float(LANES)
    is_g = (lane >= N_EXPERTS) & (lane < N_EXPERTS + N_GROUPS)
    gl = jnp.where(is_g, logits, NEG)
    gm = jnp.max(gl, axis=1, keepdims=True)
    g_idx = jnp.min(jnp.where(gl == gm, lane, big), axis=1, keepdims=True) - N_EXPERTS
    p_sel = 1.0 / jnp.sum(jnp.exp(gl - gm), axis=1, keepdims=True)
    lo = g_idx * EXPERTS_PER_GROUP
    el = jnp.where((lane >= lo) & (lane < lo + EXPERTS_PER_GROUP), logits, NEG)
    v1 = jnp.max(el, axis=1, keepdims=True)
    i1 = jnp.min(jnp.where(el == v1, lane, big), axis=1, keepdims=True)
    el2 = jnp.where(lane == i1, NEG, el)
    v2 = jnp.max(el2, axis=1, keepdims=True)
    i2 = jnp.min(jnp.where(el2 == v2, lane, big), axis=1, keepdims=True)
    t = jnp.exp(v2 - v1)
    w1 = p_sel / (1.0 + t)
    w2 = t * w1

    first_lower = i1 < i2
    a = jnp.minimum(i1, i2) - lo
    b = jnp.maximum(i1, i2) - lo
    bucket = g_idx * PAIRS_PER_GROUP + a * (7.0 - a) * 0.5 + (b - a - 1.0)
    w_lo = jnp.where(first_lower, w1, w2)
    w_hi = jnp.where(first_lower, w2, w1)

    @pl.when(first_step)
    def _():
        cnt_scr[...] = jnp.zeros(cnt_scr.shape, F32)

    tm = logits.shape[0]
    onehot = jnp.where(lane == bucket, 1.0, 0.0)
    r_i = lax.broadcasted_iota(jnp.int32, (tm, tm), 0)
    c_i = lax.broadcasted_iota(jnp.int32, (tm, tm), 1)
    earlier = jnp.where(c_i < r_i, 1.0, 0.0).astype(BF16)
    prefix = jnp.dot(earlier, onehot.astype(BF16), preferred_element_type=F32)
    base = cnt_scr[0:1, :]
    rank = jnp.sum(onehot * (prefix + base), axis=1, keepdims=True)
    counts = jnp.broadcast_to(base + jnp.sum(onehot, axis=0, keepdims=True), cnt_scr.shape)
    cnt_scr[...] = counts
    cnt_ref[...] = counts
    meta_ref[...] = (jnp.where(lane == META_BUCKET, bucket, 0.0) + jnp.where(lane == META_RANK, rank, 0.0)
                     + jnp.where(lane == META_W_LO, w_lo, 0.0) + jnp.where(lane == META_W_HI, w_hi, 0.0))


def _post_even_kernel(cb_ref, p_ref, pprev_ref, pnext_ref, cw_ref, cbias_ref, olat_ref, octx_ref, w_ref,
                      x_ref, mod_ref, g_ref, wr_ref, br_ref, x1_ref, hf_ref, meta_ref, cnt_ref, cnt_scr):
    i = pl.program_id(1)
    first_step = jnp.logical_and(pl.program_id(0) == 0, i == 0)
    pc = p_ref[...]
    row = lax.broadcasted_iota(jnp.int32, pc.shape, 0)
    has_prev = jnp.logical_and(i != 0, i != N_LAT_TILES)
    has_next = i < N_LAT_TILES - 1
    prev_row = jnp.where(has_prev, pprev_ref[SUBLANES - 1:SUBLANES, :], 0.0)
    next_row = jnp.where(has_next, pnext_ref[0:1, :], 0.0)
    up = jnp.where(row == 0, prev_row, pltpu.roll(pc, 1, axis=0))
    dn = jnp.where(row == TM - 1, next_row, pltpu.roll(pc, TM - 1, axis=0))
    conv = cbias_ref[...] + cw_ref[0:1, :] * up + cw_ref[1:2, :] * pc + cw_ref[2:3, :] * dn
    out_a = (cb_ref[...] * conv).astype(BF16)
    o = jnp.where(i == N_LAT_TILES, octx_ref[...], olat_ref[...])
    y = (jnp.dot(out_a, w_ref[0:CONV_WIDTH, :], preferred_element_type=F32)
         + jnp.dot(o, w_ref[CONV_WIDTH:, :], preferred_element_type=F32))
    _post_tail(y, first_step, x_ref, mod_ref, g_ref, wr_ref, br_ref, x1_ref, hf_ref, meta_ref, cnt_ref, cnt_scr)


def _post_odd_kernel(y_ref, hf_in_ref, hb_in_ref, w_ref, x_ref, mod_ref, g_ref, wr_ref, br_ref,
                     x1_ref, hf_ref, meta_ref, cnt_ref, cnt_scr):
    first_step = jnp.logical_and(pl.program_id(0) == 0, pl.program_id(1) == 0)
    a = (y_ref[...] * (hf_in_ref[...] + hb_in_ref[...])).astype(BF16)
    y = jnp.dot(a, w_ref[...], preferred_element_type=F32)
    _post_tail(y, first_step, x_ref, mod_ref, g_ref, wr_ref, br_ref, x1_ref, hf_ref, meta_ref, cnt_ref, cnt_scr)


def _post_specs(bsz, rows):
    tok_spec = lambda width: pl.BlockSpec((None, TM, width), lambda b, i: (b, i, 0))
    const = lambda shape: pl.BlockSpec(shape, lambda b, i: (0,) * len(shape))
    tail_in = [tok_spec(D_MODEL), pl.BlockSpec((None, SUBLANES, D_MODEL), _mod_index), const((1, D_MODEL)),
               const((D_MODEL, LANES)), const((1, LANES))]
    out_specs = [tok_spec(D_MODEL), tok_spec(D_MODEL), tok_spec(LANES), const((SUBLANES, LANES))]
    out_shape = [jax.ShapeDtypeStruct((bsz, rows, D_MODEL), F32),
                 jax.ShapeDtypeStruct((bsz, rows, D_MODEL), F32),
                 jax.ShapeDtypeStruct((bsz, rows, LANES), F32),
                 jax.ShapeDtypeStruct((SUBLANES, LANES), F32)]
    scratch = [pltpu.VMEM((SUBLANES, LANES), F32)]
    return tok_spec, const, tail_in, out_specs, out_shape, scratch


def _post_even(cb, p, conv_w, conv_b, o_lat, o_ctx, w_out, xs, mod, g, wr, br):
    bsz = xs.shape[0]
    tok_spec, const, tail_in, out_specs, out_shape, scratch = _post_specs(bsz, TOK)
    olat_spec = pl.BlockSpec((None, TM, DA_WIDTH), lambda b, i: (b, jnp.minimum(i, N_LAT_TILES - 1), 0))
    octx_spec = pl.BlockSpec((None, TM, DA_WIDTH), lambda b, i: (b, 0, 0))
    halo_blocks = TM // SUBLANES
    last_halo = TOK // SUBLANES - 1
    prev_spec = pl.BlockSpec((None, SUBLANES, CONV_WIDTH),
                             lambda b, i: (b, jnp.maximum(i * halo_blocks - 1, 0), 0))
    next_spec = pl.BlockSpec((None, SUBLANES, CONV_WIDTH),
                             lambda b, i: (b, jnp.minimum((i + 1) * halo_blocks, last_halo), 0))
    return pl.pallas_call(
        _post_even_kernel,
        grid=(bsz, N_TILES),
        in_specs=[tok_spec(CONV_WIDTH), tok_spec(CONV_WIDTH), prev_spec, next_spec,
                  const((SUBLANES, CONV_WIDTH)), const((1, CONV_WIDTH)), olat_spec, octx_spec,
                  const((D_MODEL, D_MODEL))] + tail_in,
        out_specs=out_specs,
        out_shape=out_shape,
        scratch_shapes=scratch,
        compiler_params=_cparams(("arbitrary", "arbitrary"), 48),
        name="post_even",
    )(cb, p, p, p, conv_w, conv_b, o_lat, o_ctx, w_out, xs, mod, g, wr, br)


def _post_odd(y, hfw, hbw, w_out, xs, mod, g, wr, br):
    bsz = xs.shape[0]
    tok_spec, const, tail_in, out_specs, out_shape, scratch = _post_specs(bsz, SEQ)
    return pl.pallas_call(
        _post_odd_kernel,
        grid=(bsz, N_LAT_TILES),
        in_specs=[tok_spec(LRU_WIDTH), tok_spec(LRU_WIDTH), tok_spec(LRU_WIDTH),
                  const((LRU_WIDTH, D_MODEL))] + tail_in,
        out_specs=out_specs,
        out_shape=out_shape,
        scratch_shapes=scratch,
        compiler_params=_cparams(("arbitrary", "arbitrary"), 48),
        name="post_odd",
    )(y, hfw, hbw, w_out, xs, mod, g, wr, br)


_PAIRS = [(a, b) for a in range(EXPERTS_PER_GROUP) for b in range(a + 1, EXPERTS_PER_GROUP)]
_BUCKET_LO = [g * EXPERTS_PER_GROUP + a for g in range(N_GROUPS) for a, _ in _PAIRS]
_BUCKET_HI = [g * EXPERTS_PER_GROUP + b for g in range(N_GROUPS) for _, b in _PAIRS]


def _sorted_tiles(n_tokens):
    return n_tokens // TM + N_BUCKETS


def _route_plan(meta, counts, n_tokens):
    n_tiles = _sorted_tiles(n_tokens)
    bucket = meta[..., META_BUCKET].astype(jnp.int32).reshape(n_tokens)
    rank = meta[..., META_RANK].astype(jnp.int32).reshape(n_tokens)
    cnt = counts[0, :N_BUCKETS].astype(jnp.int32)
    tiles_per = (cnt + TM - 1) // TM
    tile_end = jnp.cumsum(tiles_per)
    row_start = (tile_end - tiles_per) * TM
    dest = (row_start[bucket] + rank).reshape(n_tokens // TM, 1, TM)
    tile_bucket = jnp.minimum(jnp.sum(jnp.arange(n_tiles)[:, None] >= tile_end[None, :], axis=1), N_BUCKETS - 1)
    e_lo = jnp.asarray(_BUCKET_LO, jnp.int32)[tile_bucket]
    e_hi = jnp.asarray(_BUCKET_HI, jnp.int32)[tile_bucket]
    return dest, e_lo, e_hi, tile_end[-1:].astype(jnp.int32)


def _invert_kernel(dest_ref, src_ref):
    i = pl.program_id(0)

    @pl.when(i == 0)
    def _():
        def clear(k, carry):
            src_ref[k] = 0
            return carry
        lax.fori_loop(0, src_ref.shape[0], clear, 0, unroll=8)

    base = i * TM

    def put(r, carry):
        src_ref[dest_ref[0, r]] = base + r
        return carry

    lax.fori_loop(0, TM, put, 0, unroll=8)


def _invert(dest, n_tokens):
    rows = _sorted_tiles(n_tokens) * TM
    return pl.pallas_call(
        _invert_kernel,
        grid=(n_tokens // TM,),
        in_specs=[pl.BlockSpec((None, 1, TM), lambda i: (i, 0, 0), memory_space=pltpu.SMEM)],
        out_specs=pl.BlockSpec(memory_space=pltpu.SMEM),
        out_shape=jax.ShapeDtypeStruct((rows,), jnp.int32),
        compiler_params=_cparams(("arbitrary",), 16),
        name="moe_invert",
    )(dest)


def _gather_rows(idx_ref, src_hbm, dst_ref, sem):
    def issue(r, carry):
        pltpu.make_async_copy(src_hbm.at[pl.ds(idx_ref[0, r], 1), :], dst_ref.at[pl.ds(r, 1), :], sem).start()
        return carry
    lax.fori_loop(0, TM, issue, 0, unroll=8)


def _gather_wait(src_hbm, dst_ref, sem):
    pltpu.make_async_copy(src_hbm.at[pl.ds(0, TM), :], dst_ref, sem).wait()


def _moe_routed_kernel(elo_ref, ehi_ref, nused_ref, src_ref, src_next_ref, hf_hbm,
                       wg_lo, wu_lo, wd_lo, wg_hi, wu_hi, wd_hi, y_ref, hbuf_ref, sems):
    del elo_ref, ehi_ref
    j = pl.program_id(0)
    n_used = nused_ref[0]
    slot = j % 2

    @pl.when(j == 0)
    def _():
        _gather_rows(src_ref, hf_hbm, hbuf_ref.at[0], sems.at[0])

    @pl.when(j + 1 < n_used)
    def _():
        _gather_rows(src_next_ref, hf_hbm, hbuf_ref.at[1 - slot], sems.at[1 - slot])

    @pl.when(j < n_used)
    def _():
        _gather_wait(hf_hbm, hbuf_ref.at[slot], sems.at[slot])
        h = hbuf_ref[slot].astype(BF16)

        def expert(wg_ref, wu_ref, wd_ref):
            hg = jnp.dot(h, wg_ref[...].astype(BF16), preferred_element_type=F32)
            hu = jnp.dot(h, wu_ref[...].astype(BF16), preferred_element_type=F32)
            act = (hg * _sigmoid(hg)) * hu
            return jnp.dot(act.astype(BF16), wd_ref[...].astype(BF16), preferred_element_type=F32)

        y_ref[:, :D_MODEL] = expert(wg_lo, wu_lo, wd_lo)
        y_ref[:, D_MODEL:] = expert(wg_hi, wu_hi, wd_hi)

    @pl.when(j >= n_used)
    def _():
        y_ref[...] = jnp.zeros(y_ref.shape, F32)


def _moe_routed(src, hf, e_lo, e_hi, n_used, wg, wu, wd):
    n_tiles = src.shape[0]
    up_spec = lambda tbl: pl.BlockSpec((None, D_MODEL, D_EXPERT), lambda j, lo, hi, nu: ((lo, hi)[tbl][j], 0, 0))
    dn_spec = lambda tbl: pl.BlockSpec((None, D_EXPERT, D_MODEL), lambda j, lo, hi, nu: ((lo, hi)[tbl][j], 0, 0))
    grid_spec = pltpu.PrefetchScalarGridSpec(
        num_scalar_prefetch=3,
        grid=(n_tiles,),
        in_specs=[pl.BlockSpec((None, 1, TM), lambda j, lo, hi, nu: (j, 0, 0), memory_space=pltpu.SMEM),
                  pl.BlockSpec((None, 1, TM), lambda j, lo, hi, nu: (jnp.minimum(j + 1, n_tiles - 1), 0, 0),
                               memory_space=pltpu.SMEM),
                  pl.BlockSpec(memory_space=pl.ANY),
                  up_spec(0), up_spec(0), dn_spec(0), up_spec(1), up_spec(1), dn_spec(1)],
        out_specs=pl.BlockSpec((TM, 2 * D_MODEL), lambda j, lo, hi, nu: (j, 0)),
        scratch_shapes=[pltpu.VMEM((2, TM, D_MODEL), F32), pltpu.SemaphoreType.DMA((2,))],
    )
    return pl.pallas_call(
        _moe_routed_kernel,
        grid_spec=grid_spec,
        out_shape=jax.ShapeDtypeStruct((n_tiles * TM, 2 * D_MODEL), F32),
        compiler_params=_cparams(("arbitrary",), 56),
        name="moe_routed",
    )(e_lo, e_hi, n_used, src, src, hf, wg, wu, wd, wg, wu, wd)


def _combine_kernel(dest_ref, dest_next_ref, y_hbm, x_ref, meta_ref, mod_ref, o_ref, buf_ref, sems):
    step = pl.program_id(0) * pl.num_programs(1) + pl.program_id(1)
    n_steps = pl.num_programs(0) * pl.num_programs(1)
    slot = step % 2

    @pl.when(step == 0)
    def _():
        _gather_rows(dest_ref, y_hbm, buf_ref.at[0], sems.at[0])

    @pl.when(step + 1 < n_steps)
    def _():
        _gather_rows(dest_next_ref, y_hbm, buf_ref.at[1 - slot], sems.at[1 - slot])

    _gather_wait(y_hbm, buf_ref.at[slot], sems.at[slot])
    meta = meta_ref[...]
    moe = (meta[:, META_W_LO:META_W_LO + 1] * buf_ref[slot, :, :D_MODEL]
           + meta[:, META_W_HI:META_W_HI + 1] * buf_ref[slot, :, D_MODEL:])
    o_ref[...] = x_ref[...] + mod_ref[5:6, :] * moe


def _combine(dest, y, x1, meta, mod, *, tiles_per_batch):
    bsz = x1.shape[0]
    n_steps = bsz * tiles_per_batch
    tok_spec = lambda width: pl.BlockSpec((None, TM, width), lambda b, i: (b, i, 0))
    idx_spec = lambda ahead: pl.BlockSpec(
        (None, 1, TM), lambda b, i: (jnp.minimum(b * tiles_per_batch + i + ahead, n_steps - 1), 0, 0),
        memory_space=pltpu.SMEM)
    return pl.pallas_call(
        _combine_kernel,
        grid=(bsz, tiles_per_batch),
        in_specs=[idx_spec(0), idx_spec(1), pl.BlockSpec(memory_space=pl.ANY), tok_spec(D_MODEL),
                  tok_spec(LANES), pl.BlockSpec((None, SUBLANES, D_MODEL), _mod_index)],
        out_specs=tok_spec(D_MODEL),
        out_shape=jax.ShapeDtypeStruct(x1.shape, F32),
        scratch_shapes=[pltpu.VMEM((2, TM, 2 * D_MODEL), F32), pltpu.SemaphoreType.DMA((2,))],
        compiler_params=_cparams(("arbitrary", "arbitrary"), 32),
        name="moe_combine",
    )(dest, dest, y, x1, meta, mod)


def _moe(hf, meta, counts, x1, mod, wg, wu, wd, *, tiles_per_batch):
    n_tokens = hf.shape[0] * hf.shape[1]
    dest, e_lo, e_hi, n_used = _route_plan(meta, counts, n_tokens)
    src = _invert(dest, n_tokens).reshape(_sorted_tiles(n_tokens), 1, TM)
    y = _moe_routed(src, hf.reshape(n_tokens, D_MODEL), e_lo, e_hi, n_used, wg, wu, wd)
    return _combine(dest, y, x1, meta, mod, tiles_per_batch=tiles_per_batch)


def _inproj_odd_kernel(x_ref, mod_ref, g_ref, w_ref, y_ref, u_ref):
    h = _norm_mod(x_ref[...], g_ref[...], mod_ref[0:1, :], mod_ref[1:2, :])
    z = jnp.dot(h.astype(BF16), w_ref[...], preferred_element_type=F32)
    zy = z[:, :LRU_WIDTH]
    c0 = math.sqrt(2.0 / math.pi)
    y_ref[...] = 0.5 * zy * (1.0 + jnp.tanh(c0 * (zy + 0.044715 * (zy * zy * zy))))
    u_ref[...] = z[:, LRU_WIDTH:]


def _inproj_odd(xs, mod, g, w_in):
    bsz = xs.shape[0]
    tok_spec = lambda width: pl.BlockSpec((None, TM, width), lambda b, i: (b, i, 0))
    const = lambda shape: pl.BlockSpec(shape, lambda b, i: (0,) * len(shape))
    return pl.pallas_call(
        _inproj_odd_kernel,
        grid=(bsz, N_TILES),
        in_specs=[tok_spec(D_MODEL), pl.BlockSpec((None, SUBLANES, D_MODEL), _mod_index),
                  const((1, D_MODEL)), const((D_MODEL, 2 * LRU_WIDTH))],
        out_specs=[tok_spec(LRU_WIDTH), tok_spec(LRU_WIDTH)],
        out_shape=[jax.ShapeDtypeStruct((bsz, TOK, LRU_WIDTH), F32),
                   jax.ShapeDtypeStruct((bsz, TOK, LRU_WIDTH), F32)],
        compiler_params=_cparams(("parallel", "arbitrary"), 48),
        name="inproj_odd",
    )(xs, mod, g, w_in)


def _scan_kernel(u_ref, cw_ref, cbias_ref, wa_ref, ba_ref, wx_ref, bx_ref, lam_ref, h_ref,
                 halo_ref, carry_ref, a_ref, b_ref, *, reverse):
    i = pl.program_id(1)
    tt, w = u_ref.shape
    n_groups = tt // SUBLANES

    @pl.when(i <= 1)
    def _():
        halo_ref[...] = jnp.zeros(halo_ref.shape, F32)

    @pl.when(i == 0)
    def _():
        carry_ref[...] = jnp.zeros(carry_ref.shape, F32)

    u = u_ref[...]
    halo = halo_ref[...]
    row8 = lax.broadcasted_iota(jnp.int32, (SUBLANES, w), 0)
    k_self = 0 if reverse else LRU_CONV_K - 1
    uc = cbias_ref[...] + cw_ref[k_self:k_self + 1, :] * u
    for k in range(1, LRU_CONV_K):
        if reverse:
            tmp = pltpu.roll(u, tt - k, axis=0)
            hr = pltpu.roll(halo, SUBLANES - k, axis=0)
            edge = jnp.where(row8 >= SUBLANES - k, hr, tmp[tt - SUBLANES:, :])
            shifted = jnp.concatenate([tmp[:tt - SUBLANES, :], edge], axis=0)
            wk = cw_ref[k:k + 1, :]
        else:
            tmp = pltpu.roll(u, k, axis=0)
            hr = pltpu.roll(halo, k, axis=0)
            edge = jnp.where(row8 < k, hr, tmp[:SUBLANES, :])
            shifted = jnp.concatenate([edge, tmp[SUBLANES:, :]], axis=0)
            wk = cw_ref[LRU_CONV_K - 1 - k:LRU_CONV_K - k, :]
        uc = uc + wk * shifted
    halo_ref[...] = u[:SUBLANES, :] if reverse else u[tt - SUBLANES:, :]

    ucb = uc.astype(BF16)

    def block_diag(w_blocks):
        return jnp.concatenate(
            [jnp.dot(ucb[:, j * LRU_BLOCK:(j + 1) * LRU_BLOCK], w_blocks[j], preferred_element_type=F32)
             for j in range(LRU_BLOCKS)], axis=1)

    r = _sigmoid(block_diag(wa_ref) + ba_ref[...])
    gate_i = _sigmoid(block_diag(wx_ref) + bx_ref[...])
    neg_lam = -lam_ref[...]
    softplus = jnp.maximum(neg_lam, 0.0) + jnp.log1p(jnp.exp(-jnp.abs(neg_lam)))
    a = jnp.exp((-LRU_C * softplus) * r)
    a_ref[...] = a
    b_ref[...] = jnp.sqrt(1.0 - a * a) * (gate_i * uc)

    def group(gi, carry):
        g = n_groups - 1 - gi if reverse else gi
        off = pl.multiple_of(g * SUBLANES, SUBLANES)
        av = a_ref[pl.ds(off, SUBLANES), :]
        bv = b_ref[pl.ds(off, SUBLANES), :]
        for s in (1, 2, 4):
            if reverse:
                outside = row8 >= SUBLANES - s
                shift = SUBLANES - s
            else:
                outside = row8 < s
                shift = s
            a_sh = jnp.where(outside, 1.0, pltpu.roll(av, shift, axis=0))
            b_sh = jnp.where(outside, 0.0, pltpu.roll(bv, shift, axis=0))
            bv = av * b_sh + bv
            av = av * a_sh
        h = av * carry + bv
        h_ref[pl.ds(off, SUBLANES), :] = h
        last = h[0:1, :] if reverse else h[SUBLANES - 1:SUBLANES, :]
        return jnp.broadcast_to(last, (SUBLANES, w))

    carry_ref[...] = lax.fori_loop(0, n_groups, group, carry_ref[...])


def _rglru_scan(u, conv_w, conv_b, w_a, b_a, w_x, b_x, lam, *, reverse):
    bsz = u.shape[0]
    if reverse:
        tile_of = lambda i: N_LAT_TILES - i
    else:
        tile_of = lambda i: (i + N_LAT_TILES) % N_TILES
    tok_spec = pl.BlockSpec((None, TM, LRU_WIDTH), lambda b, i: (b, tile_of(i), 0))
    const = lambda shape: pl.BlockSpec(shape, lambda b, i: (0,) * len(shape))
    return pl.pallas_call(
        functools.partial(_scan_kernel, reverse=reverse),
        grid=(bsz, N_TILES),
        in_specs=[tok_spec, const((SUBLANES, LRU_WIDTH)), const((1, LRU_WIDTH)),
                  const((LRU_BLOCKS, LRU_BLOCK, LRU_BLOCK)), const((1, LRU_WIDTH)),
                  const((LRU_BLOCKS, LRU_BLOCK, LRU_BLOCK)), const((1, LRU_WIDTH)), const((1, LRU_WIDTH))],
        out_specs=tok_spec,
        out_shape=jax.ShapeDtypeStruct((bsz, TOK, LRU_WIDTH), F32),
        scratch_shapes=[pltpu.VMEM((SUBLANES, LRU_WIDTH), F32), pltpu.VMEM((SUBLANES, LRU_WIDTH), F32),
                        pltpu.VMEM((TM, LRU_WIDTH), F32), pltpu.VMEM((TM, LRU_WIDTH), F32)],
        compiler_params=_cparams(("parallel", "arbitrary"), 32),
        name="rglru_rev" if reverse else "rglru_fwd",
    )(u, conv_w, conv_b, w_a, b_a, w_x, b_x, lam)


def _pad_rows(a, rows):
    return jnp.pad(a, ((0, rows - a.shape[0]), (0, 0)))


def _rope_tables():
    t = jnp.arange(SEQ)
    n_freq = DA_HEAD_DIM // 4
    inv = ROPE_BASE ** (-jnp.arange(n_freq, dtype=F32) / n_freq)
    ang = jnp.concatenate([(t // GRID_W).astype(F32)[:, None] * inv,
                           (t % GRID_W).astype(F32)[:, None] * inv], axis=-1)
    cos, sin = jnp.cos(ang), jnp.sin(ang)
    cos64 = jnp.concatenate([cos, cos], axis=-1)
    sin64 = jnp.concatenate([-sin, sin], axis=-1)
    cos_t = jnp.concatenate([jnp.tile(cos64, (1, 2)), jnp.ones((CTX_LEN, LANES), F32)], axis=0)
    sin_t = jnp.concatenate([jnp.tile(sin64, (1, 2)), jnp.zeros((CTX_LEN, LANES), F32)], axis=0)
    return cos_t, sin_t


def _router_params(w_grp, b_grp, w_rt, b_rt):
    wr = jnp.concatenate([w_rt.reshape(D_MODEL, N_EXPERTS), w_grp], axis=1)
    br = jnp.concatenate([b_rt.reshape(N_EXPERTS), b_grp])
    pad = LANES - wr.shape[1]
    return jnp.pad(wr, ((0, 0), (0, pad))), jnp.pad(br, (0, pad)).reshape(1, LANES)


def kernel(x, c, ctx, c_ctx, ada_w, ada_b, norm_mix, norm_ffn, ev_w_in, ev_conv_w, ev_conv_b, ev_q_norm, ev_k_norm, ev_lam_q1, ev_lam_k1, ev_lam_q2, ev_lam_k2, ev_sub_norm, ev_w_out, od_w_in, od_conv_w, od_conv_b, od_w_a, od_b_a, od_w_x, od_b_x, od_lam, od_w_out, moe_w_grp, moe_b_grp, moe_w_rt, moe_b_rt, moe_w_gate, moe_w_up, moe_w_down):
    bsz = x.shape[0]
    assert x.shape == (bsz, SEQ, D_MODEL) and ctx.shape == (bsz, CTX_LEN, D_MODEL) and bsz == 2
    depth = ada_w.shape[0]
    assert depth == 2

    xs = jnp.concatenate([x, ctx], axis=1)

    cvec = _pad_rows(jnp.stack([c[0], c_ctx, c[1], c_ctx]), SUBLANES)
    mod_all = _ada_mod(cvec, ada_w, ada_b)
    mod_all = mod_all[:, :2 * bsz].reshape(depth, 2 * bsz, 6, D_MODEL)
    mod_all = jnp.pad(mod_all, ((0, 0), (0, 0), (0, SUBLANES - 6), (0, 0)))

    l = 0
    lam_init = 0.8 - 0.6 * math.exp(-0.3 * l)
    mod = mod_all[l]
    cos_t, sin_t = _rope_tables()
    blk = jnp.arange(DA_WIDTH) // DA_HEAD_DIM
    bd = jnp.where(blk[:, None] == blk[None, :], 1.0 / DA_HEAD_DIM, 0.0).astype(BF16)
    n_rep = DA_WIDTH // DA_HEAD_DIM
    cb, p, q, k, v = _inproj_even(
        xs, mod, norm_mix[l].reshape(1, D_MODEL), ev_w_in[0].astype(BF16),
        jnp.tile(ev_q_norm[0], n_rep).reshape(1, DA_WIDTH), jnp.tile(ev_k_norm[0], n_rep).reshape(1, DA_WIDTH),
        cos_t, sin_t, bd)

    lamv = _pad_rows(jnp.pad(jnp.stack([ev_lam_q1[0], ev_lam_k1[0], ev_lam_q2[0], ev_lam_k2[0]]),
                             ((0, 0), (0, LANES - DA_HEAD_DIM))), SUBLANES)
    subg = ev_sub_norm[0].reshape(1, DA_V_DIM)
    o_lat = _attention(q, k, v, lamv, subg, tq=256, q_start=0, n_q=SEQ // 256,
                       kv_start=0, kv_len=TOK, tk=768, lam_init=lam_init)
    o_ctx = _attention(q, k, v, lamv, subg, tq=TM, q_start=SEQ, n_q=CTX_LEN // TM,
                       kv_start=SEQ, kv_len=CTX_LEN, tk=CTX_LEN, lam_init=lam_init)

    wr, br = _router_params(moe_w_grp[l], moe_b_grp[l], moe_w_rt[l], moe_b_rt[l])
    x1, hf, meta, counts = _post_even(
        cb, p, _pad_rows(ev_conv_w[0], SUBLANES), ev_conv_b[0].reshape(1, CONV_WIDTH), o_lat, o_ctx,
        ev_w_out[0].astype(BF16), xs, mod, norm_ffn[l].reshape(1, D_MODEL), wr, br)
    xs = _moe(hf, meta, counts, x1, mod, moe_w_gate[l], moe_w_up[l], moe_w_down[l], tiles_per_batch=N_TILES)

    l = 1
    mod = mod_all[l]
    y, u = _inproj_odd(xs, mod, norm_mix[l].reshape(1, D_MODEL), od_w_in[0].astype(BF16))
    h_dirs = []
    for d in range(2):
        h_dirs.append(_rglru_scan(
            u, _pad_rows(od_conv_w[0, d], SUBLANES), od_conv_b[0, d].reshape(1, LRU_WIDTH),
            od_w_a[0, d].astype(BF16), od_b_a[0, d].reshape(1, LRU_WIDTH),
            od_w_x[0, d].astype(BF16), od_b_x[0, d].reshape(1, LRU_WIDTH),
            od_lam[0, d].reshape(1, LRU_WIDTH), reverse=bool(d)))
    wr, br = _router_params(moe_w_grp[l], moe_b_grp[l], moe_w_rt[l], moe_b_rt[l])
    x1, hf, meta, counts = _post_odd(y, h_dirs[0], h_dirs[1], od_w_out[0].astype(BF16), xs, mod,
                                     norm_ffn[l].reshape(1, D_MODEL), wr, br)
    return _moe(hf, meta, counts, x1, mod, moe_w_gate[l], moe_w_up[l], moe_w_down[l],
                tiles_per_batch=N_LAT_TILES)
```

```python
import functools
import math

import jax
import jax.numpy as jnp
from jax import lax
from jax.experimental import pallas as pl
from jax.experimental.pallas import tpu as pltpu

F32 = jnp.float32
BF16 = jnp.bfloat16
HIGHEST = lax.Precision.HIGHEST

D_MODEL = 1024
SEQ = 8192
CTX_LEN = 256
TOK = SEQ + CTX_LEN
GRID_W = 64
EPS = 1e-6

CONV_WIDTH = 512
DA_HEADS = 4
DA_HEAD_DIM = 64
DA_V_DIM = 2 * DA_HEAD_DIM
DA_WIDTH = DA_HEADS * DA_V_DIM
EVEN_IN = 3 * CONV_WIDTH + 3 * DA_WIDTH
ROPE_BASE = 10000.0

LRU_WIDTH = 1024
LRU_BLOCKS = 8
LRU_BLOCK = LRU_WIDTH // LRU_BLOCKS
LRU_CONV_K = 4
LRU_C = 8.0

N_GROUPS = 4
EXPERTS_PER_GROUP = 4
N_EXPERTS = N_GROUPS * EXPERTS_PER_GROUP
D_EXPERT = 512

LANES = 128
SUBLANES = 8
TM = 256
N_LAT_TILES = SEQ // TM
N_TILES = TOK // TM
NEG = -1e30
MIB = 2 ** 20


def _cparams(semantics, vmem_mib):
    return pltpu.CompilerParams(dimension_semantics=semantics, vmem_limit_bytes=vmem_mib * MIB)


def _sigmoid(x):
    return 0.5 * jnp.tanh(0.5 * x) + 0.5


def _norm_mod(x, g, shift, scale):
    ms = jnp.mean(x * x, axis=-1, keepdims=True)
    return (x * lax.rsqrt(ms + EPS) * g) * (1.0 + scale) + shift


ADA_TN = 1536


def _ada_kernel(c_ref, w_ref, b_ref, o_ref):
    c = c_ref[...]
    a = c * _sigmoid(c)
    o_ref[...] = jnp.dot(a, w_ref[...], precision=HIGHEST, preferred_element_type=F32) + b_ref[...]


def _ada_mod(cvec, ada_w, ada_b):
    depth, d, n = ada_w.shape
    return pl.pallas_call(
        _ada_kernel,
        grid=(depth, n // ADA_TN),
        in_specs=[
            pl.BlockSpec((SUBLANES, d), lambda l, j: (0, 0)),
            pl.BlockSpec((None, d, ADA_TN), lambda l, j: (l, 0, j)),
            pl.BlockSpec((None, 1, ADA_TN), lambda l, j: (l, 0, j)),
        ],
        out_specs=pl.BlockSpec((None, SUBLANES, ADA_TN), lambda l, j: (l, 0, j)),
        out_shape=jax.ShapeDtypeStruct((depth, SUBLANES, n), F32),
        compiler_params=_cparams(("arbitrary", "arbitrary"), 40),
        name="ada_mod",
    )(cvec, ada_w, ada_b.reshape(depth, 1, n))


def _mod_index(b, i):
    return (b * 2 + i // N_LAT_TILES, 0, 0)


def _inproj_even_kernel(x_ref, mod_ref, g_ref, w_ref, qg_ref, kg_ref, cos_ref, sin_ref, bd_ref,
                        cb_ref, p_ref, q_ref, k_ref, v_ref):
    h = _norm_mod(x_ref[...], g_ref[...], mod_ref[0:1, :], mod_ref[1:2, :])
    z = jnp.dot(h.astype(BF16), w_ref[...], preferred_element_type=F32)
    cw = CONV_WIDTH
    cb_ref[...] = z[:, :cw]
    p_ref[...] = z[:, cw:2 * cw] * z[:, 2 * cw:3 * cw]

    reps = DA_WIDTH // LANES
    cosf = jnp.concatenate([cos_ref[...]] * reps, axis=1)
    sinf = jnp.concatenate([sin_ref[...]] * reps, axis=1)
    lane = lax.broadcasted_iota(jnp.int32, (TM, DA_WIDTH), 1)
    first_half = (lane & (DA_HEAD_DIM - 1)) < DA_HEAD_DIM // 2
    bd = bd_ref[...]

    def head_norm_rope(t, gain):
        tt = t * t
        hi = tt.astype(BF16)
        lo = (tt - hi.astype(F32)).astype(BF16)
        ms = (jnp.dot(hi, bd, preferred_element_type=F32) + jnp.dot(lo, bd, preferred_element_type=F32))
        y = t * lax.rsqrt(ms + EPS) * gain
        fwd = pltpu.roll(y, DA_WIDTH - DA_HEAD_DIM // 2, axis=1)
        bwd = pltpu.roll(y, DA_HEAD_DIM // 2, axis=1)
        return y * cosf + jnp.where(first_half, fwd, bwd) * sinf

    base = 3 * cw
    q = head_norm_rope(z[:, base:base + DA_WIDTH], qg_ref[...])
    q_ref[...] = (q * (DA_HEAD_DIM ** -0.5 * math.log2(math.e))).astype(BF16)
    k = head_norm_rope(z[:, base + DA_WIDTH:base + 2 * DA_WIDTH], kg_ref[...])
    k_ref[...] = k.astype(BF16)
    v = z[:, base + 2 * DA_WIDTH:].astype(BF16)
    ones = jnp.ones((TM, DA_V_DIM), BF16)
    v_ref[...] = jnp.concatenate(
        [blk for h in range(DA_HEADS) for blk in (v[:, h * DA_V_DIM:(h + 1) * DA_V_DIM], ones)], axis=1)


def _inproj_even(xs, mod, g, w_in, qg, kg, cos, sin, bd):
    bsz = xs.shape[0]
    tok_spec = lambda width: pl.BlockSpec((None, TM, width), lambda b, i: (b, i, 0))
    const = lambda shape: pl.BlockSpec(shape, lambda b, i: (0,) * len(shape))
    return pl.pallas_call(
        _inproj_even_kernel,
        grid=(bsz, N_TILES),
        in_specs=[
            tok_spec(D_MODEL),
            pl.BlockSpec((None, SUBLANES, D_MODEL), _mod_index),
            const((1, D_MODEL)),
            const((D_MODEL, EVEN_IN)),
            const((1, DA_WIDTH)),
            const((1, DA_WIDTH)),
            pl.BlockSpec((TM, LANES), lambda b, i: (i, 0)),
            pl.BlockSpec((TM, LANES), lambda b, i: (i, 0)),
            const((DA_WIDTH, DA_WIDTH)),
        ],
        out_specs=[tok_spec(CONV_WIDTH), tok_spec(CONV_WIDTH), tok_spec(DA_WIDTH), tok_spec(DA_WIDTH),
                   tok_spec(2 * DA_WIDTH)],
        out_shape=[
            jax.ShapeDtypeStruct((bsz, TOK, CONV_WIDTH), F32),
            jax.ShapeDtypeStruct((bsz, TOK, CONV_WIDTH), F32),
            jax.ShapeDtypeStruct((bsz, TOK, DA_WIDTH), BF16),
            jax.ShapeDtypeStruct((bsz, TOK, DA_WIDTH), BF16),
            jax.ShapeDtypeStruct((bsz, TOK, 2 * DA_WIDTH), BF16),
        ],
        compiler_params=_cparams(("parallel", "arbitrary"), 48),
        name="inproj_even",
    )(xs, mod, g, w_in, qg, kg, cos, sin, bd)


def _attn_kernel(lam_ref, subg_ref, q_ref, k_ref, v_ref, o_ref, qz_ref, m_ref, acc_ref,
                 sa_ref, sb_ref, *, tq, tk, nk, lam_init):
    q = q_ref[...]
    lane = lax.broadcasted_iota(jnp.int32, (tq, LANES), 1)
    zero = jnp.zeros_like(q)
    qz_ref[0:tq, :] = jnp.where(lane < DA_HEAD_DIM, q, zero)
    qz_ref[tq:2 * tq, :] = jnp.where(lane >= DA_HEAD_DIM, q, zero)
    m_ref[...] = jnp.full(m_ref.shape, NEG, F32)
    acc_ref[...] = jnp.zeros(acc_ref.shape, F32)

    def scores(j, dst_ref):
        off = pl.multiple_of(j * tk, tk)
        dst_ref[...] = lax.dot_general(qz_ref[...], k_ref[pl.ds(off, tk), :], (((1,), (1,)), ((), ())),
                                       preferred_element_type=F32)

    def update(j, src_ref):
        off = pl.multiple_of(j * tk, tk)
        s = src_ref[...]
        m_prev = m_ref[...]
        m_new = jnp.maximum(m_prev, jnp.max(s, axis=1, keepdims=True))
        alpha = jnp.exp2(m_prev - m_new)
        p = jnp.exp2((s - m_new[:, :1]).astype(BF16))
        pv = jnp.dot(p, v_ref[pl.ds(off, tk), :], preferred_element_type=F32)
        acc_ref[:, :DA_V_DIM] = alpha * acc_ref[:, :DA_V_DIM] + pv[:, :DA_V_DIM]
        acc_ref[:, DA_V_DIM:] = alpha * acc_ref[:, DA_V_DIM:] + pv[:, DA_V_DIM:]
        m_ref[...] = m_new

    bufs = (sa_ref, sb_ref)
    scores(0, bufs[0])
    for j in range(nk):
        if j + 1 < nk:
            scores(j + 1, bufs[(j + 1) % 2])
        update(j, bufs[j % 2])

    o = acc_ref[:, :DA_V_DIM] / acc_ref[:, DA_V_DIM:]
    lv = lam_ref[...]
    lam = (jnp.exp(jnp.sum(lv[0:1, :] * lv[1:2, :], axis=1, keepdims=True))
           - jnp.exp(jnp.sum(lv[2:3, :] * lv[3:4, :], axis=1, keepdims=True)) + lam_init)
    d = o[:tq] - lam * o[tq:]
    ms = jnp.mean(d * d, axis=1, keepdims=True)
    o_ref[...] = (d * lax.rsqrt(ms + EPS) * subg_ref[...] * (1.0 - lam_init)).astype(BF16)


def _attention(q, k, v, lamv, subg, *, tq, q_start, n_q, kv_start, kv_len, tk, lam_init):
    bsz = q.shape[0]
    q_blk0 = q_start // tq
    kv_blk = kv_start // kv_len
    q_spec = pl.BlockSpec((None, tq, LANES), lambda b, h, i: (b, q_blk0 + i, h))
    o_spec = pl.BlockSpec((None, tq, LANES), lambda b, h, i: (b, i, h))
    k_spec = pl.BlockSpec((None, kv_len, LANES), lambda b, h, i: (b, kv_blk, h))
    v_spec = pl.BlockSpec((None, kv_len, 2 * DA_V_DIM), lambda b, h, i: (b, kv_blk, h))
    const = lambda shape: pl.BlockSpec(shape, lambda b, h, i: (0,) * len(shape))
    return pl.pallas_call(
        functools.partial(_attn_kernel, tq=tq, tk=tk, nk=kv_len // tk, lam_init=lam_init),
        grid=(bsz, DA_HEADS, n_q),
        in_specs=[const((SUBLANES, LANES)), const((1, LANES)), q_spec, k_spec, v_spec],
        out_specs=o_spec,
        out_shape=jax.ShapeDtypeStruct((bsz, n_q * tq, DA_WIDTH), BF16),
        scratch_shapes=[
            pltpu.VMEM((2 * tq, LANES), BF16),
            pltpu.VMEM((2 * tq, LANES), F32),
            pltpu.VMEM((2 * tq, 2 * DA_V_DIM), F32),
            pltpu.VMEM((2 * tq, tk), F32),
            pltpu.VMEM((2 * tq, tk), F32),
        ],
        compiler_params=_cparams(("parallel", "parallel", "arbitrary"), 48),
        name="diff_attn",
    )(lamv, subg, q, k, v)


PAIRS_PER_GROUP = EXPERTS_PER_GROUP * (EXPERTS_PER_GROUP - 1) // 2
N_BUCKETS = N_GROUPS * PAIRS_PER_GROUP
META_BUCKET, META_RANK, META_W_LO, META_W_HI = 0, 1, 2, 3


def _post_tail(y, first_step, x_ref, mod_ref, g_ref, wr_ref, br_ref, x1_ref, hf_ref, meta_ref, rt_ref, cnt_ref,
               cnt_scr):
    x1 = x_ref[...] + mod_ref[2:3, :] * y
    x1_ref[...] = x1
    hf = _norm_mod(x1, g_ref[...], mod_ref[3:4, :], mod_ref[4:5, :])
    hf_ref[...] = hf

    hf_hi = hf.astype(BF16)
    hf_lo = (hf - hf_hi.astype(F32)).astype(BF16)
    logits = (jnp.dot(hf_hi, wr_ref[0], preferred_element_type=F32)
              + jnp.dot(hf_lo, wr_ref[0], preferred_element_type=F32)
              + jnp.dot(hf_hi, wr_ref[1], preferred_element_type=F32)) + br_ref[...]
    lane = lax.broadcasted_iota(jnp.int32, logits.shape, 1).astype(F32)
    big = float(LANES)
    is_g = (lane >= N_EXPERTS) & (lane < N_EXPERTS + N_GROUPS)
    gl = jnp.where(is_g, logits, NEG)
    gm = jnp.max(gl, axis=1, keepdims=True)
    g_idx = jnp.min(jnp.where(gl == gm, lane, big), axis=1, keepdims=True) - N_EXPERTS
    p_sel = 1.0 / jnp.sum(jnp.exp(gl - gm), axis=1, keepdims=True)
    lo = g_idx * EXPERTS_PER_GROUP
    el = jnp.where((lane >= lo) & (lane < lo + EXPERTS_PER_GROUP), logits, NEG)
    v1 = jnp.max(el, axis=1, keepdims=True)
    i1 = jnp.min(jnp.where(el == v1, lane, big), axis=1, keepdims=True)
    el2 = jnp.where(lane == i1, NEG, el)
    v2 = jnp.max(el2, axis=1, keepdims=True)
    i2 = jnp.min(jnp.where(el2 == v2, lane, big), axis=1, keepdims=True)
    t = jnp.exp(v2 - v1)
    w1 = p_sel / (1.0 + t)
    w2 = t * w1

    first_lower = i1 < i2
    a = jnp.minimum(i1, i2) - lo
    b = jnp.maximum(i1, i2) - lo
    bucket = g_idx * PAIRS_PER_GROUP + a * (7.0 - a) * 0.5 + (b - a - 1.0)
    w_lo = jnp.where(first_lower, w1, w2)
    w_hi = jnp.where(first_lower, w2, w1)

    @pl.when(first_step)
    def _():
        cnt_scr[...] = jnp.zeros(cnt_scr.shape, F32)

    tm = logits.shape[0]
    onehot = jnp.where(lane == bucket, 1.0, 0.0)
    r_i = lax.broadcasted_iota(jnp.int32, (tm, tm), 0)
    c_i = lax.broadcasted_iota(jnp.int32, (tm, tm), 1)
    earlier = jnp.where(c_i < r_i, 1.0, 0.0).astype(BF16)
    prefix = jnp.dot(earlier, onehot.astype(BF16), preferred_element_type=F32)
    base = cnt_scr[0:1, :]
    rank = jnp.sum(onehot * (prefix + base), axis=1, keepdims=True)
    counts = jnp.broadcast_to(base + jnp.sum(onehot, axis=0, keepdims=True), cnt_scr.shape)
    cnt_scr[...] = counts
    cnt_ref[...] = counts
    meta = (jnp.where(lane == META_BUCKET, bucket, 0.0) + jnp.where(lane == META_RANK, rank, 0.0)
            + jnp.where(lane == META_W_LO, w_lo, 0.0) + jnp.where(lane == META_W_HI, w_hi, 0.0))
    meta_ref[...] = meta
    rt_ref[...] = jnp.transpose(meta)[0:SUBLANES, :]


def _post_even_kernel(cb_ref, p_ref, pprev_ref, pnext_ref, cw_ref, cbias_ref, olat_ref, octx_ref, w_ref,
                      x_ref, mod_ref, g_ref, wr_ref, br_ref, x1_ref, hf_ref, meta_ref, rt_ref, cnt_ref, cnt_scr):
    i = pl.program_id(1)
    first_step = jnp.logical_and(pl.program_id(0) == 0, i == 0)
    pc = p_ref[...]
    row = lax.broadcasted_iota(jnp.int32, pc.shape, 0)
    has_prev = jnp.logical_and(i != 0, i != N_LAT_TILES)
    has_next = i < N_LAT_TILES - 1
    prev_row = jnp.where(has_prev, pprev_ref[SUBLANES - 1:SUBLANES, :], 0.0)
    next_row = jnp.where(has_next, pnext_ref[0:1, :], 0.0)
    up = jnp.where(row == 0, prev_row, pltpu.roll(pc, 1, axis=0))
    dn = jnp.where(row == TM - 1, next_row, pltpu.roll(pc, TM - 1, axis=0))
    conv = cbias_ref[...] + cw_ref[0:1, :] * up + cw_ref[1:2, :] * pc + cw_ref[2:3, :] * dn
    out_a = (cb_ref[...] * conv).astype(BF16)
    o = jnp.where(i == N_LAT_TILES, octx_ref[...], olat_ref[...])
    y = (jnp.dot(out_a, w_ref[0:CONV_WIDTH, :], preferred_element_type=F32)
         + jnp.dot(o, w_ref[CONV_WIDTH:, :], preferred_element_type=F32))
    _post_tail(y, first_step, x_ref, mod_ref, g_ref, wr_ref, br_ref, x1_ref, hf_ref, meta_ref, rt_ref, cnt_ref,
               cnt_scr)


def _post_odd_kernel(y_ref, hf_in_ref, hb_in_ref, w_ref, x_ref, mod_ref, g_ref, wr_ref, br_ref,
                     x1_ref, hf_ref, meta_ref, rt_ref, cnt_ref, cnt_scr):
    first_step = jnp.logical_and(pl.program_id(0) == 0, pl.program_id(1) == 0)
    a = (y_ref[...] * (hf_in_ref[...] + hb_in_ref[...])).astype(BF16)
    y = jnp.dot(a, w_ref[...], preferred_element_type=F32)
    _post_tail(y, first_step, x_ref, mod_ref, g_ref, wr_ref, br_ref, x1_ref, hf_ref, meta_ref, rt_ref, cnt_ref,
               cnt_scr)


def _post_specs(bsz, rows):
    tok_spec = lambda width: pl.BlockSpec((None, TM, width), lambda b, i: (b, i, 0))
    const = lambda shape: pl.BlockSpec(shape, lambda b, i: (0,) * len(shape))
    tail_in = [tok_spec(D_MODEL), pl.BlockSpec((None, SUBLANES, D_MODEL), _mod_index), const((1, D_MODEL)),
               const((2, D_MODEL, LANES)), const((1, LANES))]
    tiles = rows // TM
    out_specs = [tok_spec(D_MODEL), tok_spec(D_MODEL), tok_spec(LANES),
                 pl.BlockSpec((None, SUBLANES, TM), lambda b, i: (b * tiles + i, 0, 0)), const((SUBLANES, LANES))]
    out_shape = [jax.ShapeDtypeStruct((bsz, rows, D_MODEL), F32),
                 jax.ShapeDtypeStruct((bsz, rows, D_MODEL), F32),
                 jax.ShapeDtypeStruct((bsz, rows, LANES), F32),
                 jax.ShapeDtypeStruct((bsz * tiles, SUBLANES, TM), F32),
                 jax.ShapeDtypeStruct((SUBLANES, LANES), F32)]
    scratch = [pltpu.VMEM((SUBLANES, LANES), F32)]
    return tok_spec, const, tail_in, out_specs, out_shape, scratch


def _post_even(cb, p, conv_w, conv_b, o_lat, o_ctx, w_out, xs, mod, g, wr, br):
    bsz = xs.shape[0]
    tok_spec, const, tail_in, out_specs, out_shape, scratch = _post_specs(bsz, TOK)
    olat_spec = pl.BlockSpec((None, TM, DA_WIDTH), lambda b, i: (b, jnp.minimum(i, N_LAT_TILES - 1), 0))
    octx_spec = pl.BlockSpec((None, TM, DA_WIDTH), lambda b, i: (b, 0, 0))
    halo_blocks = TM // SUBLANES
    last_halo = TOK // SUBLANES - 1
    prev_spec = pl.BlockSpec((None, SUBLANES, CONV_WIDTH),
                             lambda b, i: (b, jnp.maximum(i * halo_blocks - 1, 0), 0))
    next_spec = pl.BlockSpec((None, SUBLANES, CONV_WIDTH),
                             lambda b, i: (b, jnp.minimum((i + 1) * halo_blocks, last_halo), 0))
    return pl.pallas_call(
        _post_even_kernel,
        grid=(bsz, N_TILES),
        in_specs=[tok_spec(CONV_WIDTH), tok_spec(CONV_WIDTH), prev_spec, next_spec,
                  const((SUBLANES, CONV_WIDTH)), const((1, CONV_WIDTH)), olat_spec, octx_spec,
                  const((D_MODEL, D_MODEL))] + tail_in,
        out_specs=out_specs,
        out_shape=out_shape,
        scratch_shapes=scratch,
        compiler_params=_cparams(("arbitrary", "arbitrary"), 48),
        name="post_even",
    )(cb, p, p, p, conv_w, conv_b, o_lat, o_ctx, w_out, xs, mod, g, wr, br)


def _post_odd(y, hfw, hbw, w_out, xs, mod, g, wr, br):
    bsz = xs.shape[0]
    tok_spec, const, tail_in, out_specs, out_shape, scratch = _post_specs(bsz, SEQ)
    return pl.pallas_call(
        _post_odd_kernel,
        grid=(bsz, N_LAT_TILES),
        in_specs=[tok_spec(LRU_WIDTH), tok_spec(LRU_WIDTH), tok_spec(LRU_WIDTH),
                  const((LRU_WIDTH, D_MODEL))] + tail_in,
        out_specs=out_specs,
        out_shape=out_shape,
        scratch_shapes=scratch,
        compiler_params=_cparams(("arbitrary", "arbitrary"), 48),
        name="post_odd",
    )(y, hfw, hbw, w_out, xs, mod, g, wr, br)


_PAIRS = [(a, b) for a in range(EXPERTS_PER_GROUP) for b in range(a + 1, EXPERTS_PER_GROUP)]
_BUCKET_LO = [g * EXPERTS_PER_GROUP + a for g in range(N_GROUPS) for a, _ in _PAIRS]
_BUCKET_HI = [g * EXPERTS_PER_GROUP + b for g in range(N_GROUPS) for _, b in _PAIRS]


def _sorted_tiles(n_tokens):
    return n_tokens // TM + N_BUCKETS


def _route_plan(rt, counts, n_tokens):
    n_tiles = _sorted_tiles(n_tokens)
    bucket = rt[:, META_BUCKET, :].astype(jnp.int32).reshape(n_tokens)
    rank = rt[:, META_RANK, :].astype(jnp.int32).reshape(n_tokens)
    cnt = counts[0, :N_BUCKETS].astype(jnp.int32)
    tiles_per = (cnt + TM - 1) // TM
    tile_end = jnp.cumsum(tiles_per)
    row_start = (tile_end - tiles_per) * TM
    dest = (row_start[bucket] + rank).reshape(n_tokens // TM, 1, TM)
    tile_bucket = jnp.minimum(jnp.sum(jnp.arange(n_tiles)[:, None] >= tile_end[None, :], axis=1), N_BUCKETS - 1)
    e_lo = jnp.asarray(_BUCKET_LO, jnp.int32)[tile_bucket]
    e_hi = jnp.asarray(_BUCKET_HI, jnp.int32)[tile_bucket]
    return dest, e_lo, e_hi, tile_end[-1:].astype(jnp.int32)


def _invert_kernel(dest_ref, src_ref):
    i = pl.program_id(0)

    @pl.when(i == 0)
    def _():
        def clear(k, carry):
            src_ref[k] = 0
            return carry
        lax.fori_loop(0, src_ref.shape[0], clear, 0, unroll=8)

    base = i * TM

    def put(r, carry):
        src_ref[dest_ref[0, r]] = base + r
        return carry

    lax.fori_loop(0, TM, put, 0, unroll=8)


def _invert(dest, n_tokens):
    rows = _sorted_tiles(n_tokens) * TM
    return pl.pallas_call(
        _invert_kernel,
        grid=(n_tokens // TM,),
        in_specs=[pl.BlockSpec((None, 1, TM), lambda i: (i, 0, 0), memory_space=pltpu.SMEM)],
        out_specs=pl.BlockSpec(memory_space=pltpu.SMEM),
        out_shape=jax.ShapeDtypeStruct((rows,), jnp.int32),
        compiler_params=_cparams(("arbitrary",), 16),
        name="moe_invert",
    )(dest)


def _gather_rows(idx_ref, src_hbm, dst_ref, sem):
    def issue(r, carry):
        pltpu.make_async_copy(src_hbm.at[pl.ds(idx_ref[0, r], 1), :], dst_ref.at[pl.ds(r, 1), :], sem).start()
        return carry
    lax.fori_loop(0, TM, issue, 0, unroll=8)


def _gather_wait(src_hbm, dst_ref, sem):
    pltpu.make_async_copy(src_hbm.at[pl.ds(0, TM), :], dst_ref, sem).wait()


def _moe_routed_kernel(elo_ref, ehi_ref, nused_ref, src_ref, src_next_ref, hf_hbm,
                       wg_lo, wu_lo, wd_lo, wg_hi, wu_hi, wd_hi, y_ref, hbuf_ref, sems):
    del elo_ref, ehi_ref
    j = pl.program_id(0)
    n_used = nused_ref[0]
    slot = j % 2

    @pl.when(j == 0)
    def _():
        _gather_rows(src_ref, hf_hbm, hbuf_ref.at[0], sems.at[0])

    @pl.when(j < n_used)
    def _():
        _gather_wait(hf_hbm, hbuf_ref.at[slot], sems.at[slot])
        for r in range(TM):
            pltpu.make_async_copy(hf_hbm.at[pl.ds(src_next_ref[0, r], 1), :],
                                  hbuf_ref.at[1 - slot, pl.ds(r, 1), :], sems.at[1 - slot]).start()
        h = hbuf_ref[slot].astype(BF16)

        def expert(wg_ref, wu_ref, wd_ref):
            hg = jnp.dot(h, wg_ref[...].astype(BF16), preferred_element_type=F32)
            hu = jnp.dot(h, wu_ref[...].astype(BF16), preferred_element_type=F32)
            act = (hg * _sigmoid(hg)) * hu
            return jnp.dot(act.astype(BF16), wd_ref[...].astype(BF16), preferred_element_type=F32)

        y_ref[:, :D_MODEL] = expert(wg_lo, wu_lo, wd_lo)
        y_ref[:, D_MODEL:] = expert(wg_hi, wu_hi, wd_hi)

    @pl.when(j == n_used)
    def _():
        _gather_wait(hf_hbm, hbuf_ref.at[slot], sems.at[slot])

    @pl.when(j >= n_used)
    def _():
        y_ref[...] = jnp.zeros(y_ref.shape, F32)


def _moe_routed(src, hf, e_lo, e_hi, n_used, wg, wu, wd):
    n_tiles = src.shape[0]
    up_spec = lambda tbl: pl.BlockSpec((None, D_MODEL, D_EXPERT), lambda j, lo, hi, nu: ((lo, hi)[tbl][j], 0, 0))
    dn_spec = lambda tbl: pl.BlockSpec((None, D_EXPERT, D_MODEL), lambda j, lo, hi, nu: ((lo, hi)[tbl][j], 0, 0))
    grid_spec = pltpu.PrefetchScalarGridSpec(
        num_scalar_prefetch=3,
        grid=(n_tiles,),
        in_specs=[pl.BlockSpec((None, 1, TM), lambda j, lo, hi, nu: (j, 0, 0), memory_space=pltpu.SMEM),
                  pl.BlockSpec((None, 1, TM), lambda j, lo, hi, nu: (jnp.minimum(j + 1, n_tiles - 1), 0, 0),
                               memory_space=pltpu.SMEM),
                  pl.BlockSpec(memory_space=pl.ANY),
                  up_spec(0), up_spec(0), dn_spec(0), up_spec(1), up_spec(1), dn_spec(1)],
        out_specs=pl.BlockSpec((TM, 2 * D_MODEL), lambda j, lo, hi, nu: (j, 0)),
        scratch_shapes=[pltpu.VMEM((2, TM, D_MODEL), F32), pltpu.SemaphoreType.DMA((2,))],
    )
    return pl.pallas_call(
        _moe_routed_kernel,
        grid_spec=grid_spec,
        out_shape=jax.ShapeDtypeStruct((n_tiles * TM, 2 * D_MODEL), F32),
        compiler_params=_cparams(("arbitrary",), 56),
        name="moe_routed",
    )(e_lo, e_hi, n_used, src, src, hf, wg, wu, wd, wg, wu, wd)


def _combine_kernel(dest_ref, dest_next_ref, y_hbm, x_ref, meta_ref, mod_ref, o_ref, buf_ref, sems):
    step = pl.program_id(0) * pl.num_programs(1) + pl.program_id(1)
    n_steps = pl.num_programs(0) * pl.num_programs(1)
    slot = step % 2

    @pl.when(step == 0)
    def _():
        _gather_rows(dest_ref, y_hbm, buf_ref.at[0], sems.at[0])

    @pl.when(step + 1 < n_steps)
    def _():
        _gather_rows(dest_next_ref, y_hbm, buf_ref.at[1 - slot], sems.at[1 - slot])

    _gather_wait(y_hbm, buf_ref.at[slot], sems.at[slot])
    meta = meta_ref[...]
    moe = (meta[:, META_W_LO:META_W_LO + 1] * buf_ref[slot, :, :D_MODEL]
           + meta[:, META_W_HI:META_W_HI + 1] * buf_ref[slot, :, D_MODEL:])
    o_ref[...] = x_ref[...] + mod_ref[5:6, :] * moe


def _combine(dest, y, x1, meta, mod, *, tiles_per_batch):
    bsz = x1.shape[0]
    n_steps = bsz * tiles_per_batch
    tok_spec = lambda width: pl.BlockSpec((None, TM, width), lambda b, i: (b, i, 0))
    idx_spec = lambda ahead: pl.BlockSpec(
        (None, 1, TM), lambda b, i: (jnp.minimum(b * tiles_per_batch + i + ahead, n_steps - 1), 0, 0),
        memory_space=pltpu.SMEM)
    return pl.pallas_call(
        _combine_kernel,
        grid=(bsz, tiles_per_batch),
        in_specs=[idx_spec(0), idx_spec(1), pl.BlockSpec(memory_space=pl.ANY), tok_spec(D_MODEL),
                  tok_spec(LANES), pl.BlockSpec((None, SUBLANES, D_MODEL), _mod_index)],
        out_specs=tok_spec(D_MODEL),
        out_shape=jax.ShapeDtypeStruct(x1.shape, F32),
        scratch_shapes=[pltpu.VMEM((2, TM, 2 * D_MODEL), F32), pltpu.SemaphoreType.DMA((2,))],
        compiler_params=_cparams(("arbitrary", "arbitrary"), 32),
        name="moe_combine",
    )(dest, dest, y, x1, meta, mod)


def _moe(hf, meta, rt, counts, x1, mod, wg, wu, wd, *, tiles_per_batch):
    n_tokens = hf.shape[0] * hf.shape[1]
    dest, e_lo, e_hi, n_used = _route_plan(rt, counts, n_tokens)
    src = _invert(dest, n_tokens).reshape(_sorted_tiles(n_tokens), 1, TM)
    y = _moe_routed(src, hf.reshape(n_tokens, D_MODEL), e_lo, e_hi, n_used, wg, wu, wd)
    return _combine(dest, y, x1, meta, mod, tiles_per_batch=tiles_per_batch)


def _inproj_odd_kernel(x_ref, mod_ref, g_ref, w_ref, y_ref, u_ref):
    h = _norm_mod(x_ref[...], g_ref[...], mod_ref[0:1, :], mod_ref[1:2, :])
    z = jnp.dot(h.astype(BF16), w_ref[...], preferred_element_type=F32)
    zy = z[:, :LRU_WIDTH]
    c0 = math.sqrt(2.0 / math.pi)
    y_ref[...] = 0.5 * zy * (1.0 + jnp.tanh(c0 * (zy + 0.044715 * (zy * zy * zy))))
    u_ref[...] = z[:, LRU_WIDTH:]


def _inproj_odd(xs, mod, g, w_in):
    bsz = xs.shape[0]
    tok_spec = lambda width: pl.BlockSpec((None, TM, width), lambda b, i: (b, i, 0))
    const = lambda shape: pl.BlockSpec(shape, lambda b, i: (0,) * len(shape))
    return pl.pallas_call(
        _inproj_odd_kernel,
        grid=(bsz, N_TILES),
        in_specs=[tok_spec(D_MODEL), pl.BlockSpec((None, SUBLANES, D_MODEL), _mod_index),
                  const((1, D_MODEL)), const((D_MODEL, 2 * LRU_WIDTH))],
        out_specs=[tok_spec(LRU_WIDTH), tok_spec(LRU_WIDTH)],
        out_shape=[jax.ShapeDtypeStruct((bsz, TOK, LRU_WIDTH), F32),
                   jax.ShapeDtypeStruct((bsz, TOK, LRU_WIDTH), F32)],
        compiler_params=_cparams(("parallel", "arbitrary"), 48),
        name="inproj_odd",
    )(xs, mod, g, w_in)


def _scan_kernel(u_ref, cw_ref, cbias_ref, wa_ref, ba_ref, wx_ref, bx_ref, lam_ref, h_ref,
                 halo_ref, carry_ref, a_ref, b_ref, *, reverse):
    i = pl.program_id(1)
    tt, w = u_ref.shape
    n_groups = tt // SUBLANES

    @pl.when(i <= 1)
    def _():
        halo_ref[...] = jnp.zeros(halo_ref.shape, F32)

    @pl.when(i == 0)
    def _():
        carry_ref[...] = jnp.zeros(carry_ref.shape, F32)

    u = u_ref[...]
    halo = halo_ref[...]
    row8 = lax.broadcasted_iota(jnp.int32, (SUBLANES, w), 0)
    k_self = 0 if reverse else LRU_CONV_K - 1
    uc = cbias_ref[...] + cw_ref[k_self:k_self + 1, :] * u
    for k in range(1, LRU_CONV_K):
        if reverse:
            tmp = pltpu.roll(u, tt - k, axis=0)
            hr = pltpu.roll(halo, SUBLANES - k, axis=0)
            edge = jnp.where(row8 >= SUBLANES - k, hr, tmp[tt - SUBLANES:, :])
            shifted = jnp.concatenate([tmp[:tt - SUBLANES, :], edge], axis=0)
            wk = cw_ref[k:k + 1, :]
        else:
            tmp = pltpu.roll(u, k, axis=0)
            hr = pltpu.roll(halo, k, axis=0)
            edge = jnp.where(row8 < k, hr, tmp[:SUBLANES, :])
            shifted = jnp.concatenate([edge, tmp[SUBLANES:, :]], axis=0)
            wk = cw_ref[LRU_CONV_K - 1 - k:LRU_CONV_K - k, :]
        uc = uc + wk * shifted
    halo_ref[...] = u[:SUBLANES, :] if reverse else u[tt - SUBLANES:, :]

    ucb = uc.astype(BF16)

    def block_diag(w_blocks):
        return jnp.concatenate(
            [jnp.dot(ucb[:, j * LRU_BLOCK:(j + 1) * LRU_BLOCK], w_blocks[j], preferred_element_type=F32)
             for j in range(LRU_BLOCKS)], axis=1)

    r = _sigmoid(block_diag(wa_ref) + ba_ref[...])
    gate_i = _sigmoid(block_diag(wx_ref) + bx_ref[...])
    neg_lam = -lam_ref[...]
    softplus = jnp.maximum(neg_lam, 0.0) + jnp.log1p(jnp.exp(-jnp.abs(neg_lam)))
    a = jnp.exp((-LRU_C * softplus) * r)
    a_ref[...] = a
    one_m_a2 = 1.0 - a * a
    root = jnp.where(one_m_a2 > 0.0, one_m_a2 * lax.rsqrt(one_m_a2), 0.0)
    b_ref[...] = root * (gate_i * uc)

    def group(gi, carry):
        g = n_groups - 1 - gi if reverse else gi
        off = pl.multiple_of(g * SUBLANES, SUBLANES)
        av = a_ref[pl.ds(off, SUBLANES), :]
        bv = b_ref[pl.ds(off, SUBLANES), :]
        for s in (1, 2, 4):
            if reverse:
                outside = row8 >= SUBLANES - s
                shift = SUBLANES - s
            else:
                outside = row8 < s
                shift = s
            a_sh = jnp.where(outside, 1.0, pltpu.roll(av, shift, axis=0))
            b_sh = jnp.where(outside, 0.0, pltpu.roll(bv, shift, axis=0))
            bv = av * b_sh + bv
            av = av * a_sh
        h = av * carry + bv
        h_ref[pl.ds(off, SUBLANES), :] = h
        last = h[0:1, :] if reverse else h[SUBLANES - 1:SUBLANES, :]
        return jnp.broadcast_to(last, (SUBLANES, w))

    carry_ref[...] = lax.fori_loop(0, n_groups, group, carry_ref[...])


def _rglru_scan(u, conv_w, conv_b, w_a, b_a, w_x, b_x, lam, *, reverse):
    bsz = u.shape[0]
    if reverse:
        tile_of = lambda i: N_LAT_TILES - i
    else:
        tile_of = lambda i: (i + N_LAT_TILES) % N_TILES
    tok_spec = pl.BlockSpec((None, TM, LRU_WIDTH), lambda b, i: (b, tile_of(i), 0))
    const = lambda shape: pl.BlockSpec(shape, lambda b, i: (0,) * len(shape))
    return pl.pallas_call(
        functools.partial(_scan_kernel, reverse=reverse),
        grid=(bsz, N_TILES),
        in_specs=[tok_spec, const((SUBLANES, LRU_WIDTH)), const((1, LRU_WIDTH)),
                  const((LRU_BLOCKS, LRU_BLOCK, LRU_BLOCK)), const((1, LRU_WIDTH)),
                  const((LRU_BLOCKS, LRU_BLOCK, LRU_BLOCK)), const((1, LRU_WIDTH)), const((1, LRU_WIDTH))],
        out_specs=tok_spec,
        out_shape=jax.ShapeDtypeStruct((bsz, TOK, LRU_WIDTH), F32),
        scratch_shapes=[pltpu.VMEM((SUBLANES, LRU_WIDTH), F32), pltpu.VMEM((SUBLANES, LRU_WIDTH), F32),
                        pltpu.VMEM((TM, LRU_WIDTH), F32), pltpu.VMEM((TM, LRU_WIDTH), F32)],
        compiler_params=_cparams(("parallel", "arbitrary"), 32),
        name="rglru_rev" if reverse else "rglru_fwd",
    )(u, conv_w, conv_b, w_a, b_a, w_x, b_x, lam)


def _pad_rows(a, rows):
    return jnp.pad(a, ((0, rows - a.shape[0]), (0, 0)))


def _rope_tables():
    t = jnp.arange(SEQ)
    n_freq = DA_HEAD_DIM // 4
    inv = ROPE_BASE ** (-jnp.arange(n_freq, dtype=F32) / n_freq)
    ang = jnp.concatenate([(t // GRID_W).astype(F32)[:, None] * inv,
                           (t % GRID_W).astype(F32)[:, None] * inv], axis=-1)
    cos, sin = jnp.cos(ang), jnp.sin(ang)
    cos64 = jnp.concatenate([cos, cos], axis=-1)
    sin64 = jnp.concatenate([-sin, sin], axis=-1)
    cos_t = jnp.concatenate([jnp.tile(cos64, (1, 2)), jnp.ones((CTX_LEN, LANES), F32)], axis=0)
    sin_t = jnp.concatenate([jnp.tile(sin64, (1, 2)), jnp.zeros((CTX_LEN, LANES), F32)], axis=0)
    return cos_t, sin_t


def _router_params(w_grp, b_grp, w_rt, b_rt):
    wr = jnp.concatenate([w_rt.reshape(D_MODEL, N_EXPERTS), w_grp], axis=1)
    br = jnp.concatenate([b_rt.reshape(N_EXPERTS), b_grp])
    pad = LANES - wr.shape[1]
    wr = jnp.pad(wr, ((0, 0), (0, pad)))
    wr_hi = wr.astype(BF16)
    wr_lo = (wr - wr_hi.astype(F32)).astype(BF16)
    return jnp.stack([wr_hi, wr_lo]), jnp.pad(br, (0, pad)).reshape(1, LANES)


def kernel(x, c, ctx, c_ctx, ada_w, ada_b, norm_mix, norm_ffn, ev_w_in, ev_conv_w, ev_conv_b, ev_q_norm, ev_k_norm, ev_lam_q1, ev_lam_k1, ev_lam_q2, ev_lam_k2, ev_sub_norm, ev_w_out, od_w_in, od_conv_w, od_conv_b, od_w_a, od_b_a, od_w_x, od_b_x, od_lam, od_w_out, moe_w_grp, moe_b_grp, moe_w_rt, moe_b_rt, moe_w_gate, moe_w_up, moe_w_down):
    bsz = x.shape[0]
    assert x.shape == (bsz, SEQ, D_MODEL) and ctx.shape == (bsz, CTX_LEN, D_MODEL) and bsz == 2
    depth = ada_w.shape[0]
    assert depth == 2

    xs = jnp.concatenate([x, ctx], axis=1)

    cvec = _pad_rows(jnp.stack([c[0], c_ctx, c[1], c_ctx]), SUBLANES)
    mod_all = _ada_mod(cvec, ada_w, ada_b)
    mod_all = mod_all[:, :2 * bsz].reshape(depth, 2 * bsz, 6, D_MODEL)
    mod_all = jnp.pad(mod_all, ((0, 0), (0, 0), (0, SUBLANES - 6), (0, 0)))

    l = 0
    lam_init = 0.8 - 0.6 * math.exp(-0.3 * l)
    mod = mod_all[l]
    cos_t, sin_t = _rope_tables()
    blk = jnp.arange(DA_WIDTH) // DA_HEAD_DIM
    bd = jnp.where(blk[:, None] == blk[None, :], 1.0 / DA_HEAD_DIM, 0.0).astype(BF16)
    n_rep = DA_WIDTH // DA_HEAD_DIM
    cb, p, q, k, v = _inproj_even(
        xs, mod, norm_mix[l].reshape(1, D_MODEL), ev_w_in[0].astype(BF16),
        jnp.tile(ev_q_norm[0], n_rep).reshape(1, DA_WIDTH), jnp.tile(ev_k_norm[0], n_rep).reshape(1, DA_WIDTH),
        cos_t, sin_t, bd)

    lamv = _pad_rows(jnp.pad(jnp.stack([ev_lam_q1[0], ev_lam_k1[0], ev_lam_q2[0], ev_lam_k2[0]]),
                             ((0, 0), (0, LANES - DA_HEAD_DIM))), SUBLANES)
    subg = ev_sub_norm[0].reshape(1, DA_V_DIM)
    o_lat = _attention(q, k, v, lamv, subg, tq=256, q_start=0, n_q=SEQ // 256,
                       kv_start=0, kv_len=TOK, tk=768, lam_init=lam_init)
    o_ctx = _attention(q, k, v, lamv, subg, tq=TM, q_start=SEQ, n_q=CTX_LEN // TM,
                       kv_start=SEQ, kv_len=CTX_LEN, tk=CTX_LEN, lam_init=lam_init)

    wr, br = _router_params(moe_w_grp[l], moe_b_grp[l], moe_w_rt[l], moe_b_rt[l])
    x1, hf, meta, rt, counts = _post_even(
        cb, p, _pad_rows(ev_conv_w[0], SUBLANES), ev_conv_b[0].reshape(1, CONV_WIDTH), o_lat, o_ctx,
        ev_w_out[0].astype(BF16), xs, mod, norm_ffn[l].reshape(1, D_MODEL), wr, br)
    xs = _moe(hf, meta, rt, counts, x1, mod, moe_w_gate[l], moe_w_up[l], moe_w_down[l], tiles_per_batch=N_TILES)

    l = 1
    mod = mod_all[l]
    y, u = _inproj_odd(xs, mod, norm_mix[l].reshape(1, D_MODEL), od_w_in[0].astype(BF16))
    h_dirs = []
    for d in range(2):
        h_dirs.append(_rglru_scan(
            u, _pad_rows(od_conv_w[0, d], SUBLANES), od_conv_b[0, d].reshape(1, LRU_WIDTH),
            od_w_a[0, d].astype(BF16), od_b_a[0, d].reshape(1, LRU_WIDTH),
            od_w_x[0, d].astype(BF16), od_b_x[0, d].reshape(1, LRU_WIDTH),
            od_lam[0, d].reshape(1, LRU_WIDTH), reverse=bool(d)))
    wr, br = _router_params(moe_w_grp[l], moe_b_grp[l], moe_w_rt[l], moe_b_rt[l])
    x1, hf, meta, rt, counts = _post_odd(y, h_dirs[0], h_dirs[1], od_w_out[0].astype(BF16), xs, mod,
                                     norm_ffn[l].reshape(1, D_MODEL), wr, br)
    return _moe(hf, meta, rt, counts, x1, mod, moe_w_gate[l], moe_w_up[l], moe_w_down[l],
                tiles_per_batch=N_LAT_TILES)
```

```python
import functools
import math

import jax
import jax.numpy as jnp
from jax import lax
from jax.experimental import pallas as pl
from jax.experimental.pallas import tpu as pltpu

F32 = jnp.float32
BF16 = jnp.bfloat16
HIGHEST = lax.Precision.HIGHEST

D_MODEL = 1024
SEQ = 8192
CTX_LEN = 256
TOK = SEQ + CTX_LEN
GRID_W = 64
EPS = 1e-6

CONV_WIDTH = 512
DA_HEADS = 4
DA_HEAD_DIM = 64
DA_V_DIM = 2 * DA_HEAD_DIM
DA_WIDTH = DA_HEADS * DA_V_DIM
EVEN_IN = 3 * CONV_WIDTH + 3 * DA_WIDTH
ROPE_BASE = 10000.0

LRU_WIDTH = 1024
LRU_BLOCKS = 8
LRU_BLOCK = LRU_WIDTH // LRU_BLOCKS
LRU_CONV_K = 4
LRU_C = 8.0

N_GROUPS = 4
EXPERTS_PER_GROUP = 4
N_EXPERTS = N_GROUPS * EXPERTS_PER_GROUP
D_EXPERT = 512

LANES = 128
SUBLANES = 8
TM = 256
N_LAT_TILES = SEQ // TM
N_TILES = TOK // TM
NEG = -1e30
MIB = 2 ** 20


def _cparams(semantics, vmem_mib):
    return pltpu.CompilerParams(dimension_semantics=semantics, vmem_limit_bytes=vmem_mib * MIB)


def _sigmoid(x):
    return 0.5 * jnp.tanh(0.5 * x) + 0.5


def _norm_mod(x, g, shift, scale):
    ms = jnp.mean(x * x, axis=-1, keepdims=True)
    return (x * lax.rsqrt(ms + EPS) * g) * (1.0 + scale) + shift


ADA_TN = 1536


def _ada_kernel(c_ref, w_ref, b_ref, o_ref):
    c = c_ref[...]
    a = c * _sigmoid(c)
    o_ref[...] = jnp.dot(a, w_ref[...], precision=HIGHEST, preferred_element_type=F32) + b_ref[...]


def _ada_mod(cvec, ada_w, ada_b):
    depth, d, n = ada_w.shape
    return pl.pallas_call(
        _ada_kernel,
        grid=(depth, n // ADA_TN),
        in_specs=[
            pl.BlockSpec((SUBLANES, d), lambda l, j: (0, 0)),
            pl.BlockSpec((None, d, ADA_TN), lambda l, j: (l, 0, j)),
            pl.BlockSpec((None, 1, ADA_TN), lambda l, j: (l, 0, j)),
        ],
        out_specs=pl.BlockSpec((None, SUBLANES, ADA_TN), lambda l, j: (l, 0, j)),
        out_shape=jax.ShapeDtypeStruct((depth, SUBLANES, n), F32),
        compiler_params=_cparams(("arbitrary", "arbitrary"), 40),
        name="ada_mod",
    )(cvec, ada_w, ada_b.reshape(depth, 1, n))


def _mod_index(b, i):
    return (b * 2 + i // N_LAT_TILES, 0, 0)


def _inproj_even_kernel(xl_ref, xc_ref, mod_ref, g_ref, w_ref, qg_ref, kg_ref, cos_ref, sin_ref, bd_ref,
                        cb_ref, p_ref, q_ref, k_ref, v_ref):
    x = jnp.where(pl.program_id(1) == N_LAT_TILES, xc_ref[...], xl_ref[...])
    h = _norm_mod(x, g_ref[...], mod_ref[0:1, :], mod_ref[1:2, :])
    z = jnp.dot(h.astype(BF16), w_ref[...], preferred_element_type=F32)
    cw = CONV_WIDTH
    cb_ref[...] = z[:, :cw]
    p_ref[...] = z[:, cw:2 * cw] * z[:, 2 * cw:3 * cw]

    reps = DA_WIDTH // LANES
    cosf = jnp.concatenate([cos_ref[...]] * reps, axis=1)
    sinf = jnp.concatenate([sin_ref[...]] * reps, axis=1)
    lane = lax.broadcasted_iota(jnp.int32, (TM, DA_WIDTH), 1)
    first_half = (lane & (DA_HEAD_DIM - 1)) < DA_HEAD_DIM // 2
    bd = bd_ref[...]

    def head_norm_rope(t, gain):
        tt = t * t
        hi = tt.astype(BF16)
        lo = (tt - hi.astype(F32)).astype(BF16)
        ms = (jnp.dot(hi, bd, preferred_element_type=F32) + jnp.dot(lo, bd, preferred_element_type=F32))
        y = t * lax.rsqrt(ms + EPS) * gain
        fwd = pltpu.roll(y, DA_WIDTH - DA_HEAD_DIM // 2, axis=1)
        bwd = pltpu.roll(y, DA_HEAD_DIM // 2, axis=1)
        return y * cosf + jnp.where(first_half, fwd, bwd) * sinf

    base = 3 * cw
    q = head_norm_rope(z[:, base:base + DA_WIDTH], qg_ref[...])
    q_ref[...] = (q * (DA_HEAD_DIM ** -0.5 * math.log2(math.e))).astype(BF16)
    k = head_norm_rope(z[:, base + DA_WIDTH:base + 2 * DA_WIDTH], kg_ref[...])
    k_ref[...] = k.astype(BF16)
    v = z[:, base + 2 * DA_WIDTH:].astype(BF16)
    ones = jnp.ones((TM, DA_V_DIM), BF16)
    v_ref[...] = jnp.concatenate(
        [blk for h in range(DA_HEADS) for blk in (v[:, h * DA_V_DIM:(h + 1) * DA_V_DIM], ones)], axis=1)


def _lat_ctx_specs(width):
    return [pl.BlockSpec((None, TM, width), lambda b, i: (b, jnp.minimum(i, N_LAT_TILES - 1), 0)),
            pl.BlockSpec((None, TM, width), lambda b, i: (b, 0, 0))]


def _inproj_even(x, ctx, mod, g, w_in, qg, kg, cos, sin, bd):
    bsz = x.shape[0]
    tok_spec = lambda width: pl.BlockSpec((None, TM, width), lambda b, i: (b, i, 0))
    const = lambda shape: pl.BlockSpec(shape, lambda b, i: (0,) * len(shape))
    return pl.pallas_call(
        _inproj_even_kernel,
        grid=(bsz, N_TILES),
        in_specs=_lat_ctx_specs(D_MODEL) + [
            pl.BlockSpec((None, SUBLANES, D_MODEL), _mod_index),
            const((1, D_MODEL)),
            const((D_MODEL, EVEN_IN)),
            const((1, DA_WIDTH)),
            const((1, DA_WIDTH)),
            pl.BlockSpec((TM, LANES), lambda b, i: (i, 0)),
            pl.BlockSpec((TM, LANES), lambda b, i: (i, 0)),
            const((DA_WIDTH, DA_WIDTH)),
        ],
        out_specs=[tok_spec(CONV_WIDTH), tok_spec(CONV_WIDTH), tok_spec(DA_WIDTH), tok_spec(DA_WIDTH),
                   tok_spec(2 * DA_WIDTH)],
        out_shape=[
            jax.ShapeDtypeStruct((bsz, TOK, CONV_WIDTH), F32),
            jax.ShapeDtypeStruct((bsz, TOK, CONV_WIDTH), F32),
            jax.ShapeDtypeStruct((bsz, TOK, DA_WIDTH), BF16),
            jax.ShapeDtypeStruct((bsz, TOK, DA_WIDTH), BF16),
            jax.ShapeDtypeStruct((bsz, TOK, 2 * DA_WIDTH), BF16),
        ],
        compiler_params=_cparams(("parallel", "arbitrary"), 48),
        name="inproj_even",
    )(x, ctx, mod, g, w_in, qg, kg, cos, sin, bd)


def _attn_kernel(lam_ref, subg_ref, q_ref, k_ref, v_ref, o_ref, qz_ref, m_ref, acc_ref,
                 sa_ref, sb_ref, *, tq, tk, nk, lam_init):
    q = q_ref[...]
    lane = lax.broadcasted_iota(jnp.int32, (tq, LANES), 1)
    zero = jnp.zeros_like(q)
    qz_ref[0:tq, :] = jnp.where(lane < DA_HEAD_DIM, q, zero)
    qz_ref[tq:2 * tq, :] = jnp.where(lane >= DA_HEAD_DIM, q, zero)
    m_ref[...] = jnp.full(m_ref.shape, NEG, F32)
    acc_ref[...] = jnp.zeros(acc_ref.shape, F32)

    def scores(j, dst_ref):
        off = pl.multiple_of(j * tk, tk)
        dst_ref[...] = lax.dot_general(qz_ref[...], k_ref[pl.ds(off, tk), :], (((1,), (1,)), ((), ())),
                                       preferred_element_type=F32)

    def update(j, src_ref):
        off = pl.multiple_of(j * tk, tk)
        s = src_ref[...]
        m_prev = m_ref[...]
        m_new = jnp.maximum(m_prev, jnp.max(s, axis=1, keepdims=True))
        alpha = jnp.exp2(m_prev - m_new)
        p = jnp.exp2((s - m_new[:, :1]).astype(BF16))
        pv = jnp.dot(p, v_ref[pl.ds(off, tk), :], preferred_element_type=F32)
        acc_ref[:, :DA_V_DIM] = alpha * acc_ref[:, :DA_V_DIM] + pv[:, :DA_V_DIM]
        acc_ref[:, DA_V_DIM:] = alpha * acc_ref[:, DA_V_DIM:] + pv[:, DA_V_DIM:]
        m_ref[...] = m_new

    bufs = (sa_ref, sb_ref)
    scores(0, bufs[0])
    for j in range(nk):
        if j + 1 < nk:
            scores(j + 1, bufs[(j + 1) % 2])
        update(j, bufs[j % 2])

    o = acc_ref[:, :DA_V_DIM] / acc_ref[:, DA_V_DIM:]
    lv = lam_ref[...]
    lam = (jnp.exp(jnp.sum(lv[0:1, :] * lv[1:2, :], axis=1, keepdims=True))
           - jnp.exp(jnp.sum(lv[2:3, :] * lv[3:4, :], axis=1, keepdims=True)) + lam_init)
    d = o[:tq] - lam * o[tq:]
    ms = jnp.mean(d * d, axis=1, keepdims=True)
    o_ref[...] = (d * lax.rsqrt(ms + EPS) * subg_ref[...] * (1.0 - lam_init)).astype(BF16)


def _attention(q, k, v, lamv, subg, *, tq, q_start, n_q, kv_start, kv_len, tk, lam_init):
    bsz = q.shape[0]
    q_blk0 = q_start // tq
    kv_blk = kv_start // kv_len
    q_spec = pl.BlockSpec((None, tq, LANES), lambda b, h, i: (b, q_blk0 + i, h))
    o_spec = pl.BlockSpec((None, tq, LANES), lambda b, h, i: (b, i, h))
    k_spec = pl.BlockSpec((None, kv_len, LANES), lambda b, h, i: (b, kv_blk, h))
    v_spec = pl.BlockSpec((None, kv_len, 2 * DA_V_DIM), lambda b, h, i: (b, kv_blk, h))
    const = lambda shape: pl.BlockSpec(shape, lambda b, h, i: (0,) * len(shape))
    return pl.pallas_call(
        functools.partial(_attn_kernel, tq=tq, tk=tk, nk=kv_len // tk, lam_init=lam_init),
        grid=(bsz, DA_HEADS, n_q),
        in_specs=[const((SUBLANES, LANES)), const((1, LANES)), q_spec, k_spec, v_spec],
        out_specs=o_spec,
        out_shape=jax.ShapeDtypeStruct((bsz, n_q * tq, DA_WIDTH), BF16),
        scratch_shapes=[
            pltpu.VMEM((2 * tq, LANES), BF16),
            pltpu.VMEM((2 * tq, LANES), F32),
            pltpu.VMEM((2 * tq, 2 * DA_V_DIM), F32),
            pltpu.VMEM((2 * tq, tk), F32),
            pltpu.VMEM((2 * tq, tk), F32),
        ],
        compiler_params=_cparams(("parallel", "parallel", "arbitrary"), 48),
        name="diff_attn",
    )(lamv, subg, q, k, v)


PAIRS_PER_GROUP = EXPERTS_PER_GROUP * (EXPERTS_PER_GROUP - 1) // 2
N_BUCKETS = N_GROUPS * PAIRS_PER_GROUP
META_BUCKET, META_RANK, META_W_LO, META_W_HI = 0, 1, 2, 3


def _post_tail(y, first_step, x, mod_ref, g_ref, wr_ref, br_ref, x1_ref, hf_ref, meta_ref, rt_ref, cnt_ref,
               cnt_scr):
    x1 = x + mod_ref[2:3, :] * y
    x1_ref[...] = x1
    hf = _norm_mod(x1, g_ref[...], mod_ref[3:4, :], mod_ref[4:5, :])
    hf_ref[...] = hf

    hf_hi = hf.astype(BF16)
    hf_lo = (hf - hf_hi.astype(F32)).astype(BF16)
    logits = (jnp.dot(hf_hi, wr_ref[0], preferred_element_type=F32)
              + jnp.dot(hf_lo, wr_ref[0], preferred_element_type=F32)
              + jnp.dot(hf_hi, wr_ref[1], preferred_element_type=F32)) + br_ref[...]
    lane = lax.broadcasted_iota(jnp.int32, logits.shape, 1).astype(F32)
    big = float(LANES)
    is_g = (lane >= N_EXPERTS) & (lane < N_EXPERTS + N_GROUPS)
    gl = jnp.where(is_g, logits, NEG)
    gm = jnp.max(gl, axis=1, keepdims=True)
    g_idx = jnp.min(jnp.where(gl == gm, lane, big), axis=1, keepdims=True) - N_EXPERTS
    p_sel = 1.0 / jnp.sum(jnp.exp(gl - gm), axis=1, keepdims=True)
    lo = g_idx * EXPERTS_PER_GROUP
    el = jnp.where((lane >= lo) & (lane < lo + EXPERTS_PER_GROUP), logits, NEG)
    v1 = jnp.max(el, axis=1, keepdims=True)
    i1 = jnp.min(jnp.where(el == v1, lane, big), axis=1, keepdims=True)
    el2 = jnp.where(lane == i1, NEG, el)
    v2 = jnp.max(el2, axis=1, keepdims=True)
    i2 = jnp.min(jnp.where(el2 == v2, lane, big), axis=1, keepdims=True)
    t = jnp.exp(v2 - v1)
    w1 = p_sel / (1.0 + t)
    w2 = t * w1

    first_lower = i1 < i2
    a = jnp.minimum(i1, i2) - lo
    b = jnp.maximum(i1, i2) - lo
    code = a * EXPERTS_PER_GROUP + b
    pair = jnp.where(code == 1.0, 0.0, jnp.where(code == 6.0, 1.0, jnp.where(code == 2.0, 2.0, jnp.where(
        code == 3.0, 3.0, jnp.where(code == 7.0, 4.0, 5.0)))))
    bucket = g_idx * PAIRS_PER_GROUP + pair
    w_low = jnp.where(first_lower, w1, w2)
    w_high = jnp.where(first_lower, w2, w1)
    w_lo = jnp.where(code == 1.0, w_low, w_high)
    w_hi = jnp.where(code == 1.0, w_high, w_low)

    @pl.when(first_step)
    def _():
        cnt_scr[...] = jnp.zeros(cnt_scr.shape, F32)

    tm = logits.shape[0]
    onehot = jnp.where(lane == bucket, 1.0, 0.0)
    r_i = lax.broadcasted_iota(jnp.int32, (tm, tm), 0)
    c_i = lax.broadcasted_iota(jnp.int32, (tm, tm), 1)
    earlier = jnp.where(c_i < r_i, 1.0, 0.0).astype(BF16)
    prefix = jnp.dot(earlier, onehot.astype(BF16), preferred_element_type=F32)
    base = cnt_scr[0:1, :]
    rank = jnp.sum(onehot * (prefix + base), axis=1, keepdims=True)
    counts = jnp.broadcast_to(base + jnp.sum(onehot, axis=0, keepdims=True), cnt_scr.shape)
    cnt_scr[...] = counts
    cnt_ref[...] = counts
    meta = (jnp.where(lane == META_BUCKET, bucket, 0.0) + jnp.where(lane == META_RANK, rank, 0.0)
            + jnp.where(lane == META_W_LO, w_lo, 0.0) + jnp.where(lane == META_W_HI, w_hi, 0.0))
    meta_ref[...] = meta
    rt_ref[...] = jnp.transpose(meta)[0:SUBLANES, :]


def _post_even_kernel(cb_ref, p_ref, pprev_ref, pnext_ref, cw_ref, cbias_ref, olat_ref, octx_ref, w_ref,
                      xl_ref, xc_ref, mod_ref, g_ref, wr_ref, br_ref, x1_ref, hf_ref, meta_ref, rt_ref, cnt_ref, cnt_scr):
    i = pl.program_id(1)
    first_step = jnp.logical_and(pl.program_id(0) == 0, i == 0)
    pc = p_ref[...]
    row = lax.broadcasted_iota(jnp.int32, pc.shape, 0)
    has_prev = jnp.logical_and(i != 0, i != N_LAT_TILES)
    has_next = i < N_LAT_TILES - 1
    prev_row = jnp.where(has_prev, pprev_ref[SUBLANES - 1:SUBLANES, :], 0.0)
    next_row = jnp.where(has_next, pnext_ref[0:1, :], 0.0)
    up = jnp.where(row == 0, prev_row, pltpu.roll(pc, 1, axis=0))
    dn = jnp.where(row == TM - 1, next_row, pltpu.roll(pc, TM - 1, axis=0))
    conv = cbias_ref[...] + cw_ref[0:1, :] * up + cw_ref[1:2, :] * pc + cw_ref[2:3, :] * dn
    out_a = (cb_ref[...] * conv).astype(BF16)
    o = jnp.where(i == N_LAT_TILES, octx_ref[...], olat_ref[...])
    x = jnp.where(i == N_LAT_TILES, xc_ref[...], xl_ref[...])
    y = (jnp.dot(out_a, w_ref[0:CONV_WIDTH, :], preferred_element_type=F32)
         + jnp.dot(o, w_ref[CONV_WIDTH:, :], preferred_element_type=F32))
    _post_tail(y, first_step, x, mod_ref, g_ref, wr_ref, br_ref, x1_ref, hf_ref, meta_ref, rt_ref, cnt_ref,
               cnt_scr)


def _post_odd_kernel(y_ref, hf_in_ref, hb_in_ref, w_ref, x_ref, mod_ref, g_ref, wr_ref, br_ref,
                     x1_ref, hf_ref, meta_ref, rt_ref, cnt_ref, cnt_scr):
    first_step = jnp.logical_and(pl.program_id(0) == 0, pl.program_id(1) == 0)
    a = (y_ref[...] * (hf_in_ref[...] + hb_in_ref[...])).astype(BF16)
    y = jnp.dot(a, w_ref[...], preferred_element_type=F32)
    _post_tail(y, first_step, x_ref[...], mod_ref, g_ref, wr_ref, br_ref, x1_ref, hf_ref, meta_ref, rt_ref, cnt_ref,
               cnt_scr)


def _post_specs(bsz, rows):
    tok_spec = lambda width: pl.BlockSpec((None, TM, width), lambda b, i: (b, i, 0))
    const = lambda shape: pl.BlockSpec(shape, lambda b, i: (0,) * len(shape))
    tail_in = [pl.BlockSpec((None, SUBLANES, D_MODEL), _mod_index), const((1, D_MODEL)),
               const((2, D_MODEL, LANES)), const((1, LANES))]
    tiles = rows // TM
    out_specs = [tok_spec(D_MODEL), tok_spec(D_MODEL), tok_spec(LANES),
                 pl.BlockSpec((None, SUBLANES, TM), lambda b, i: (b * tiles + i, 0, 0)), const((SUBLANES, LANES))]
    out_shape = [jax.ShapeDtypeStruct((bsz, rows, D_MODEL), F32),
                 jax.ShapeDtypeStruct((bsz, rows, D_MODEL), F32),
                 jax.ShapeDtypeStruct((bsz, rows, LANES), F32),
                 jax.ShapeDtypeStruct((bsz * tiles, SUBLANES, TM), F32),
                 jax.ShapeDtypeStruct((SUBLANES, LANES), F32)]
    scratch = [pltpu.VMEM((SUBLANES, LANES), F32)]
    return tok_spec, const, tail_in, out_specs, out_shape, scratch


def _post_even(cb, p, conv_w, conv_b, o_lat, o_ctx, w_out, x, ctx, mod, g, wr, br):
    bsz = x.shape[0]
    tok_spec, const, tail_in, out_specs, out_shape, scratch = _post_specs(bsz, TOK)
    halo_blocks = TM // SUBLANES
    last_halo = TOK // SUBLANES - 1
    prev_spec = pl.BlockSpec((None, SUBLANES, CONV_WIDTH),
                             lambda b, i: (b, jnp.maximum(i * halo_blocks - 1, 0), 0))
    next_spec = pl.BlockSpec((None, SUBLANES, CONV_WIDTH),
                             lambda b, i: (b, jnp.minimum((i + 1) * halo_blocks, last_halo), 0))
    return pl.pallas_call(
        _post_even_kernel,
        grid=(bsz, N_TILES),
        in_specs=[tok_spec(CONV_WIDTH), tok_spec(CONV_WIDTH), prev_spec, next_spec,
                  const((SUBLANES, CONV_WIDTH)), const((1, CONV_WIDTH))] + _lat_ctx_specs(DA_WIDTH)
        + [const((D_MODEL, D_MODEL))] + _lat_ctx_specs(D_MODEL) + tail_in,
        out_specs=out_specs,
        out_shape=out_shape,
        scratch_shapes=scratch,
        compiler_params=_cparams(("arbitrary", "arbitrary"), 48),
        name="post_even",
    )(cb, p, p, p, conv_w, conv_b, o_lat, o_ctx, w_out, x, ctx, mod, g, wr, br)


def _post_odd(y, hfw, hbw, w_out, xs, mod, g, wr, br):
    bsz = xs.shape[0]
    tok_spec, const, tail_in, out_specs, out_shape, scratch = _post_specs(bsz, SEQ)
    return pl.pallas_call(
        _post_odd_kernel,
        grid=(bsz, N_LAT_TILES),
        in_specs=[tok_spec(LRU_WIDTH), tok_spec(LRU_WIDTH), tok_spec(LRU_WIDTH),
                  const((LRU_WIDTH, D_MODEL)), tok_spec(D_MODEL)] + tail_in,
        out_specs=out_specs,
        out_shape=out_shape,
        scratch_shapes=scratch,
        compiler_params=_cparams(("arbitrary", "arbitrary"), 48),
        name="post_odd",
    )(y, hfw, hbw, w_out, xs, mod, g, wr, br)


_PAIR_SLOTS = [(0, 1), (2, 1), (2, 0), (3, 0), (3, 1), (3, 2)]
_BUCKET_LO = [g * EXPERTS_PER_GROUP + a for g in range(N_GROUPS) for a, _ in _PAIR_SLOTS]
_BUCKET_HI = [g * EXPERTS_PER_GROUP + b for g in range(N_GROUPS) for _, b in _PAIR_SLOTS]


def _sorted_tiles(n_tokens):
    return n_tokens // TM + N_BUCKETS


def _route_plan(rt, counts, n_tokens):
    n_tiles = _sorted_tiles(n_tokens)
    bucket = rt[:, META_BUCKET, :].astype(jnp.int32).reshape(n_tokens)
    rank = rt[:, META_RANK, :].astype(jnp.int32).reshape(n_tokens)
    cnt = counts[0, :N_BUCKETS].astype(jnp.int32)
    tiles_per = (cnt + TM - 1) // TM
    tile_end = jnp.cumsum(tiles_per)
    row_start = (tile_end - tiles_per) * TM
    dest = (row_start[bucket] + rank).reshape(n_tokens // TM, 1, TM)
    tile_bucket = jnp.minimum(jnp.sum(jnp.arange(n_tiles)[:, None] >= tile_end[None, :], axis=1), N_BUCKETS - 1)
    e_lo = jnp.asarray(_BUCKET_LO, jnp.int32)[tile_bucket]
    e_hi = jnp.asarray(_BUCKET_HI, jnp.int32)[tile_bucket]
    return dest, e_lo, e_hi, tile_end[-1:].astype(jnp.int32)


def _invert_kernel(dest_ref, src_ref):
    i = pl.program_id(0)

    @pl.when(i == 0)
    def _():
        def clear(k, carry):
            src_ref[k] = 0
            return carry
        lax.fori_loop(0, src_ref.shape[0], clear, 0, unroll=8)

    base = i * TM

    def put(r, carry):
        src_ref[dest_ref[0, r]] = base + r
        return carry

    lax.fori_loop(0, TM, put, 0, unroll=8)


def _invert(dest, n_tokens):
    rows = _sorted_tiles(n_tokens) * TM
    return pl.pallas_call(
        _invert_kernel,
        grid=(n_tokens // TM,),
        in_specs=[pl.BlockSpec((None, 1, TM), lambda i: (i, 0, 0), memory_space=pltpu.SMEM)],
        out_specs=pl.BlockSpec(memory_space=pltpu.SMEM),
        out_shape=jax.ShapeDtypeStruct((rows,), jnp.int32),
        compiler_params=_cparams(("arbitrary",), 16),
        name="moe_invert",
    )(dest)


def _gather_rows(idx_ref, src_hbm, dst_ref, sem):
    def issue(r, carry):
        pltpu.make_async_copy(src_hbm.at[pl.ds(idx_ref[0, r], 1), :], dst_ref.at[pl.ds(r, 1), :], sem).start()
        return carry
    lax.fori_loop(0, TM, issue, 0, unroll=8)


def _gather_wait(src_hbm, dst_ref, sem):
    pltpu.make_async_copy(src_hbm.at[pl.ds(0, TM), :], dst_ref, sem).wait()


def _moe_routed_kernel(elo_ref, ehi_ref, nused_ref, src_ref, src_next_ref, hf_hbm,
                       wg_lo, wu_lo, wd_lo, wg_hi, wu_hi, wd_hi, y_ref, hbuf_ref, sems):
    del elo_ref, ehi_ref
    j = pl.program_id(0)
    n_used = nused_ref[0]
    slot = j % 2

    @pl.when(j == 0)
    def _():
        _gather_rows(src_ref, hf_hbm, hbuf_ref.at[0], sems.at[0])

    @pl.when(j < n_used)
    def _():
        _gather_wait(hf_hbm, hbuf_ref.at[slot], sems.at[slot])
        for r in range(TM):
            pltpu.make_async_copy(hf_hbm.at[pl.ds(src_next_ref[0, r], 1), :],
                                  hbuf_ref.at[1 - slot, pl.ds(r, 1), :], sems.at[1 - slot]).start()
        h = hbuf_ref[slot].astype(BF16)

        def expert(wg_ref, wu_ref, wd_ref):
            hg = jnp.dot(h, wg_ref[...].astype(BF16), preferred_element_type=F32)
            hu = jnp.dot(h, wu_ref[...].astype(BF16), preferred_element_type=F32)
            act = (hg * _sigmoid(hg)) * hu
            return jnp.dot(act.astype(BF16), wd_ref[...].astype(BF16), preferred_element_type=F32)

        y_ref[:, :D_MODEL] = expert(wg_lo, wu_lo, wd_lo)
        y_ref[:, D_MODEL:] = expert(wg_hi, wu_hi, wd_hi)

    @pl.when(j == n_used)
    def _():
        _gather_wait(hf_hbm, hbuf_ref.at[slot], sems.at[slot])

    @pl.when(j >= n_used)
    def _():
        y_ref[...] = jnp.zeros(y_ref.shape, F32)


def _moe_routed(src, hf, e_lo, e_hi, n_used, wg, wu, wd, layer):
    n_tiles = src.shape[0]
    up_spec = lambda tbl: pl.BlockSpec((None, None, D_MODEL, D_EXPERT),
                                       lambda j, lo, hi, nu: (layer, (lo, hi)[tbl][j], 0, 0))
    dn_spec = lambda tbl: pl.BlockSpec((None, None, D_EXPERT, D_MODEL),
                                       lambda j, lo, hi, nu: (layer, (lo, hi)[tbl][j], 0, 0))
    grid_spec = pltpu.PrefetchScalarGridSpec(
        num_scalar_prefetch=3,
        grid=(n_tiles,),
        in_specs=[pl.BlockSpec((None, 1, TM), lambda j, lo, hi, nu: (j, 0, 0), memory_space=pltpu.SMEM),
                  pl.BlockSpec((None, 1, TM), lambda j, lo, hi, nu: (jnp.minimum(j + 1, n_tiles - 1), 0, 0),
                               memory_space=pltpu.SMEM),
                  pl.BlockSpec(memory_space=pl.ANY),
                  up_spec(0), up_spec(0), dn_spec(0), up_spec(1), up_spec(1), dn_spec(1)],
        out_specs=pl.BlockSpec((TM, 2 * D_MODEL), lambda j, lo, hi, nu: (j, 0)),
        scratch_shapes=[pltpu.VMEM((2, TM, D_MODEL), F32), pltpu.SemaphoreType.DMA((2,))],
    )
    return pl.pallas_call(
        _moe_routed_kernel,
        grid_spec=grid_spec,
        out_shape=jax.ShapeDtypeStruct((n_tiles * TM, 2 * D_MODEL), F32),
        compiler_params=_cparams(("arbitrary",), 56),
        name="moe_routed",
    )(e_lo, e_hi, n_used, src, src, hf, wg, wu, wd, wg, wu, wd)


def _combine_kernel(dest_ref, dest_next_ref, y_hbm, x_ref, meta_ref, mod_ref, o_ref, buf_ref, sems):
    step = pl.program_id(0) * pl.num_programs(1) + pl.program_id(1)
    n_steps = pl.num_programs(0) * pl.num_programs(1)
    slot = step % 2

    @pl.when(step == 0)
    def _():
        _gather_rows(dest_ref, y_hbm, buf_ref.at[0], sems.at[0])

    @pl.when(step + 1 < n_steps)
    def _():
        _gather_rows(dest_next_ref, y_hbm, buf_ref.at[1 - slot], sems.at[1 - slot])

    _gather_wait(y_hbm, buf_ref.at[slot], sems.at[slot])
    meta = meta_ref[...]
    moe = (meta[:, META_W_LO:META_W_LO + 1] * buf_ref[slot, :, :D_MODEL]
           + meta[:, META_W_HI:META_W_HI + 1] * buf_ref[slot, :, D_MODEL:])
    o_ref[...] = x_ref[...] + mod_ref[5:6, :] * moe


def _combine(dest, y, x1, meta, mod, *, tiles_per_batch):
    bsz = x1.shape[0]
    n_steps = bsz * tiles_per_batch
    tok_spec = lambda width: pl.BlockSpec((None, TM, width), lambda b, i: (b, i, 0))
    idx_spec = lambda ahead: pl.BlockSpec(
        (None, 1, TM), lambda b, i: (jnp.minimum(b * tiles_per_batch + i + ahead, n_steps - 1), 0, 0),
        memory_space=pltpu.SMEM)
    return pl.pallas_call(
        _combine_kernel,
        grid=(bsz, tiles_per_batch),
        in_specs=[idx_spec(0), idx_spec(1), pl.BlockSpec(memory_space=pl.ANY), tok_spec(D_MODEL),
                  tok_spec(LANES), pl.BlockSpec((None, SUBLANES, D_MODEL), _mod_index)],
        out_specs=tok_spec(D_MODEL),
        out_shape=jax.ShapeDtypeStruct(x1.shape, F32),
        scratch_shapes=[pltpu.VMEM((2, TM, 2 * D_MODEL), F32), pltpu.SemaphoreType.DMA((2,))],
        compiler_params=_cparams(("arbitrary", "arbitrary"), 32),
        name="moe_combine",
    )(dest, dest, y, x1, meta, mod)


def _moe(hf, meta, rt, counts, x1, mod, wg, wu, wd, layer, *, tiles_per_batch):
    n_tokens = hf.shape[0] * hf.shape[1]
    dest, e_lo, e_hi, n_used = _route_plan(rt, counts, n_tokens)
    src = _invert(dest, n_tokens).reshape(_sorted_tiles(n_tokens), 1, TM)
    y = _moe_routed(src, hf.reshape(n_tokens, D_MODEL), e_lo, e_hi, n_used, wg, wu, wd, layer)
    return _combine(dest, y, x1, meta, mod, tiles_per_batch=tiles_per_batch)


def _inproj_odd_kernel(x_ref, mod_ref, g_ref, w_ref, y_ref, u_ref):
    h = _norm_mod(x_ref[...], g_ref[...], mod_ref[0:1, :], mod_ref[1:2, :])
    z = jnp.dot(h.astype(BF16), w_ref[...], preferred_element_type=F32)
    zy = z[:, :LRU_WIDTH]
    c0 = math.sqrt(2.0 / math.pi)
    y_ref[...] = 0.5 * zy * (1.0 + jnp.tanh(c0 * (zy + 0.044715 * (zy * zy * zy))))
    u_ref[...] = z[:, LRU_WIDTH:]


def _inproj_odd(xs, mod, g, w_in):
    bsz = xs.shape[0]
    tok_spec = lambda width: pl.BlockSpec((None, TM, width), lambda b, i: (b, i, 0))
    const = lambda shape: pl.BlockSpec(shape, lambda b, i: (0,) * len(shape))
    return pl.pallas_call(
        _inproj_odd_kernel,
        grid=(bsz, N_TILES),
        in_specs=[tok_spec(D_MODEL), pl.BlockSpec((None, SUBLANES, D_MODEL), _mod_index),
                  const((1, D_MODEL)), const((D_MODEL, 2 * LRU_WIDTH))],
        out_specs=[tok_spec(LRU_WIDTH), tok_spec(LRU_WIDTH)],
        out_shape=[jax.ShapeDtypeStruct((bsz, TOK, LRU_WIDTH), F32),
                   jax.ShapeDtypeStruct((bsz, TOK, LRU_WIDTH), F32)],
        compiler_params=_cparams(("parallel", "arbitrary"), 48),
        name="inproj_odd",
    )(xs, mod, g, w_in)


def _scan_kernel(u_ref, cw_ref, cbias_ref, wa_ref, ba_ref, wx_ref, bx_ref, lam_ref, h_ref,
                 halo_ref, carry_ref, a_ref, b_ref, *, reverse):
    i = pl.program_id(1)
    tt, w = u_ref.shape
    n_groups = tt // SUBLANES

    @pl.when(i <= 1)
    def _():
        halo_ref[...] = jnp.zeros(halo_ref.shape, F32)

    @pl.when(i == 0)
    def _():
        carry_ref[...] = jnp.zeros(carry_ref.shape, F32)

    u = u_ref[...]
    halo = halo_ref[...]
    row8 = lax.broadcasted_iota(jnp.int32, (SUBLANES, w), 0)
    k_self = 0 if reverse else LRU_CONV_K - 1
    uc = cbias_ref[...] + cw_ref[k_self:k_self + 1, :] * u
    for k in range(1, LRU_CONV_K):
        if reverse:
            tmp = pltpu.roll(u, tt - k, axis=0)
            hr = pltpu.roll(halo, SUBLANES - k, axis=0)
            edge = jnp.where(row8 >= SUBLANES - k, hr, tmp[tt - SUBLANES:, :])
            shifted = jnp.concatenate([tmp[:tt - SUBLANES, :], edge], axis=0)
            wk = cw_ref[k:k + 1, :]
        else:
            tmp = pltpu.roll(u, k, axis=0)
            hr = pltpu.roll(halo, k, axis=0)
            edge = jnp.where(row8 < k, hr, tmp[:SUBLANES, :])
            shifted = jnp.concatenate([edge, tmp[SUBLANES:, :]], axis=0)
            wk = cw_ref[LRU_CONV_K - 1 - k:LRU_CONV_K - k, :]
        uc = uc + wk * shifted
    halo_ref[...] = u[:SUBLANES, :] if reverse else u[tt - SUBLANES:, :]

    ucb = uc.astype(BF16)

    def block_diag(w_blocks):
        return jnp.concatenate(
            [jnp.dot(ucb[:, j * LRU_BLOCK:(j + 1) * LRU_BLOCK], w_blocks[j], preferred_element_type=F32)
             for j in range(LRU_BLOCKS)], axis=1)

    r = _sigmoid(block_diag(wa_ref) + ba_ref[...])
    gate_i = _sigmoid(block_diag(wx_ref) + bx_ref[...])
    neg_lam = -lam_ref[...]
    softplus = jnp.maximum(neg_lam, 0.0) + jnp.log1p(jnp.exp(-jnp.abs(neg_lam)))
    a = jnp.exp((-LRU_C * softplus) * r)
    a_ref[...] = a
    one_m_a2 = 1.0 - a * a
    root = jnp.where(one_m_a2 > 0.0, one_m_a2 * lax.rsqrt(one_m_a2), 0.0)
    b_ref[...] = root * (gate_i * uc)

    def group(gi, carry):
        g = n_groups - 1 - gi if reverse else gi
        off = pl.multiple_of(g * SUBLANES, SUBLANES)
        av = a_ref[pl.ds(off, SUBLANES), :]
        bv = b_ref[pl.ds(off, SUBLANES), :]
        for s in (1, 2, 4):
            if reverse:
                outside = row8 >= SUBLANES - s
                shift = SUBLANES - s
            else:
                outside = row8 < s
                shift = s
            a_sh = jnp.where(outside, 1.0, pltpu.roll(av, shift, axis=0))
            b_sh = jnp.where(outside, 0.0, pltpu.roll(bv, shift, axis=0))
            bv = av * b_sh + bv
            av = av * a_sh
        h = av * carry + bv
        h_ref[pl.ds(off, SUBLANES), :] = h
        last = h[0:1, :] if reverse else h[SUBLANES - 1:SUBLANES, :]
        return jnp.broadcast_to(last, (SUBLANES, w))

    carry_ref[...] = lax.fori_loop(0, n_groups, group, carry_ref[...])


def _rglru_scan(u, conv_w, conv_b, w_a, b_a, w_x, b_x, lam, *, reverse):
    bsz = u.shape[0]
    if reverse:
        tile_of = lambda i: N_LAT_TILES - i
    else:
        tile_of = lambda i: (i + N_LAT_TILES) % N_TILES
    tok_spec = pl.BlockSpec((None, TM, LRU_WIDTH), lambda b, i: (b, tile_of(i), 0))
    const = lambda shape: pl.BlockSpec(shape, lambda b, i: (0,) * len(shape))
    return pl.pallas_call(
        functools.partial(_scan_kernel, reverse=reverse),
        grid=(bsz, N_TILES),
        in_specs=[tok_spec, const((SUBLANES, LRU_WIDTH)), const((1, LRU_WIDTH)),
                  const((LRU_BLOCKS, LRU_BLOCK, LRU_BLOCK)), const((1, LRU_WIDTH)),
                  const((LRU_BLOCKS, LRU_BLOCK, LRU_BLOCK)), const((1, LRU_WIDTH)), const((1, LRU_WIDTH))],
        out_specs=tok_spec,
        out_shape=jax.ShapeDtypeStruct((bsz, TOK, LRU_WIDTH), F32),
        scratch_shapes=[pltpu.VMEM((SUBLANES, LRU_WIDTH), F32), pltpu.VMEM((SUBLANES, LRU_WIDTH), F32),
                        pltpu.VMEM((TM, LRU_WIDTH), F32), pltpu.VMEM((TM, LRU_WIDTH), F32)],
        compiler_params=_cparams(("parallel", "arbitrary"), 32),
        name="rglru_rev" if reverse else "rglru_fwd",
    )(u, conv_w, conv_b, w_a, b_a, w_x, b_x, lam)


def _pad_rows(a, rows):
    return jnp.pad(a, ((0, rows - a.shape[0]), (0, 0)))


def _rope_tables():
    t = jnp.arange(SEQ)
    n_freq = DA_HEAD_DIM // 4
    inv = ROPE_BASE ** (-jnp.arange(n_freq, dtype=F32) / n_freq)
    ang = jnp.concatenate([(t // GRID_W).astype(F32)[:, None] * inv,
                           (t % GRID_W).astype(F32)[:, None] * inv], axis=-1)
    cos, sin = jnp.cos(ang), jnp.sin(ang)
    cos64 = jnp.concatenate([cos, cos], axis=-1)
    sin64 = jnp.concatenate([-sin, sin], axis=-1)
    cos_t = jnp.concatenate([jnp.tile(cos64, (1, 2)), jnp.ones((CTX_LEN, LANES), F32)], axis=0)
    sin_t = jnp.concatenate([jnp.tile(sin64, (1, 2)), jnp.zeros((CTX_LEN, LANES), F32)], axis=0)
    return cos_t, sin_t


def _router_params(w_grp, b_grp, w_rt, b_rt):
    wr = jnp.concatenate([w_rt.reshape(D_MODEL, N_EXPERTS), w_grp], axis=1)
    br = jnp.concatenate([b_rt.reshape(N_EXPERTS), b_grp])
    pad = LANES - wr.shape[1]
    wr = jnp.pad(wr, ((0, 0), (0, pad)))
    wr_hi = wr.astype(BF16)
    wr_lo = (wr - wr_hi.astype(F32)).astype(BF16)
    return jnp.stack([wr_hi, wr_lo]), jnp.pad(br, (0, pad)).reshape(1, LANES)


def kernel(x, c, ctx, c_ctx, ada_w, ada_b, norm_mix, norm_ffn, ev_w_in, ev_conv_w, ev_conv_b, ev_q_norm, ev_k_norm, ev_lam_q1, ev_lam_k1, ev_lam_q2, ev_lam_k2, ev_sub_norm, ev_w_out, od_w_in, od_conv_w, od_conv_b, od_w_a, od_b_a, od_w_x, od_b_x, od_lam, od_w_out, moe_w_grp, moe_b_grp, moe_w_rt, moe_b_rt, moe_w_gate, moe_w_up, moe_w_down):
    bsz = x.shape[0]
    assert x.shape == (bsz, SEQ, D_MODEL) and ctx.shape == (bsz, CTX_LEN, D_MODEL) and bsz == 2
    depth = ada_w.shape[0]
    assert depth == 2


    cvec = _pad_rows(jnp.stack([c[0], c_ctx, c[1], c_ctx]), SUBLANES)
    mod_all = _ada_mod(cvec, ada_w, ada_b)
    mod_all = mod_all[:, :2 * bsz].reshape(depth, 2 * bsz, 6, D_MODEL)
    mod_all = jnp.pad(mod_all, ((0, 0), (0, 0), (0, SUBLANES - 6), (0, 0)))

    l = 0
    lam_init = 0.8 - 0.6 * math.exp(-0.3 * l)
    mod = mod_all[l]
    cos_t, sin_t = _rope_tables()
    blk = jnp.arange(DA_WIDTH) // DA_HEAD_DIM
    bd = jnp.where(blk[:, None] == blk[None, :], 1.0 / DA_HEAD_DIM, 0.0).astype(BF16)
    n_rep = DA_WIDTH // DA_HEAD_DIM
    cb, p, q, k, v = _inproj_even(
        x, ctx, mod, norm_mix[l].reshape(1, D_MODEL), ev_w_in[0].astype(BF16),
        jnp.tile(ev_q_norm[0], n_rep).reshape(1, DA_WIDTH), jnp.tile(ev_k_norm[0], n_rep).reshape(1, DA_WIDTH),
        cos_t, sin_t, bd)

    lamv = _pad_rows(jnp.pad(jnp.stack([ev_lam_q1[0], ev_lam_k1[0], ev_lam_q2[0], ev_lam_k2[0]]),
                             ((0, 0), (0, LANES - DA_HEAD_DIM))), SUBLANES)
    subg = ev_sub_norm[0].reshape(1, DA_V_DIM)
    o_lat = _attention(q, k, v, lamv, subg, tq=256, q_start=0, n_q=SEQ // 256,
                       kv_start=0, kv_len=TOK, tk=768, lam_init=lam_init)
    o_ctx = _attention(q, k, v, lamv, subg, tq=TM, q_start=SEQ, n_q=CTX_LEN // TM,
                       kv_start=SEQ, kv_len=CTX_LEN, tk=CTX_LEN, lam_init=lam_init)

    wr, br = _router_params(moe_w_grp[l], moe_b_grp[l], moe_w_rt[l], moe_b_rt[l])
    x1, hf, meta, rt, counts = _post_even(
        cb, p, _pad_rows(ev_conv_w[0], SUBLANES), ev_conv_b[0].reshape(1, CONV_WIDTH), o_lat, o_ctx,
        ev_w_out[0].astype(BF16), x, ctx, mod, norm_ffn[l].reshape(1, D_MODEL), wr, br)
    xs = _moe(hf, meta, rt, counts, x1, mod, moe_w_gate, moe_w_up, moe_w_down, l, tiles_per_batch=N_TILES)

    l = 1
    mod = mod_all[l]
    y, u = _inproj_odd(xs, mod, norm_mix[l].reshape(1, D_MODEL), od_w_in[0].astype(BF16))
    h_dirs = []
    for d in range(2):
        h_dirs.append(_rglru_scan(
            u, _pad_rows(od_conv_w[0, d], SUBLANES), od_conv_b[0, d].reshape(1, LRU_WIDTH),
            od_w_a[0, d].astype(BF16), od_b_a[0, d].reshape(1, LRU_WIDTH),
            od_w_x[0, d].astype(BF16), od_b_x[0, d].reshape(1, LRU_WIDTH),
            od_lam[0, d].reshape(1, LRU_WIDTH), reverse=bool(d)))
    wr, br = _router_params(moe_w_grp[l], moe_b_grp[l], moe_w_rt[l], moe_b_rt[l])
    x1, hf, meta, rt, counts = _post_odd(y, h_dirs[0], h_dirs[1], od_w_out[0].astype(BF16), xs, mod,
                                     norm_ffn[l].reshape(1, D_MODEL), wr, br)
    return _moe(hf, meta, rt, counts, x1, mod, moe_w_gate, moe_w_up, moe_w_down, l,
                tiles_per_batch=N_LAT_TILES)
```

```python
import functools
import math

import jax
import jax.numpy as jnp
from jax import lax
from jax.experimental import pallas as pl
from jax.experimental.pallas import tpu as pltpu

F32 = jnp.float32
BF16 = jnp.bfloat16
HIGHEST = lax.Precision.HIGHEST

D_MODEL = 1024
SEQ = 8192
CTX_LEN = 256
TOK = SEQ + CTX_LEN
GRID_W = 64
EPS = 1e-6

CONV_WIDTH = 512
DA_HEADS = 4
DA_HEAD_DIM = 64
DA_V_DIM = 2 * DA_HEAD_DIM
DA_WIDTH = DA_HEADS * DA_V_DIM
EVEN_IN = 3 * CONV_WIDTH + 3 * DA_WIDTH
ROPE_BASE = 10000.0

LRU_WIDTH = 1024
LRU_BLOCKS = 8
LRU_BLOCK = LRU_WIDTH // LRU_BLOCKS
LRU_CONV_K = 4
LRU_C = 8.0

N_GROUPS = 4
EXPERTS_PER_GROUP = 4
N_EXPERTS = N_GROUPS * EXPERTS_PER_GROUP
D_EXPERT = 512

LANES = 128
SUBLANES = 8
TM = 256
N_LAT_TILES = SEQ // TM
N_TILES = TOK // TM
NEG = -1e30
MIB = 2 ** 20


def _cparams(semantics, vmem_mib):
    return pltpu.CompilerParams(dimension_semantics=semantics, vmem_limit_bytes=vmem_mib * MIB)


def _sigmoid(x):
    return 0.5 * jnp.tanh(0.5 * x) + 0.5


def _norm_mod(x, g, shift, scale):
    ms = jnp.mean(x * x, axis=-1, keepdims=True)
    return (x * lax.rsqrt(ms + EPS) * g) * (1.0 + scale) + shift


ADA_TN = 1536


def _ada_kernel(c_ref, w_ref, b_ref, o_ref):
    c = c_ref[...]
    a = c * _sigmoid(c)
    o_ref[...] = jnp.dot(a, w_ref[...], precision=HIGHEST, preferred_element_type=F32) + b_ref[...]


def _ada_mod(cvec, ada_w, ada_b):
    depth, d, n = ada_w.shape
    return pl.pallas_call(
        _ada_kernel,
        grid=(depth, n // ADA_TN),
        in_specs=[
            pl.BlockSpec((SUBLANES, d), lambda l, j: (0, 0)),
            pl.BlockSpec((None, d, ADA_TN), lambda l, j: (l, 0, j)),
            pl.BlockSpec((None, 1, ADA_TN), lambda l, j: (l, 0, j)),
        ],
        out_specs=pl.BlockSpec((None, SUBLANES, ADA_TN), lambda l, j: (l, 0, j)),
        out_shape=jax.ShapeDtypeStruct((depth, SUBLANES, n), F32),
        compiler_params=_cparams(("arbitrary", "arbitrary"), 40),
        name="ada_mod",
    )(cvec, ada_w, ada_b.reshape(depth, 1, n))


def _mod_index(b, i):
    return (b * 2 + i // N_LAT_TILES, 0, 0)


def _inproj_even_kernel(xl_ref, xc_ref, mod_ref, g_ref, w_ref, qg_ref, kg_ref, cos_ref, sin_ref, bd_ref,
                        cb_ref, p_ref, q_ref, k_ref, v_ref):
    x = jnp.where(pl.program_id(1) == N_LAT_TILES, xc_ref[...], xl_ref[...])
    h = _norm_mod(x, g_ref[...], mod_ref[0:1, :], mod_ref[1:2, :])
    z = jnp.dot(h.astype(BF16), w_ref[...], preferred_element_type=F32)
    cw = CONV_WIDTH
    cb_ref[...] = z[:, :cw]
    p_ref[...] = z[:, cw:2 * cw] * z[:, 2 * cw:3 * cw]

    reps = DA_WIDTH // LANES
    cosf = jnp.concatenate([cos_ref[...]] * reps, axis=1)
    sinf = jnp.concatenate([sin_ref[...]] * reps, axis=1)
    lane = lax.broadcasted_iota(jnp.int32, (TM, DA_WIDTH), 1)
    first_half = (lane & (DA_HEAD_DIM - 1)) < DA_HEAD_DIM // 2
    bd = bd_ref[...]

    def head_norm_rope(t, gain):
        ms = jnp.dot((t * t).astype(BF16), bd, preferred_element_type=F32)
        y = t * lax.rsqrt(ms + EPS) * gain
        fwd = pltpu.roll(y, DA_WIDTH - DA_HEAD_DIM // 2, axis=1)
        bwd = pltpu.roll(y, DA_HEAD_DIM // 2, axis=1)
        return y * cosf + jnp.where(first_half, fwd, bwd) * sinf

    base = 3 * cw
    q = head_norm_rope(z[:, base:base + DA_WIDTH], qg_ref[...])
    q_ref[...] = (q * (DA_HEAD_DIM ** -0.5 * math.log2(math.e))).astype(BF16)
    k = head_norm_rope(z[:, base + DA_WIDTH:base + 2 * DA_WIDTH], kg_ref[...])
    k_ref[...] = k.astype(BF16)
    v = z[:, base + 2 * DA_WIDTH:].astype(BF16)
    ones = jnp.ones((TM, DA_V_DIM), BF16)
    v_ref[...] = jnp.concatenate(
        [blk for h in range(DA_HEADS) for blk in (v[:, h * DA_V_DIM:(h + 1) * DA_V_DIM], ones)], axis=1)


def _lat_ctx_specs(width):
    return [pl.BlockSpec((None, TM, width), lambda b, i: (b, jnp.minimum(i, N_LAT_TILES - 1), 0)),
            pl.BlockSpec((None, TM, width), lambda b, i: (b, 0, 0))]


def _inproj_even(x, ctx, mod, g, w_in, qg, kg, cos, sin, bd):
    bsz = x.shape[0]
    tok_spec = lambda width: pl.BlockSpec((None, TM, width), lambda b, i: (b, i, 0))
    const = lambda shape: pl.BlockSpec(shape, lambda b, i: (0,) * len(shape))
    return pl.pallas_call(
        _inproj_even_kernel,
        grid=(bsz, N_TILES),
        in_specs=_lat_ctx_specs(D_MODEL) + [
            pl.BlockSpec((None, SUBLANES, D_MODEL), _mod_index),
            const((1, D_MODEL)),
            const((D_MODEL, EVEN_IN)),
            const((1, DA_WIDTH)),
            const((1, DA_WIDTH)),
            pl.BlockSpec((TM, LANES), lambda b, i: (i, 0)),
            pl.BlockSpec((TM, LANES), lambda b, i: (i, 0)),
            const((DA_WIDTH, DA_WIDTH)),
        ],
        out_specs=[tok_spec(CONV_WIDTH), tok_spec(CONV_WIDTH), tok_spec(DA_WIDTH), tok_spec(DA_WIDTH),
                   tok_spec(2 * DA_WIDTH)],
        out_shape=[
            jax.ShapeDtypeStruct((bsz, TOK, CONV_WIDTH), F32),
            jax.ShapeDtypeStruct((bsz, TOK, CONV_WIDTH), F32),
            jax.ShapeDtypeStruct((bsz, TOK, DA_WIDTH), BF16),
            jax.ShapeDtypeStruct((bsz, TOK, DA_WIDTH), BF16),
            jax.ShapeDtypeStruct((bsz, TOK, 2 * DA_WIDTH), BF16),
        ],
        compiler_params=_cparams(("parallel", "arbitrary"), 48),
        name="inproj_even",
    )(x, ctx, mod, g, w_in, qg, kg, cos, sin, bd)


def _attn_kernel(lam_ref, subg_ref, q_ref, k_ref, v_ref, o_ref, qz_ref, m_ref, acc_ref,
                 sa_ref, sb_ref, *, tq, tk, nk, lam_init):
    q = q_ref[...]
    lane = lax.broadcasted_iota(jnp.int32, (tq, LANES), 1)
    zero = jnp.zeros_like(q)
    qz_ref[0:tq, :] = jnp.where(lane < DA_HEAD_DIM, q, zero)
    qz_ref[tq:2 * tq, :] = jnp.where(lane >= DA_HEAD_DIM, q, zero)
    m_ref[...] = jnp.full(m_ref.shape, NEG, F32)
    acc_ref[...] = jnp.zeros(acc_ref.shape, F32)

    def scores(j, dst_ref):
        off = pl.multiple_of(j * tk, tk)
        dst_ref[...] = lax.dot_general(qz_ref[...], k_ref[pl.ds(off, tk), :], (((1,), (1,)), ((), ())),
                                       preferred_element_type=F32)

    def update(j, src_ref):
        off = pl.multiple_of(j * tk, tk)
        s = src_ref[...]
        m_prev = m_ref[...]
        m_new = jnp.maximum(m_prev, jnp.max(s, axis=1, keepdims=True))
        alpha = jnp.exp2(m_prev - m_new)
        p = jnp.exp2((s - m_new[:, :1]).astype(BF16))
        pv = jnp.dot(p, v_ref[pl.ds(off, tk), :], preferred_element_type=F32)
        acc_ref[:, :DA_V_DIM] = alpha * acc_ref[:, :DA_V_DIM] + pv[:, :DA_V_DIM]
        acc_ref[:, DA_V_DIM:] = alpha * acc_ref[:, DA_V_DIM:] + pv[:, DA_V_DIM:]
        m_ref[...] = m_new

    bufs = (sa_ref, sb_ref)
    scores(0, bufs[0])
    for j in range(nk):
        if j + 1 < nk:
            scores(j + 1, bufs[(j + 1) % 2])
        update(j, bufs[j % 2])

    o = acc_ref[:, :DA_V_DIM] / acc_ref[:, DA_V_DIM:]
    lv = lam_ref[...]
    lam = (jnp.exp(jnp.sum(lv[0:1, :] * lv[1:2, :], axis=1, keepdims=True))
           - jnp.exp(jnp.sum(lv[2:3, :] * lv[3:4, :], axis=1, keepdims=True)) + lam_init)
    d = o[:tq] - lam * o[tq:]
    ms = jnp.mean(d * d, axis=1, keepdims=True)
    o_ref[...] = (d * lax.rsqrt(ms + EPS) * subg_ref[...] * (1.0 - lam_init)).astype(BF16)


def _attention(q, k, v, lamv, subg, *, tq, q_start, n_q, kv_start, kv_len, tk, lam_init):
    bsz = q.shape[0]
    q_blk0 = q_start // tq
    kv_blk = kv_start // kv_len
    q_spec = pl.BlockSpec((None, tq, LANES), lambda b, h, i: (b, q_blk0 + i, h))
    o_spec = pl.BlockSpec((None, tq, LANES), lambda b, h, i: (b, i, h))
    k_spec = pl.BlockSpec((None, kv_len, LANES), lambda b, h, i: (b, kv_blk, h))
    v_spec = pl.BlockSpec((None, kv_len, 2 * DA_V_DIM), lambda b, h, i: (b, kv_blk, h))
    const = lambda shape: pl.BlockSpec(shape, lambda b, h, i: (0,) * len(shape))
    return pl.pallas_call(
        functools.partial(_attn_kernel, tq=tq, tk=tk, nk=kv_len // tk, lam_init=lam_init),
        grid=(bsz, DA_HEADS, n_q),
        in_specs=[const((SUBLANES, LANES)), const((1, LANES)), q_spec, k_spec, v_spec],
        out_specs=o_spec,
        out_shape=jax.ShapeDtypeStruct((bsz, n_q * tq, DA_WIDTH), BF16),
        scratch_shapes=[
            pltpu.VMEM((2 * tq, LANES), BF16),
            pltpu.VMEM((2 * tq, LANES), F32),
            pltpu.VMEM((2 * tq, 2 * DA_V_DIM), F32),
            pltpu.VMEM((2 * tq, tk), F32),
            pltpu.VMEM((2 * tq, tk), F32),
        ],
        compiler_params=_cparams(("parallel", "parallel", "arbitrary"), 48),
        name="diff_attn",
    )(lamv, subg, q, k, v)


PAIRS_PER_GROUP = EXPERTS_PER_GROUP * (EXPERTS_PER_GROUP - 1) // 2
N_BUCKETS = N_GROUPS * PAIRS_PER_GROUP
META_BUCKET, META_RANK, META_W_LO, META_W_HI = 0, 1, 2, 3


def _post_tail(y, first_step, x, mod_ref, g_ref, wr_ref, br_ref, x1_ref, hf_ref, meta_ref, rt_ref, cnt_ref,
               cnt_scr):
    x1 = x + mod_ref[2:3, :] * y
    x1_ref[...] = x1
    hf = _norm_mod(x1, g_ref[...], mod_ref[3:4, :], mod_ref[4:5, :])
    hf_ref[...] = hf

    hf_hi = hf.astype(BF16)
    hf_lo = (hf - hf_hi.astype(F32)).astype(BF16)
    logits = (jnp.dot(hf_hi, wr_ref[0], preferred_element_type=F32)
              + jnp.dot(hf_lo, wr_ref[0], preferred_element_type=F32)
              + jnp.dot(hf_hi, wr_ref[1], preferred_element_type=F32)) + br_ref[...]
    lane = lax.broadcasted_iota(jnp.int32, logits.shape, 1).astype(F32)
    big = float(LANES)
    is_g = (lane >= N_EXPERTS) & (lane < N_EXPERTS + N_GROUPS)
    gl = jnp.where(is_g, logits, NEG)
    gm = jnp.max(gl, axis=1, keepdims=True)
    g_idx = jnp.min(jnp.where(gl == gm, lane, big), axis=1, keepdims=True) - N_EXPERTS
    p_sel = 1.0 / jnp.sum(jnp.exp(gl - gm), axis=1, keepdims=True)
    lo = g_idx * EXPERTS_PER_GROUP
    el = jnp.where((lane >= lo) & (lane < lo + EXPERTS_PER_GROUP), logits, NEG)
    v1 = jnp.max(el, axis=1, keepdims=True)
    i1 = jnp.min(jnp.where(el == v1, lane, big), axis=1, keepdims=True)
    el2 = jnp.where(lane == i1, NEG, el)
    v2 = jnp.max(el2, axis=1, keepdims=True)
    i2 = jnp.min(jnp.where(el2 == v2, lane, big), axis=1, keepdims=True)
    t = jnp.exp(v2 - v1)
    w1 = p_sel / (1.0 + t)
    w2 = t * w1

    first_lower = i1 < i2
    a = jnp.minimum(i1, i2) - lo
    b = jnp.maximum(i1, i2) - lo
    code = a * EXPERTS_PER_GROUP + b
    pair = jnp.where(code == 1.0, 0.0, jnp.where(code == 6.0, 1.0, jnp.where(code == 2.0, 2.0, jnp.where(
        code == 3.0, 3.0, jnp.where(code == 7.0, 4.0, 5.0)))))
    bucket = g_idx * PAIRS_PER_GROUP + pair
    w_low = jnp.where(first_lower, w1, w2)
    w_high = jnp.where(first_lower, w2, w1)
    w_lo = jnp.where(code == 1.0, w_low, w_high)
    w_hi = jnp.where(code == 1.0, w_high, w_low)

    @pl.when(first_step)
    def _():
        cnt_scr[...] = jnp.zeros(cnt_scr.shape, F32)

    tm = logits.shape[0]
    onehot = jnp.where(lane == bucket, 1.0, 0.0)
    r_i = lax.broadcasted_iota(jnp.int32, (tm, tm), 0)
    c_i = lax.broadcasted_iota(jnp.int32, (tm, tm), 1)
    earlier = jnp.where(c_i < r_i, 1.0, 0.0).astype(BF16)
    prefix = jnp.dot(earlier, onehot.astype(BF16), preferred_element_type=F32)
    base = cnt_scr[0:1, :]
    rank = jnp.sum(onehot * (prefix + base), axis=1, keepdims=True)
    counts = jnp.broadcast_to(base + jnp.sum(onehot, axis=0, keepdims=True), cnt_scr.shape)
    cnt_scr[...] = counts
    cnt_ref[...] = counts
    meta = (jnp.where(lane == META_BUCKET, bucket, 0.0) + jnp.where(lane == META_RANK, rank, 0.0)
            + jnp.where(lane == META_W_LO, w_lo, 0.0) + jnp.where(lane == META_W_HI, w_hi, 0.0))
    meta_ref[...] = meta
    rt_ref[...] = jnp.transpose(meta)[0:SUBLANES, :]


def _post_even_kernel(cb_ref, p_ref, pprev_ref, pnext_ref, cw_ref, cbias_ref, olat_ref, octx_ref, w_ref,
                      xl_ref, xc_ref, mod_ref, g_ref, wr_ref, br_ref, x1_ref, hf_ref, meta_ref, rt_ref, cnt_ref, cnt_scr):
    i = pl.program_id(1)
    first_step = jnp.logical_and(pl.program_id(0) == 0, i == 0)
    pc = p_ref[...]
    row = lax.broadcasted_iota(jnp.int32, pc.shape, 0)
    has_prev = jnp.logical_and(i != 0, i != N_LAT_TILES)
    has_next = i < N_LAT_TILES - 1
    prev_row = jnp.where(has_prev, pprev_ref[SUBLANES - 1:SUBLANES, :], 0.0)
    next_row = jnp.where(has_next, pnext_ref[0:1, :], 0.0)
    up = jnp.where(row == 0, prev_row, pltpu.roll(pc, 1, axis=0))
    dn = jnp.where(row == TM - 1, next_row, pltpu.roll(pc, TM - 1, axis=0))
    conv = cbias_ref[...] + cw_ref[0:1, :] * up + cw_ref[1:2, :] * pc + cw_ref[2:3, :] * dn
    out_a = (cb_ref[...] * conv).astype(BF16)
    o = jnp.where(i == N_LAT_TILES, octx_ref[...], olat_ref[...])
    x = jnp.where(i == N_LAT_TILES, xc_ref[...], xl_ref[...])
    y = (jnp.dot(out_a, w_ref[0:CONV_WIDTH, :], preferred_element_type=F32)
         + jnp.dot(o, w_ref[CONV_WIDTH:, :], preferred_element_type=F32))
    _post_tail(y, first_step, x, mod_ref, g_ref, wr_ref, br_ref, x1_ref, hf_ref, meta_ref, rt_ref, cnt_ref,
               cnt_scr)


def _post_odd_kernel(y_ref, hf_in_ref, hb_in_ref, w_ref, x_ref, mod_ref, g_ref, wr_ref, br_ref,
                     x1_ref, hf_ref, meta_ref, rt_ref, cnt_ref, cnt_scr):
    first_step = jnp.logical_and(pl.program_id(0) == 0, pl.program_id(1) == 0)
    a = (y_ref[...] * (hf_in_ref[...] + hb_in_ref[...])).astype(BF16)
    y = jnp.dot(a, w_ref[...], preferred_element_type=F32)
    _post_tail(y, first_step, x_ref[...], mod_ref, g_ref, wr_ref, br_ref, x1_ref, hf_ref, meta_ref, rt_ref, cnt_ref,
               cnt_scr)


def _post_specs(bsz, rows):
    tok_spec = lambda width: pl.BlockSpec((None, TM, width), lambda b, i: (b, i, 0))
    const = lambda shape: pl.BlockSpec(shape, lambda b, i: (0,) * len(shape))
    tail_in = [pl.BlockSpec((None, SUBLANES, D_MODEL), _mod_index), const((1, D_MODEL)),
               const((2, D_MODEL, LANES)), const((1, LANES))]
    tiles = rows // TM
    out_specs = [tok_spec(D_MODEL), tok_spec(D_MODEL), tok_spec(LANES),
                 pl.BlockSpec((None, SUBLANES, TM), lambda b, i: (b * tiles + i, 0, 0)), const((SUBLANES, LANES))]
    out_shape = [jax.ShapeDtypeStruct((bsz, rows, D_MODEL), F32),
                 jax.ShapeDtypeStruct((bsz, rows, D_MODEL), F32),
                 jax.ShapeDtypeStruct((bsz, rows, LANES), F32),
                 jax.ShapeDtypeStruct((bsz * tiles, SUBLANES, TM), F32),
                 jax.ShapeDtypeStruct((SUBLANES, LANES), F32)]
    scratch = [pltpu.VMEM((SUBLANES, LANES), F32)]
    return tok_spec, const, tail_in, out_specs, out_shape, scratch


def _post_even(cb, p, conv_w, conv_b, o_lat, o_ctx, w_out, x, ctx, mod, g, wr, br):
    bsz = x.shape[0]
    tok_spec, const, tail_in, out_specs, out_shape, scratch = _post_specs(bsz, TOK)
    halo_blocks = TM // SUBLANES
    last_halo = TOK // SUBLANES - 1
    prev_spec = pl.BlockSpec((None, SUBLANES, CONV_WIDTH),
                             lambda b, i: (b, jnp.maximum(i * halo_blocks - 1, 0), 0))
    next_spec = pl.BlockSpec((None, SUBLANES, CONV_WIDTH),
                             lambda b, i: (b, jnp.minimum((i + 1) * halo_blocks, last_halo), 0))
    return pl.pallas_call(
        _post_even_kernel,
        grid=(bsz, N_TILES),
        in_specs=[tok_spec(CONV_WIDTH), tok_spec(CONV_WIDTH), prev_spec, next_spec,
                  const((SUBLANES, CONV_WIDTH)), const((1, CONV_WIDTH))] + _lat_ctx_specs(DA_WIDTH)
        + [const((D_MODEL, D_MODEL))] + _lat_ctx_specs(D_MODEL) + tail_in,
        out_specs=out_specs,
        out_shape=out_shape,
        scratch_shapes=scratch,
        compiler_params=_cparams(("arbitrary", "arbitrary"), 48),
        name="post_even",
    )(cb, p, p, p, conv_w, conv_b, o_lat, o_ctx, w_out, x, ctx, mod, g, wr, br)


def _post_odd(y, hfw, hbw, w_out, xs, mod, g, wr, br):
    bsz = xs.shape[0]
    tok_spec, const, tail_in, out_specs, out_shape, scratch = _post_specs(bsz, SEQ)
    return pl.pallas_call(
        _post_odd_kernel,
        grid=(bsz, N_LAT_TILES),
        in_specs=[tok_spec(LRU_WIDTH), tok_spec(LRU_WIDTH), tok_spec(LRU_WIDTH),
                  const((LRU_WIDTH, D_MODEL)), tok_spec(D_MODEL)] + tail_in,
        out_specs=out_specs,
        out_shape=out_shape,
        scratch_shapes=scratch,
        compiler_params=_cparams(("arbitrary", "arbitrary"), 48),
        name="post_odd",
    )(y, hfw, hbw, w_out, xs, mod, g, wr, br)


_PAIR_SLOTS = [(0, 1), (2, 1), (2, 0), (3, 0), (3, 1), (3, 2)]
_BUCKET_LO = [g * EXPERTS_PER_GROUP + a for g in range(N_GROUPS) for a, _ in _PAIR_SLOTS]
_BUCKET_HI = [g * EXPERTS_PER_GROUP + b for g in range(N_GROUPS) for _, b in _PAIR_SLOTS]


def _sorted_tiles(n_tokens):
    return n_tokens // TM + N_BUCKETS


def _route_plan(rt, counts, n_tokens):
    n_tiles = _sorted_tiles(n_tokens)
    bucket = rt[:, META_BUCKET, :].astype(jnp.int32).reshape(n_tokens)
    rank = rt[:, META_RANK, :].astype(jnp.int32).reshape(n_tokens)
    cnt = counts[0, :N_BUCKETS].astype(jnp.int32)
    tiles_per = (cnt + TM - 1) // TM
    tile_end = jnp.cumsum(tiles_per)
    row_start = (tile_end - tiles_per) * TM
    dest = (row_start[bucket] + rank).reshape(n_tokens // TM, 1, TM)
    tile_bucket = jnp.minimum(jnp.sum(jnp.arange(n_tiles)[:, None] >= tile_end[None, :], axis=1), N_BUCKETS - 1)
    e_lo = jnp.asarray(_BUCKET_LO, jnp.int32)[tile_bucket]
    e_hi = jnp.asarray(_BUCKET_HI, jnp.int32)[tile_bucket]
    return dest, e_lo, e_hi, tile_end[-1:].astype(jnp.int32)


def _invert_kernel(dest_ref, src_ref):
    i = pl.program_id(0)

    @pl.when(i == 0)
    def _():
        def clear(k, carry):
            src_ref[k] = 0
            return carry
        lax.fori_loop(0, src_ref.shape[0], clear, 0, unroll=8)

    base = i * TM

    def put(r, carry):
        src_ref[dest_ref[0, r]] = base + r
        return carry

    lax.fori_loop(0, TM, put, 0, unroll=8)


def _invert(dest, n_tokens):
    rows = _sorted_tiles(n_tokens) * TM
    return pl.pallas_call(
        _invert_kernel,
        grid=(n_tokens // TM,),
        in_specs=[pl.BlockSpec((None, 1, TM), lambda i: (i, 0, 0), memory_space=pltpu.SMEM)],
        out_specs=pl.BlockSpec(memory_space=pltpu.SMEM),
        out_shape=jax.ShapeDtypeStruct((rows,), jnp.int32),
        compiler_params=_cparams(("arbitrary",), 16),
        name="moe_invert",
    )(dest)


def _gather_rows(idx_ref, src_hbm, dst_ref, sem):
    for r in range(TM):
        pltpu.make_async_copy(src_hbm.at[pl.ds(idx_ref[0, r], 1), :], dst_ref.at[pl.ds(r, 1), :], sem).start()


def _gather_wait(src_hbm, dst_ref, sem):
    pltpu.make_async_copy(src_hbm.at[pl.ds(0, TM), :], dst_ref, sem).wait()


def _moe_routed_kernel(elo_ref, ehi_ref, nused_ref, src_ref, src_next_ref, hf_hbm,
                       wg_lo, wu_lo, wd_lo, wg_hi, wu_hi, wd_hi, y_ref, hbuf_ref, sems):
    del elo_ref, ehi_ref
    j = pl.program_id(0)
    n_used = nused_ref[0]
    slot = j % 2

    @pl.when(j == 0)
    def _():
        _gather_rows(src_ref, hf_hbm, hbuf_ref.at[0], sems.at[0])

    @pl.when(j < n_used)
    def _():
        _gather_wait(hf_hbm, hbuf_ref.at[slot], sems.at[slot])
        _gather_rows(src_next_ref, hf_hbm, hbuf_ref.at[1 - slot], sems.at[1 - slot])
        h = hbuf_ref[slot].astype(BF16)

        def expert(wg_ref, wu_ref, wd_ref):
            hg = jnp.dot(h, wg_ref[...].astype(BF16), preferred_element_type=F32)
            hu = jnp.dot(h, wu_ref[...].astype(BF16), preferred_element_type=F32)
            act = (hg * _sigmoid(hg)) * hu
            return jnp.dot(act.astype(BF16), wd_ref[...].astype(BF16), preferred_element_type=F32)

        y_ref[:, :D_MODEL] = expert(wg_lo, wu_lo, wd_lo)
        y_ref[:, D_MODEL:] = expert(wg_hi, wu_hi, wd_hi)

    @pl.when(j == n_used)
    def _():
        _gather_wait(hf_hbm, hbuf_ref.at[slot], sems.at[slot])

    @pl.when(j >= n_used)
    def _():
        y_ref[...] = jnp.zeros(y_ref.shape, F32)


def _moe_routed(src, hf, e_lo, e_hi, n_used, wg, wu, wd, layer):
    n_tiles = src.shape[0]
    up_spec = lambda tbl: pl.BlockSpec((None, None, D_MODEL, D_EXPERT),
                                       lambda j, lo, hi, nu: (layer, (lo, hi)[tbl][j], 0, 0))
    dn_spec = lambda tbl: pl.BlockSpec((None, None, D_EXPERT, D_MODEL),
                                       lambda j, lo, hi, nu: (layer, (lo, hi)[tbl][j], 0, 0))
    grid_spec = pltpu.PrefetchScalarGridSpec(
        num_scalar_prefetch=3,
        grid=(n_tiles,),
        in_specs=[pl.BlockSpec((None, 1, TM), lambda j, lo, hi, nu: (j, 0, 0), memory_space=pltpu.SMEM),
                  pl.BlockSpec((None, 1, TM), lambda j, lo, hi, nu: (jnp.minimum(j + 1, n_tiles - 1), 0, 0),
                               memory_space=pltpu.SMEM),
                  pl.BlockSpec(memory_space=pl.ANY),
                  up_spec(0), up_spec(0), dn_spec(0), up_spec(1), up_spec(1), dn_spec(1)],
        out_specs=pl.BlockSpec((TM, 2 * D_MODEL), lambda j, lo, hi, nu: (j, 0)),
        scratch_shapes=[pltpu.VMEM((2, TM, D_MODEL), F32), pltpu.SemaphoreType.DMA((2,))],
    )
    return pl.pallas_call(
        _moe_routed_kernel,
        grid_spec=grid_spec,
        out_shape=jax.ShapeDtypeStruct((n_tiles * TM, 2 * D_MODEL), F32),
        compiler_params=_cparams(("arbitrary",), 56),
        name="moe_routed",
    )(e_lo, e_hi, n_used, src, src, hf, wg, wu, wd, wg, wu, wd)


def _combine_kernel(dest_ref, dest_next_ref, y_hbm, x_ref, meta_ref, mod_ref, o_ref, buf_ref, sems):
    step = pl.program_id(0) * pl.num_programs(1) + pl.program_id(1)
    n_steps = pl.num_programs(0) * pl.num_programs(1)
    slot = step % 2

    @pl.when(step == 0)
    def _():
        _gather_rows(dest_ref, y_hbm, buf_ref.at[0], sems.at[0])

    @pl.when(step + 1 < n_steps)
    def _():
        _gather_rows(dest_next_ref, y_hbm, buf_ref.at[1 - slot], sems.at[1 - slot])

    _gather_wait(y_hbm, buf_ref.at[slot], sems.at[slot])
    meta = meta_ref[...]
    moe = (meta[:, META_W_LO:META_W_LO + 1] * buf_ref[slot, :, :D_MODEL]
           + meta[:, META_W_HI:META_W_HI + 1] * buf_ref[slot, :, D_MODEL:])
    o_ref[...] = x_ref[...] + mod_ref[5:6, :] * moe


def _combine(dest, y, x1, meta, mod, *, tiles_per_batch):
    bsz = x1.shape[0]
    n_steps = bsz * tiles_per_batch
    tok_spec = lambda width: pl.BlockSpec((None, TM, width), lambda b, i: (b, i, 0))
    idx_spec = lambda ahead: pl.BlockSpec(
        (None, 1, TM), lambda b, i: (jnp.minimum(b * tiles_per_batch + i + ahead, n_steps - 1), 0, 0),
        memory_space=pltpu.SMEM)
    return pl.pallas_call(
        _combine_kernel,
        grid=(bsz, tiles_per_batch),
        in_specs=[idx_spec(0), idx_spec(1), pl.BlockSpec(memory_space=pl.ANY), tok_spec(D_MODEL),
                  tok_spec(LANES), pl.BlockSpec((None, SUBLANES, D_MODEL), _mod_index)],
        out_specs=tok_spec(D_MODEL),
        out_shape=jax.ShapeDtypeStruct(x1.shape, F32),
        scratch_shapes=[pltpu.VMEM((2, TM, 2 * D_MODEL), F32), pltpu.SemaphoreType.DMA((2,))],
        compiler_params=_cparams(("arbitrary", "arbitrary"), 32),
        name="moe_combine",
    )(dest, dest, y, x1, meta, mod)


def _moe(hf, meta, rt, counts, x1, mod, wg, wu, wd, layer, *, tiles_per_batch):
    n_tokens = hf.shape[0] * hf.shape[1]
    dest, e_lo, e_hi, n_used = _route_plan(rt, counts, n_tokens)
    src = _invert(dest, n_tokens).reshape(_sorted_tiles(n_tokens), 1, TM)
    y = _moe_routed(src, hf.reshape(n_tokens, D_MODEL), e_lo, e_hi, n_used, wg, wu, wd, layer)
    return _combine(dest, y, x1, meta, mod, tiles_per_batch=tiles_per_batch)


def _inproj_odd_kernel(x_ref, mod_ref, g_ref, w_ref, y_ref, u_ref):
    h = _norm_mod(x_ref[...], g_ref[...], mod_ref[0:1, :], mod_ref[1:2, :])
    z = jnp.dot(h.astype(BF16), w_ref[...], preferred_element_type=F32)
    zy = z[:, :LRU_WIDTH]
    c0 = math.sqrt(2.0 / math.pi)
    y_ref[...] = 0.5 * zy * (1.0 + jnp.tanh(c0 * (zy + 0.044715 * (zy * zy * zy))))
    u_ref[...] = z[:, LRU_WIDTH:]


def _inproj_odd(xs, mod, g, w_in):
    bsz = xs.shape[0]
    tok_spec = lambda width: pl.BlockSpec((None, TM, width), lambda b, i: (b, i, 0))
    const = lambda shape: pl.BlockSpec(shape, lambda b, i: (0,) * len(shape))
    return pl.pallas_call(
        _inproj_odd_kernel,
        grid=(bsz, N_TILES),
        in_specs=[tok_spec(D_MODEL), pl.BlockSpec((None, SUBLANES, D_MODEL), _mod_index),
                  const((1, D_MODEL)), const((D_MODEL, 2 * LRU_WIDTH))],
        out_specs=[tok_spec(LRU_WIDTH), tok_spec(LRU_WIDTH)],
        out_shape=[jax.ShapeDtypeStruct((bsz, TOK, LRU_WIDTH), F32),
                   jax.ShapeDtypeStruct((bsz, TOK, LRU_WIDTH), F32)],
        compiler_params=_cparams(("parallel", "arbitrary"), 48),
        name="inproj_odd",
    )(xs, mod, g, w_in)


def _scan_kernel(u_ref, cw_ref, cbias_ref, wa_ref, ba_ref, wx_ref, bx_ref, lam_ref, h_ref,
                 halo_ref, carry_ref, a_ref, b_ref, *, reverse):
    i = pl.program_id(1)
    tt, w = u_ref.shape
    n_groups = tt // SUBLANES

    @pl.when(i <= 1)
    def _():
        halo_ref[...] = jnp.zeros(halo_ref.shape, F32)

    @pl.when(i == 0)
    def _():
        carry_ref[...] = jnp.zeros(carry_ref.shape, F32)

    u = u_ref[...]
    halo = halo_ref[...]
    row8 = lax.broadcasted_iota(jnp.int32, (SUBLANES, w), 0)
    k_self = 0 if reverse else LRU_CONV_K - 1
    uc = cbias_ref[...] + cw_ref[k_self:k_self + 1, :] * u
    for k in range(1, LRU_CONV_K):
        if reverse:
            tmp = pltpu.roll(u, tt - k, axis=0)
            hr = pltpu.roll(halo, SUBLANES - k, axis=0)
            edge = jnp.where(row8 >= SUBLANES - k, hr, tmp[tt - SUBLANES:, :])
            shifted = jnp.concatenate([tmp[:tt - SUBLANES, :], edge], axis=0)
            wk = cw_ref[k:k + 1, :]
        else:
            tmp = pltpu.roll(u, k, axis=0)
            hr = pltpu.roll(halo, k, axis=0)
            edge = jnp.where(row8 < k, hr, tmp[:SUBLANES, :])
            shifted = jnp.concatenate([edge, tmp[SUBLANES:, :]], axis=0)
            wk = cw_ref[LRU_CONV_K - 1 - k:LRU_CONV_K - k, :]
        uc = uc + wk * shifted
    halo_ref[...] = u[:SUBLANES, :] if reverse else u[tt - SUBLANES:, :]

    ucb = uc.astype(BF16)

    def block_diag(w_blocks):
        return jnp.concatenate(
            [jnp.dot(ucb[:, j * LRU_BLOCK:(j + 1) * LRU_BLOCK], w_blocks[j], preferred_element_type=F32)
             for j in range(LRU_BLOCKS)], axis=1)

    r = _sigmoid(block_diag(wa_ref) + ba_ref[...])
    gate_i = _sigmoid(block_diag(wx_ref) + bx_ref[...])
    neg_lam = -lam_ref[...]
    softplus = jnp.maximum(neg_lam, 0.0) + jnp.log1p(jnp.exp(-jnp.abs(neg_lam)))
    a = jnp.exp((-LRU_C * softplus) * r)
    a_ref[...] = a
    one_m_a2 = 1.0 - a * a
    root = jnp.where(one_m_a2 > 0.0, one_m_a2 * lax.rsqrt(one_m_a2), 0.0)
    b_ref[...] = root * (gate_i * uc)

    def group(gi, carry):
        g = n_groups - 1 - gi if reverse else gi
        off = pl.multiple_of(g * SUBLANES, SUBLANES)
        av = a_ref[pl.ds(off, SUBLANES), :]
        bv = b_ref[pl.ds(off, SUBLANES), :]
        for s in (1, 2, 4):
            if reverse:
                outside = row8 >= SUBLANES - s
                shift = SUBLANES - s
            else:
                outside = row8 < s
                shift = s
            a_sh = jnp.where(outside, 1.0, pltpu.roll(av, shift, axis=0))
            b_sh = jnp.where(outside, 0.0, pltpu.roll(bv, shift, axis=0))
            bv = av * b_sh + bv
            av = av * a_sh
        h = av * carry + bv
        h_ref[pl.ds(off, SUBLANES), :] = h
        last = h[0:1, :] if reverse else h[SUBLANES - 1:SUBLANES, :]
        return jnp.broadcast_to(last, (SUBLANES, w))

    carry_ref[...] = lax.fori_loop(0, n_groups, group, carry_ref[...])


def _rglru_scan(u, conv_w, conv_b, w_a, b_a, w_x, b_x, lam, *, reverse):
    bsz = u.shape[0]
    if reverse:
        tile_of = lambda i: N_LAT_TILES - i
    else:
        tile_of = lambda i: (i + N_LAT_TILES) % N_TILES
    tok_spec = pl.BlockSpec((None, TM, LRU_WIDTH), lambda b, i: (b, tile_of(i), 0))
    const = lambda shape: pl.BlockSpec(shape, lambda b, i: (0,) * len(shape))
    return pl.pallas_call(
        functools.partial(_scan_kernel, reverse=reverse),
        grid=(bsz, N_TILES),
        in_specs=[tok_spec, const((SUBLANES, LRU_WIDTH)), const((1, LRU_WIDTH)),
                  const((LRU_BLOCKS, LRU_BLOCK, LRU_BLOCK)), const((1, LRU_WIDTH)),
                  const((LRU_BLOCKS, LRU_BLOCK, LRU_BLOCK)), const((1, LRU_WIDTH)), const((1, LRU_WIDTH))],
        out_specs=tok_spec,
        out_shape=jax.ShapeDtypeStruct((bsz, TOK, LRU_WIDTH), F32),
        scratch_shapes=[pltpu.VMEM((SUBLANES, LRU_WIDTH), F32), pltpu.VMEM((SUBLANES, LRU_WIDTH), F32),
                        pltpu.VMEM((TM, LRU_WIDTH), F32), pltpu.VMEM((TM, LRU_WIDTH), F32)],
        compiler_params=_cparams(("parallel", "arbitrary"), 32),
        name="rglru_rev" if reverse else "rglru_fwd",
    )(u, conv_w, conv_b, w_a, b_a, w_x, b_x, lam)


def _pad_rows(a, rows):
    return jnp.pad(a, ((0, rows - a.shape[0]), (0, 0)))


def _rope_tables():
    t = jnp.arange(SEQ)
    n_freq = DA_HEAD_DIM // 4
    inv = ROPE_BASE ** (-jnp.arange(n_freq, dtype=F32) / n_freq)
    ang = jnp.concatenate([(t // GRID_W).astype(F32)[:, None] * inv,
                           (t % GRID_W).astype(F32)[:, None] * inv], axis=-1)
    cos, sin = jnp.cos(ang), jnp.sin(ang)
    cos64 = jnp.concatenate([cos, cos], axis=-1)
    sin64 = jnp.concatenate([-sin, sin], axis=-1)
    cos_t = jnp.concatenate([jnp.tile(cos64, (1, 2)), jnp.ones((CTX_LEN, LANES), F32)], axis=0)
    sin_t = jnp.concatenate([jnp.tile(sin64, (1, 2)), jnp.zeros((CTX_LEN, LANES), F32)], axis=0)
    return cos_t, sin_t


def _router_params(w_grp, b_grp, w_rt, b_rt):
    wr = jnp.concatenate([w_rt.reshape(D_MODEL, N_EXPERTS), w_grp], axis=1)
    br = jnp.concatenate([b_rt.reshape(N_EXPERTS), b_grp])
    pad = LANES - wr.shape[1]
    wr = jnp.pad(wr, ((0, 0), (0, pad)))
    wr_hi = wr.astype(BF16)
    wr_lo = (wr - wr_hi.astype(F32)).astype(BF16)
    return jnp.stack([wr_hi, wr_lo]), jnp.pad(br, (0, pad)).reshape(1, LANES)


def kernel(x, c, ctx, c_ctx, ada_w, ada_b, norm_mix, norm_ffn, ev_w_in, ev_conv_w, ev_conv_b, ev_q_norm, ev_k_norm, ev_lam_q1, ev_lam_k1, ev_lam_q2, ev_lam_k2, ev_sub_norm, ev_w_out, od_w_in, od_conv_w, od_conv_b, od_w_a, od_b_a, od_w_x, od_b_x, od_lam, od_w_out, moe_w_grp, moe_b_grp, moe_w_rt, moe_b_rt, moe_w_gate, moe_w_up, moe_w_down):
    bsz = x.shape[0]
    assert x.shape == (bsz, SEQ, D_MODEL) and ctx.shape == (bsz, CTX_LEN, D_MODEL) and bsz == 2
    depth = ada_w.shape[0]
    assert depth == 2


    cvec = _pad_rows(jnp.stack([c[0], c_ctx, c[1], c_ctx]), SUBLANES)
    mod_all = _ada_mod(cvec, ada_w, ada_b)
    mod_all = mod_all[:, :2 * bsz].reshape(depth, 2 * bsz, 6, D_MODEL)
    mod_all = jnp.pad(mod_all, ((0, 0), (0, 0), (0, SUBLANES - 6), (0, 0)))

    l = 0
    lam_init = 0.8 - 0.6 * math.exp(-0.3 * l)
    mod = mod_all[l]
    cos_t, sin_t = _rope_tables()
    blk = jnp.arange(DA_WIDTH) // DA_HEAD_DIM
    bd = jnp.where(blk[:, None] == blk[None, :], 1.0 / DA_HEAD_DIM, 0.0).astype(BF16)
    n_rep = DA_WIDTH // DA_HEAD_DIM
    cb, p, q, k, v = _inproj_even(
        x, ctx, mod, norm_mix[l].reshape(1, D_MODEL), ev_w_in[0].astype(BF16),
        jnp.tile(ev_q_norm[0], n_rep).reshape(1, DA_WIDTH), jnp.tile(ev_k_norm[0], n_rep).reshape(1, DA_WIDTH),
        cos_t, sin_t, bd)

    lamv = _pad_rows(jnp.pad(jnp.stack([ev_lam_q1[0], ev_lam_k1[0], ev_lam_q2[0], ev_lam_k2[0]]),
                             ((0, 0), (0, LANES - DA_HEAD_DIM))), SUBLANES)
    subg = ev_sub_norm[0].reshape(1, DA_V_DIM)
    o_lat = _attention(q, k, v, lamv, subg, tq=256, q_start=0, n_q=SEQ // 256,
                       kv_start=0, kv_len=TOK, tk=768, lam_init=lam_init)
    o_ctx = _attention(q, k, v, lamv, subg, tq=TM, q_start=SEQ, n_q=CTX_LEN // TM,
                       kv_start=SEQ, kv_len=CTX_LEN, tk=CTX_LEN, lam_init=lam_init)

    wr, br = _router_params(moe_w_grp[l], moe_b_grp[l], moe_w_rt[l], moe_b_rt[l])
    x1, hf, meta, rt, counts = _post_even(
        cb, p, _pad_rows(ev_conv_w[0], SUBLANES), ev_conv_b[0].reshape(1, CONV_WIDTH), o_lat, o_ctx,
        ev_w_out[0].astype(BF16), x, ctx, mod, norm_ffn[l].reshape(1, D_MODEL), wr, br)
    xs = _moe(hf, meta, rt, counts, x1, mod, moe_w_gate, moe_w_up, moe_w_down, l, tiles_per_batch=N_TILES)

    l = 1
    mod = mod_all[l]
    y, u = _inproj_odd(xs, mod, norm_mix[l].reshape(1, D_MODEL), od_w_in[0].astype(BF16))
    h_dirs = []
    for d in range(2):
        h_dirs.append(_rglru_scan(
            u, _pad_rows(od_conv_w[0, d], SUBLANES), od_conv_b[0, d].reshape(1, LRU_WIDTH),
            od_w_a[0, d].astype(BF16), od_b_a[0, d].reshape(1, LRU_WIDTH),
            od_w_x[0, d].astype(BF16), od_b_x[0, d].reshape(1, LRU_WIDTH),
            od_lam[0, d].reshape(1, LRU_WIDTH), reverse=bool(d)))
    wr, br = _router_params(moe_w_grp[l], moe_b_grp[l], moe_w_rt[l], moe_b_rt[l])
    x1, hf, meta, rt, counts = _post_odd(y, h_dirs[0], h_dirs[1], od_w_out[0].astype(BF16), xs, mod,
                                     norm_ffn[l].reshape(1, D_MODEL), wr, br)
    return _moe(hf, meta, rt, counts, x1, mod, moe_w_gate, moe_w_up, moe_w_down, l,
                tiles_per_batch=N_LAT_TILES)
```

```python
import functools
import math

import jax
import jax.numpy as jnp
from jax import lax
from jax.experimental import pallas as pl
from jax.experimental.pallas import tpu as pltpu

F32 = jnp.float32
BF16 = jnp.bfloat16
HIGHEST = lax.Precision.HIGHEST

D_MODEL = 1024
SEQ = 8192
CTX_LEN = 256
TOK = SEQ + CTX_LEN
GRID_W = 64
EPS = 1e-6

CONV_WIDTH = 512
DA_HEADS = 4
DA_HEAD_DIM = 64
DA_V_DIM = 2 * DA_HEAD_DIM
DA_WIDTH = DA_HEADS * DA_V_DIM
EVEN_IN = 3 * CONV_WIDTH + 3 * DA_WIDTH
ROPE_BASE = 10000.0

LRU_WIDTH = 1024
LRU_BLOCKS = 8
LRU_BLOCK = LRU_WIDTH // LRU_BLOCKS
LRU_CONV_K = 4
LRU_C = 8.0

N_GROUPS = 4
EXPERTS_PER_GROUP = 4
N_EXPERTS = N_GROUPS * EXPERTS_PER_GROUP
D_EXPERT = 512

LANES = 128
SUBLANES = 8
TM = 256
N_LAT_TILES = SEQ // TM
N_TILES = TOK // TM
NEG = -1e30
MIB = 2 ** 20


def _cparams(semantics, vmem_mib):
    return pltpu.CompilerParams(dimension_semantics=semantics, vmem_limit_bytes=vmem_mib * MIB)


def _sigmoid(x):
    return 0.5 * jnp.tanh(0.5 * x) + 0.5


def _norm_mod(x, g, shift, scale):
    ms = jnp.mean(x * x, axis=-1, keepdims=True)
    return (x * lax.rsqrt(ms + EPS) * g) * (1.0 + scale) + shift


ADA_TN = 1536


def _ada_kernel(c_ref, w_ref, b_ref, o_ref):
    c = c_ref[...]
    a = c * _sigmoid(c)
    o_ref[...] = jnp.dot(a, w_ref[...], precision=HIGHEST, preferred_element_type=F32) + b_ref[...]


def _ada_mod(cvec, ada_w, ada_b):
    depth, d, n = ada_w.shape
    return pl.pallas_call(
        _ada_kernel,
        grid=(depth, n // ADA_TN),
        in_specs=[
            pl.BlockSpec((SUBLANES, d), lambda l, j: (0, 0)),
            pl.BlockSpec((None, d, ADA_TN), lambda l, j: (l, 0, j)),
            pl.BlockSpec((None, 1, ADA_TN), lambda l, j: (l, 0, j)),
        ],
        out_specs=pl.BlockSpec((None, SUBLANES, ADA_TN), lambda l, j: (l, 0, j)),
        out_shape=jax.ShapeDtypeStruct((depth, SUBLANES, n), F32),
        compiler_params=_cparams(("arbitrary", "arbitrary"), 40),
        name="ada_mod",
    )(cvec, ada_w, ada_b.reshape(depth, 1, n))


def _mod_index(b, i):
    return (b * 2 + i // N_LAT_TILES, 0, 0)


def _inproj_even_kernel(xl_ref, xc_ref, mod_ref, g_ref, w_ref, qg_ref, kg_ref, cos_ref, sin_ref, bd_ref,
                        cb_ref, p_ref, q_ref, k_ref, v_ref):
    x = jnp.where(pl.program_id(1) == N_LAT_TILES, xc_ref[...], xl_ref[...])
    h = _norm_mod(x, g_ref[...], mod_ref[0:1, :], mod_ref[1:2, :])
    z = jnp.dot(h.astype(BF16), w_ref[...], preferred_element_type=F32)
    cw = CONV_WIDTH
    cb_ref[...] = z[:, :cw]
    p_ref[...] = z[:, cw:2 * cw] * z[:, 2 * cw:3 * cw]

    reps = DA_WIDTH // LANES
    cosf = jnp.concatenate([cos_ref[...]] * reps, axis=1)
    sinf = jnp.concatenate([sin_ref[...]] * reps, axis=1)
    lane = lax.broadcasted_iota(jnp.int32, (TM, DA_WIDTH), 1)
    first_half = (lane & (DA_HEAD_DIM - 1)) < DA_HEAD_DIM // 2
    bd = bd_ref[...]

    def head_norm_rope(t, gain):
        ms = jnp.dot((t * t).astype(BF16), bd, preferred_element_type=F32)
        y = t * lax.rsqrt(ms + EPS) * gain
        fwd = pltpu.roll(y, DA_WIDTH - DA_HEAD_DIM // 2, axis=1)
        bwd = pltpu.roll(y, DA_HEAD_DIM // 2, axis=1)
        return y * cosf + jnp.where(first_half, fwd, bwd) * sinf

    base = 3 * cw
    q = head_norm_rope(z[:, base:base + DA_WIDTH], qg_ref[...])
    q_ref[...] = (q * (DA_HEAD_DIM ** -0.5 * math.log2(math.e))).astype(BF16)
    k = head_norm_rope(z[:, base + DA_WIDTH:base + 2 * DA_WIDTH], kg_ref[...])
    k_ref[...] = k.astype(BF16)
    v = z[:, base + 2 * DA_WIDTH:].astype(BF16)
    ones = jnp.ones((TM, DA_V_DIM), BF16)
    v_ref[...] = jnp.concatenate(
        [blk for h in range(DA_HEADS) for blk in (v[:, h * DA_V_DIM:(h + 1) * DA_V_DIM], ones)], axis=1)


def _lat_ctx_specs(width):
    return [pl.BlockSpec((None, TM, width), lambda b, i: (b, jnp.minimum(i, N_LAT_TILES - 1), 0)),
            pl.BlockSpec((None, TM, width), lambda b, i: (b, 0, 0))]


def _inproj_even(x, ctx, mod, g, w_in, qg, kg, cos, sin, bd):
    bsz = x.shape[0]
    tok_spec = lambda width: pl.BlockSpec((None, TM, width), lambda b, i: (b, i, 0))
    const = lambda shape: pl.BlockSpec(shape, lambda b, i: (0,) * len(shape))
    return pl.pallas_call(
        _inproj_even_kernel,
        grid=(bsz, N_TILES),
        in_specs=_lat_ctx_specs(D_MODEL) + [
            pl.BlockSpec((None, SUBLANES, D_MODEL), _mod_index),
            const((1, D_MODEL)),
            const((D_MODEL, EVEN_IN)),
            const((1, DA_WIDTH)),
            const((1, DA_WIDTH)),
            pl.BlockSpec((TM, LANES), lambda b, i: (i, 0)),
            pl.BlockSpec((TM, LANES), lambda b, i: (i, 0)),
            const((DA_WIDTH, DA_WIDTH)),
        ],
        out_specs=[tok_spec(CONV_WIDTH), tok_spec(CONV_WIDTH), tok_spec(DA_WIDTH), tok_spec(DA_WIDTH),
                   tok_spec(2 * DA_WIDTH)],
        out_shape=[
            jax.ShapeDtypeStruct((bsz, TOK, CONV_WIDTH), F32),
            jax.ShapeDtypeStruct((bsz, TOK, CONV_WIDTH), F32),
            jax.ShapeDtypeStruct((bsz, TOK, DA_WIDTH), BF16),
            jax.ShapeDtypeStruct((bsz, TOK, DA_WIDTH), BF16),
            jax.ShapeDtypeStruct((bsz, TOK, 2 * DA_WIDTH), BF16),
        ],
        compiler_params=_cparams(("parallel", "arbitrary"), 48),
        name="inproj_even",
    )(x, ctx, mod, g, w_in, qg, kg, cos, sin, bd)


def _attn_kernel(lam_ref, subg_ref, q_ref, k_ref, v_ref, o_ref, qz_ref, m_ref, acc_ref,
                 sa_ref, sb_ref, *, tq, tk, nk, lam_init):
    q = q_ref[...]
    lane = lax.broadcasted_iota(jnp.int32, (tq, LANES), 1)
    zero = jnp.zeros_like(q)
    qz_ref[0:tq, :] = jnp.where(lane < DA_HEAD_DIM, q, zero)
    qz_ref[tq:2 * tq, :] = jnp.where(lane >= DA_HEAD_DIM, q, zero)
    m_ref[...] = jnp.full(m_ref.shape, NEG, F32)
    acc_ref[...] = jnp.zeros(acc_ref.shape, F32)

    def scores(j, dst_ref):
        off = pl.multiple_of(j * tk, tk)
        dst_ref[...] = lax.dot_general(qz_ref[...], k_ref[pl.ds(off, tk), :], (((1,), (1,)), ((), ())),
                                       preferred_element_type=F32)

    def update(j, src_ref):
        off = pl.multiple_of(j * tk, tk)
        s = src_ref[...]
        m_prev = m_ref[...]
        m_new = jnp.maximum(m_prev, jnp.max(s, axis=1, keepdims=True))
        alpha = jnp.exp2(m_prev - m_new)
        p = jnp.exp2((s - m_new[:, :1]).astype(BF16))
        pv = jnp.dot(p, v_ref[pl.ds(off, tk), :], preferred_element_type=F32)
        acc_ref[:, :DA_V_DIM] = alpha * acc_ref[:, :DA_V_DIM] + pv[:, :DA_V_DIM]
        acc_ref[:, DA_V_DIM:] = alpha * acc_ref[:, DA_V_DIM:] + pv[:, DA_V_DIM:]
        m_ref[...] = m_new

    bufs = (sa_ref, sb_ref)
    scores(0, bufs[0])
    for j in range(nk):
        if j + 1 < nk:
            scores(j + 1, bufs[(j + 1) % 2])
        update(j, bufs[j % 2])

    o = acc_ref[:, :DA_V_DIM] / acc_ref[:, DA_V_DIM:]
    lv = lam_ref[...]
    lam = (jnp.exp(jnp.sum(lv[0:1, :] * lv[1:2, :], axis=1, keepdims=True))
           - jnp.exp(jnp.sum(lv[2:3, :] * lv[3:4, :], axis=1, keepdims=True)) + lam_init)
    d = o[:tq] - lam * o[tq:]
    ms = jnp.mean(d * d, axis=1, keepdims=True)
    o_ref[...] = (d * lax.rsqrt(ms + EPS) * subg_ref[...] * (1.0 - lam_init)).astype(BF16)


def _attention(q, k, v, lamv, subg, *, tq, q_start, n_q, kv_start, kv_len, tk, lam_init):
    bsz = q.shape[0]
    q_blk0 = q_start // tq
    kv_blk = kv_start // kv_len
    q_spec = pl.BlockSpec((None, tq, LANES), lambda b, h, i: (b, q_blk0 + i, h))
    o_spec = pl.BlockSpec((None, tq, LANES), lambda b, h, i: (b, i, h))
    k_spec = pl.BlockSpec((None, kv_len, LANES), lambda b, h, i: (b, kv_blk, h))
    v_spec = pl.BlockSpec((None, kv_len, 2 * DA_V_DIM), lambda b, h, i: (b, kv_blk, h))
    const = lambda shape: pl.BlockSpec(shape, lambda b, h, i: (0,) * len(shape))
    return pl.pallas_call(
        functools.partial(_attn_kernel, tq=tq, tk=tk, nk=kv_len // tk, lam_init=lam_init),
        grid=(bsz, DA_HEADS, n_q),
        in_specs=[const((SUBLANES, LANES)), const((1, LANES)), q_spec, k_spec, v_spec],
        out_specs=o_spec,
        out_shape=jax.ShapeDtypeStruct((bsz, n_q * tq, DA_WIDTH), BF16),
        scratch_shapes=[
            pltpu.VMEM((2 * tq, LANES), BF16),
            pltpu.VMEM((2 * tq, LANES), F32),
            pltpu.VMEM((2 * tq, 2 * DA_V_DIM), F32),
            pltpu.VMEM((2 * tq, tk), F32),
            pltpu.VMEM((2 * tq, tk), F32),
        ],
        compiler_params=_cparams(("parallel", "parallel", "arbitrary"), 48),
        name="diff_attn",
    )(lamv, subg, q, k, v)


PAIRS_PER_GROUP = EXPERTS_PER_GROUP * (EXPERTS_PER_GROUP - 1) // 2
N_BUCKETS = N_GROUPS * PAIRS_PER_GROUP
META_BUCKET, META_RANK, META_W_LO, META_W_HI = 0, 1, 2, 3


def _post_tail(y, first_step, x, mod_ref, g_ref, wr_ref, br_ref, x1_ref, hf_ref, meta_ref, rt_ref, cnt_ref,
               cnt_scr):
    x1 = x + mod_ref[2:3, :] * y
    x1_ref[...] = x1
    hf = _norm_mod(x1, g_ref[...], mod_ref[3:4, :], mod_ref[4:5, :])
    hf_ref[...] = hf

    hf_hi = hf.astype(BF16)
    hf_lo = (hf - hf_hi.astype(F32)).astype(BF16)
    logits = (jnp.dot(hf_hi, wr_ref[0], preferred_element_type=F32)
              + jnp.dot(hf_lo, wr_ref[0], preferred_element_type=F32)
              + jnp.dot(hf_hi, wr_ref[1], preferred_element_type=F32)) + br_ref[...]
    lane = lax.broadcasted_iota(jnp.int32, logits.shape, 1).astype(F32)
    big = float(LANES)
    is_g = (lane >= N_EXPERTS) & (lane < N_EXPERTS + N_GROUPS)
    gl = jnp.where(is_g, logits, NEG)
    gm = jnp.max(gl, axis=1, keepdims=True)
    g_idx = jnp.min(jnp.where(gl == gm, lane, big), axis=1, keepdims=True) - N_EXPERTS
    p_sel = 1.0 / jnp.sum(jnp.exp(gl - gm), axis=1, keepdims=True)
    lo = g_idx * EXPERTS_PER_GROUP
    el = jnp.where((lane >= lo) & (lane < lo + EXPERTS_PER_GROUP), logits, NEG)
    v1 = jnp.max(el, axis=1, keepdims=True)
    i1 = jnp.min(jnp.where(el == v1, lane, big), axis=1, keepdims=True)
    el2 = jnp.where(lane == i1, NEG, el)
    v2 = jnp.max(el2, axis=1, keepdims=True)
    i2 = jnp.min(jnp.where(el2 == v2, lane, big), axis=1, keepdims=True)
    t = jnp.exp(v2 - v1)
    w1 = p_sel / (1.0 + t)
    w2 = t * w1

    first_lower = i1 < i2
    a = jnp.minimum(i1, i2) - lo
    b = jnp.maximum(i1, i2) - lo
    code = a * EXPERTS_PER_GROUP + b
    pair = jnp.where(code == 1.0, 0.0, jnp.where(code == 6.0, 1.0, jnp.where(code == 2.0, 2.0, jnp.where(
        code == 3.0, 3.0, jnp.where(code == 7.0, 4.0, 5.0)))))
    bucket = g_idx * PAIRS_PER_GROUP + pair
    w_low = jnp.where(first_lower, w1, w2)
    w_high = jnp.where(first_lower, w2, w1)
    w_lo = jnp.where(code == 1.0, w_low, w_high)
    w_hi = jnp.where(code == 1.0, w_high, w_low)

    @pl.when(first_step)
    def _():
        cnt_scr[...] = jnp.zeros(cnt_scr.shape, F32)

    tm = logits.shape[0]
    onehot = jnp.where(lane == bucket, 1.0, 0.0)
    r_i = lax.broadcasted_iota(jnp.int32, (tm, tm), 0)
    c_i = lax.broadcasted_iota(jnp.int32, (tm, tm), 1)
    earlier = jnp.where(c_i < r_i, 1.0, 0.0).astype(BF16)
    prefix = jnp.dot(earlier, onehot.astype(BF16), preferred_element_type=F32)
    base = cnt_scr[0:1, :]
    rank = jnp.sum(onehot * (prefix + base), axis=1, keepdims=True)
    counts = jnp.broadcast_to(base + jnp.sum(onehot, axis=0, keepdims=True), cnt_scr.shape)
    cnt_scr[...] = counts
    cnt_ref[...] = counts
    meta = (jnp.where(lane == META_BUCKET, bucket, 0.0) + jnp.where(lane == META_RANK, rank, 0.0)
            + jnp.where(lane == META_W_LO, w_lo, 0.0) + jnp.where(lane == META_W_HI, w_hi, 0.0))
    meta_ref[...] = meta
    rt_ref[...] = jnp.transpose(meta)[0:SUBLANES, :]


def _post_even_kernel(cb_ref, p_ref, pprev_ref, pnext_ref, cw_ref, cbias_ref, olat_ref, octx_ref, w_ref,
                      xl_ref, xc_ref, mod_ref, g_ref, wr_ref, br_ref, x1_ref, hf_ref, meta_ref, rt_ref, cnt_ref, cnt_scr):
    i = pl.program_id(1)
    first_step = jnp.logical_and(pl.program_id(0) == 0, i == 0)
    pc = p_ref[...]
    row = lax.broadcasted_iota(jnp.int32, pc.shape, 0)
    has_prev = jnp.logical_and(i != 0, i != N_LAT_TILES)
    has_next = i < N_LAT_TILES - 1
    prev_row = jnp.where(has_prev, pprev_ref[SUBLANES - 1:SUBLANES, :], 0.0)
    next_row = jnp.where(has_next, pnext_ref[0:1, :], 0.0)
    up = jnp.where(row == 0, prev_row, pltpu.roll(pc, 1, axis=0))
    dn = jnp.where(row == TM - 1, next_row, pltpu.roll(pc, TM - 1, axis=0))
    conv = cbias_ref[...] + cw_ref[0:1, :] * up + cw_ref[1:2, :] * pc + cw_ref[2:3, :] * dn
    out_a = (cb_ref[...] * conv).astype(BF16)
    o = jnp.where(i == N_LAT_TILES, octx_ref[...], olat_ref[...])
    x = jnp.where(i == N_LAT_TILES, xc_ref[...], xl_ref[...])
    y = (jnp.dot(out_a, w_ref[0:CONV_WIDTH, :], preferred_element_type=F32)
         + jnp.dot(o, w_ref[CONV_WIDTH:, :], preferred_element_type=F32))
    _post_tail(y, first_step, x, mod_ref, g_ref, wr_ref, br_ref, x1_ref, hf_ref, meta_ref, rt_ref, cnt_ref,
               cnt_scr)


def _post_odd_kernel(y_ref, hf_in_ref, hb_in_ref, w_ref, x_ref, mod_ref, g_ref, wr_ref, br_ref,
                     x1_ref, hf_ref, meta_ref, rt_ref, cnt_ref, cnt_scr):
    first_step = jnp.logical_and(pl.program_id(0) == 0, pl.program_id(1) == 0)
    a = (y_ref[...] * (hf_in_ref[...] + hb_in_ref[...])).astype(BF16)
    y = jnp.dot(a, w_ref[...], preferred_element_type=F32)
    _post_tail(y, first_step, x_ref[...], mod_ref, g_ref, wr_ref, br_ref, x1_ref, hf_ref, meta_ref, rt_ref, cnt_ref,
               cnt_scr)


def _post_specs(bsz, rows):
    tok_spec = lambda width: pl.BlockSpec((None, TM, width), lambda b, i: (b, i, 0))
    const = lambda shape: pl.BlockSpec(shape, lambda b, i: (0,) * len(shape))
    tail_in = [pl.BlockSpec((None, SUBLANES, D_MODEL), _mod_index), const((1, D_MODEL)),
               const((2, D_MODEL, LANES)), const((1, LANES))]
    tiles = rows // TM
    out_specs = [tok_spec(D_MODEL), tok_spec(D_MODEL), tok_spec(LANES),
                 pl.BlockSpec((None, SUBLANES, TM), lambda b, i: (b * tiles + i, 0, 0)), const((SUBLANES, LANES))]
    out_shape = [jax.ShapeDtypeStruct((bsz, rows, D_MODEL), F32),
                 jax.ShapeDtypeStruct((bsz, rows, D_MODEL), F32),
                 jax.ShapeDtypeStruct((bsz, rows, LANES), F32),
                 jax.ShapeDtypeStruct((bsz * tiles, SUBLANES, TM), F32),
                 jax.ShapeDtypeStruct((SUBLANES, LANES), F32)]
    scratch = [pltpu.VMEM((SUBLANES, LANES), F32)]
    return tok_spec, const, tail_in, out_specs, out_shape, scratch


def _post_even(cb, p, conv_w, conv_b, o_lat, o_ctx, w_out, x, ctx, mod, g, wr, br):
    bsz = x.shape[0]
    tok_spec, const, tail_in, out_specs, out_shape, scratch = _post_specs(bsz, TOK)
    halo_blocks = TM // SUBLANES
    last_halo = TOK // SUBLANES - 1
    prev_spec = pl.BlockSpec((None, SUBLANES, CONV_WIDTH),
                             lambda b, i: (b, jnp.maximum(i * halo_blocks - 1, 0), 0))
    next_spec = pl.BlockSpec((None, SUBLANES, CONV_WIDTH),
                             lambda b, i: (b, jnp.minimum((i + 1) * halo_blocks, last_halo), 0))
    return pl.pallas_call(
        _post_even_kernel,
        grid=(bsz, N_TILES),
        in_specs=[tok_spec(CONV_WIDTH), tok_spec(CONV_WIDTH), prev_spec, next_spec,
                  const((SUBLANES, CONV_WIDTH)), const((1, CONV_WIDTH))] + _lat_ctx_specs(DA_WIDTH)
        + [const((D_MODEL, D_MODEL))] + _lat_ctx_specs(D_MODEL) + tail_in,
        out_specs=out_specs,
        out_shape=out_shape,
        scratch_shapes=scratch,
        compiler_params=_cparams(("arbitrary", "arbitrary"), 48),
        name="post_even",
    )(cb, p, p, p, conv_w, conv_b, o_lat, o_ctx, w_out, x, ctx, mod, g, wr, br)


def _post_odd(y, hfw, hbw, w_out, xs, mod, g, wr, br):
    bsz = xs.shape[0]
    tok_spec, const, tail_in, out_specs, out_shape, scratch = _post_specs(bsz, SEQ)
    return pl.pallas_call(
        _post_odd_kernel,
        grid=(bsz, N_LAT_TILES),
        in_specs=[tok_spec(LRU_WIDTH), tok_spec(LRU_WIDTH), tok_spec(LRU_WIDTH),
                  const((LRU_WIDTH, D_MODEL)), tok_spec(D_MODEL)] + tail_in,
        out_specs=out_specs,
        out_shape=out_shape,
        scratch_shapes=scratch,
        compiler_params=_cparams(("arbitrary", "arbitrary"), 48),
        name="post_odd",
    )(y, hfw, hbw, w_out, xs, mod, g, wr, br)


_PAIR_SLOTS = [(0, 1), (2, 1), (2, 0), (3, 0), (3, 1), (3, 2)]
_BUCKET_LO = [g * EXPERTS_PER_GROUP + a for g in range(N_GROUPS) for a, _ in _PAIR_SLOTS]
_BUCKET_HI = [g * EXPERTS_PER_GROUP + b for g in range(N_GROUPS) for _, b in _PAIR_SLOTS]


def _sorted_tiles(n_tokens):
    return n_tokens // TM + N_BUCKETS


def _route_plan(rt, counts, n_tokens):
    n_tiles = _sorted_tiles(n_tokens)
    bucket = rt[:, META_BUCKET, :].astype(jnp.int32).reshape(n_tokens)
    rank = rt[:, META_RANK, :].astype(jnp.int32).reshape(n_tokens)
    cnt = counts[0, :N_BUCKETS].astype(jnp.int32)
    tiles_per = (cnt + TM - 1) // TM
    tile_end = jnp.cumsum(tiles_per)
    row_start = (tile_end - tiles_per) * TM
    dest = (row_start[bucket] + rank).reshape(n_tokens // TM, 1, TM)
    tile_bucket = jnp.minimum(jnp.sum(jnp.arange(n_tiles)[:, None] >= tile_end[None, :], axis=1), N_BUCKETS - 1)
    e_lo = jnp.asarray(_BUCKET_LO, jnp.int32)[tile_bucket]
    e_hi = jnp.asarray(_BUCKET_HI, jnp.int32)[tile_bucket]
    return dest, e_lo, e_hi, tile_end[-1:].astype(jnp.int32)


def _dispatch_kernel(dest_ref, hf_ref, hs_in_hbm, hs_hbm, sem):
    del hs_in_hbm
    for r in range(TM):
        pltpu.make_async_copy(hf_ref.at[pl.ds(r, 1), :], hs_hbm.at[pl.ds(dest_ref[0, r], 1), :], sem).start()
    pltpu.make_async_copy(hf_ref, hs_hbm.at[pl.ds(0, TM), :], sem).wait()


def _dispatch(dest, hf, n_tokens):
    rows = _sorted_tiles(n_tokens) * TM
    return pl.pallas_call(
        _dispatch_kernel,
        grid=(n_tokens // TM,),
        in_specs=[pl.BlockSpec((None, 1, TM), lambda i: (i, 0, 0), memory_space=pltpu.SMEM),
                  pl.BlockSpec((TM, D_MODEL), lambda i: (i, 0)),
                  pl.BlockSpec(memory_space=pl.ANY)],
        out_specs=pl.BlockSpec(memory_space=pl.ANY),
        out_shape=jax.ShapeDtypeStruct((rows, D_MODEL), F32),
        scratch_shapes=[pltpu.SemaphoreType.DMA(())],
        input_output_aliases={2: 0},
        compiler_params=_cparams(("arbitrary",), 16),
        name="moe_dispatch",
    )(dest, hf, jnp.zeros((rows, D_MODEL), F32))


def _gather_rows(idx_ref, src_hbm, dst_ref, sem):
    for r in range(TM):
        pltpu.make_async_copy(src_hbm.at[pl.ds(idx_ref[0, r], 1), :], dst_ref.at[pl.ds(r, 1), :], sem).start()


def _gather_wait(src_hbm, dst_ref, sem):
    pltpu.make_async_copy(src_hbm.at[pl.ds(0, TM), :], dst_ref, sem).wait()


def _moe_routed_kernel(elo_ref, ehi_ref, nused_ref, h_ref, wg_lo, wu_lo, wd_lo, wg_hi, wu_hi, wd_hi, y_ref):
    del elo_ref, ehi_ref
    j = pl.program_id(0)
    n_used = nused_ref[0]

    @pl.when(j < n_used)
    def _():
        h = h_ref[...].astype(BF16)

        def expert(wg_ref, wu_ref, wd_ref):
            hg = jnp.dot(h, wg_ref[...].astype(BF16), preferred_element_type=F32)
            hu = jnp.dot(h, wu_ref[...].astype(BF16), preferred_element_type=F32)
            act = (hg * _sigmoid(hg)) * hu
            return jnp.dot(act.astype(BF16), wd_ref[...].astype(BF16), preferred_element_type=F32)

        y_ref[:, :D_MODEL] = expert(wg_lo, wu_lo, wd_lo)
        y_ref[:, D_MODEL:] = expert(wg_hi, wu_hi, wd_hi)

    @pl.when(j >= n_used)
    def _():
        y_ref[...] = jnp.zeros(y_ref.shape, F32)


def _moe_routed(hs, e_lo, e_hi, n_used, wg, wu, wd, layer):
    n_tiles = hs.shape[0] // TM
    up_spec = lambda tbl: pl.BlockSpec((None, None, D_MODEL, D_EXPERT),
                                       lambda j, lo, hi, nu: (layer, (lo, hi)[tbl][j], 0, 0))
    dn_spec = lambda tbl: pl.BlockSpec((None, None, D_EXPERT, D_MODEL),
                                       lambda j, lo, hi, nu: (layer, (lo, hi)[tbl][j], 0, 0))
    grid_spec = pltpu.PrefetchScalarGridSpec(
        num_scalar_prefetch=3,
        grid=(n_tiles,),
        in_specs=[pl.BlockSpec((TM, D_MODEL), lambda j, lo, hi, nu: (j, 0)),
                  up_spec(0), up_spec(0), dn_spec(0), up_spec(1), up_spec(1), dn_spec(1)],
        out_specs=pl.BlockSpec((TM, 2 * D_MODEL), lambda j, lo, hi, nu: (j, 0)),
    )
    return pl.pallas_call(
        _moe_routed_kernel,
        grid_spec=grid_spec,
        out_shape=jax.ShapeDtypeStruct((n_tiles * TM, 2 * D_MODEL), F32),
        compiler_params=_cparams(("arbitrary",), 56),
        name="moe_routed",
    )(e_lo, e_hi, n_used, hs, wg, wu, wd, wg, wu, wd)


def _combine_kernel(dest_ref, dest_next_ref, y_hbm, x_ref, meta_ref, mod_ref, o_ref, buf_ref, sems):
    step = pl.program_id(0) * pl.num_programs(1) + pl.program_id(1)
    n_steps = pl.num_programs(0) * pl.num_programs(1)
    slot = step % 2

    @pl.when(step == 0)
    def _():
        _gather_rows(dest_ref, y_hbm, buf_ref.at[0], sems.at[0])

    @pl.when(step + 1 < n_steps)
    def _():
        _gather_rows(dest_next_ref, y_hbm, buf_ref.at[1 - slot], sems.at[1 - slot])

    _gather_wait(y_hbm, buf_ref.at[slot], sems.at[slot])
    meta = meta_ref[...]
    moe = (meta[:, META_W_LO:META_W_LO + 1] * buf_ref[slot, :, :D_MODEL]
           + meta[:, META_W_HI:META_W_HI + 1] * buf_ref[slot, :, D_MODEL:])
    o_ref[...] = x_ref[...] + mod_ref[5:6, :] * moe


def _combine(dest, y, x1, meta, mod, *, tiles_per_batch):
    bsz = x1.shape[0]
    n_steps = bsz * tiles_per_batch
    tok_spec = lambda width: pl.BlockSpec((None, TM, width), lambda b, i: (b, i, 0))
    idx_spec = lambda ahead: pl.BlockSpec(
        (None, 1, TM), lambda b, i: (jnp.minimum(b * tiles_per_batch + i + ahead, n_steps - 1), 0, 0),
        memory_space=pltpu.SMEM)
    return pl.pallas_call(
        _combine_kernel,
        grid=(bsz, tiles_per_batch),
        in_specs=[idx_spec(0), idx_spec(1), pl.BlockSpec(memory_space=pl.ANY), tok_spec(D_MODEL),
                  tok_spec(LANES), pl.BlockSpec((None, SUBLANES, D_MODEL), _mod_index)],
        out_specs=tok_spec(D_MODEL),
        out_shape=jax.ShapeDtypeStruct(x1.shape, F32),
        scratch_shapes=[pltpu.VMEM((2, TM, 2 * D_MODEL), F32), pltpu.SemaphoreType.DMA((2,))],
        compiler_params=_cparams(("arbitrary", "arbitrary"), 32),
        name="moe_combine",
    )(dest, dest, y, x1, meta, mod)


def _moe(hf, meta, rt, counts, x1, mod, wg, wu, wd, layer, *, tiles_per_batch):
    n_tokens = hf.shape[0] * hf.shape[1]
    dest, e_lo, e_hi, n_used = _route_plan(rt, counts, n_tokens)
    hs = _dispatch(dest, hf.reshape(n_tokens, D_MODEL), n_tokens)
    y = _moe_routed(hs, e_lo, e_hi, n_used, wg, wu, wd, layer)
    return _combine(dest, y, x1, meta, mod, tiles_per_batch=tiles_per_batch)


def _inproj_odd_kernel(x_ref, mod_ref, g_ref, w_ref, y_ref, u_ref):
    h = _norm_mod(x_ref[...], g_ref[...], mod_ref[0:1, :], mod_ref[1:2, :])
    z = jnp.dot(h.astype(BF16), w_ref[...], preferred_element_type=F32)
    zy = z[:, :LRU_WIDTH]
    c0 = math.sqrt(2.0 / math.pi)
    y_ref[...] = 0.5 * zy * (1.0 + jnp.tanh(c0 * (zy + 0.044715 * (zy * zy * zy))))
    u_ref[...] = z[:, LRU_WIDTH:]


def _inproj_odd(xs, mod, g, w_in):
    bsz = xs.shape[0]
    tok_spec = lambda width: pl.BlockSpec((None, TM, width), lambda b, i: (b, i, 0))
    const = lambda shape: pl.BlockSpec(shape, lambda b, i: (0,) * len(shape))
    return pl.pallas_call(
        _inproj_odd_kernel,
        grid=(bsz, N_TILES),
        in_specs=[tok_spec(D_MODEL), pl.BlockSpec((None, SUBLANES, D_MODEL), _mod_index),
                  const((1, D_MODEL)), const((D_MODEL, 2 * LRU_WIDTH))],
        out_specs=[tok_spec(LRU_WIDTH), tok_spec(LRU_WIDTH)],
        out_shape=[jax.ShapeDtypeStruct((bsz, TOK, LRU_WIDTH), F32),
                   jax.ShapeDtypeStruct((bsz, TOK, LRU_WIDTH), F32)],
        compiler_params=_cparams(("parallel", "arbitrary"), 48),
        name="inproj_odd",
    )(xs, mod, g, w_in)


def _scan_kernel(u_ref, cw_ref, cbias_ref, wa_ref, ba_ref, wx_ref, bx_ref, lam_ref, h_ref,
                 halo_ref, carry_ref, a_ref, b_ref, *, reverse):
    i = pl.program_id(1)
    tt, w = u_ref.shape
    n_groups = tt // SUBLANES

    @pl.when(i <= 1)
    def _():
        halo_ref[...] = jnp.zeros(halo_ref.shape, F32)

    @pl.when(i == 0)
    def _():
        carry_ref[...] = jnp.zeros(carry_ref.shape, F32)

    u = u_ref[...]
    halo = halo_ref[...]
    row8 = lax.broadcasted_iota(jnp.int32, (SUBLANES, w), 0)
    k_self = 0 if reverse else LRU_CONV_K - 1
    uc = cbias_ref[...] + cw_ref[k_self:k_self + 1, :] * u
    for k in range(1, LRU_CONV_K):
        if reverse:
            tmp = pltpu.roll(u, tt - k, axis=0)
            hr = pltpu.roll(halo, SUBLANES - k, axis=0)
            edge = jnp.where(row8 >= SUBLANES - k, hr, tmp[tt - SUBLANES:, :])
            shifted = jnp.concatenate([tmp[:tt - SUBLANES, :], edge], axis=0)
            wk = cw_ref[k:k + 1, :]
        else:
            tmp = pltpu.roll(u, k, axis=0)
            hr = pltpu.roll(halo, k, axis=0)
            edge = jnp.where(row8 < k, hr, tmp[:SUBLANES, :])
            shifted = jnp.concatenate([edge, tmp[SUBLANES:, :]], axis=0)
            wk = cw_ref[LRU_CONV_K - 1 - k:LRU_CONV_K - k, :]
        uc = uc + wk * shifted
    halo_ref[...] = u[:SUBLANES, :] if reverse else u[tt - SUBLANES:, :]

    ucb = uc.astype(BF16)

    def block_diag(w_blocks):
        return jnp.concatenate(
            [jnp.dot(ucb[:, j * LRU_BLOCK:(j + 1) * LRU_BLOCK], w_blocks[j], preferred_element_type=F32)
             for j in range(LRU_BLOCKS)], axis=1)

    r = _sigmoid(block_diag(wa_ref) + ba_ref[...])
    gate_i = _sigmoid(block_diag(wx_ref) + bx_ref[...])
    neg_lam = -lam_ref[...]
    softplus = jnp.maximum(neg_lam, 0.0) + jnp.log1p(jnp.exp(-jnp.abs(neg_lam)))
    a = jnp.exp((-LRU_C * softplus) * r)
    a_ref[...] = a
    one_m_a2 = 1.0 - a * a
    root = jnp.where(one_m_a2 > 0.0, one_m_a2 * lax.rsqrt(one_m_a2), 0.0)
    b_ref[...] = root * (gate_i * uc)

    def group(gi, carry):
        g = n_groups - 1 - gi if reverse else gi
        off = pl.multiple_of(g * SUBLANES, SUBLANES)
        av = a_ref[pl.ds(off, SUBLANES), :]
        bv = b_ref[pl.ds(off, SUBLANES), :]
        for s in (1, 2, 4):
            if reverse:
                outside = row8 >= SUBLANES - s
                shift = SUBLANES - s
            else:
                outside = row8 < s
                shift = s
            a_sh = jnp.where(outside, 1.0, pltpu.roll(av, shift, axis=0))
            b_sh = jnp.where(outside, 0.0, pltpu.roll(bv, shift, axis=0))
            bv = av * b_sh + bv
            av = av * a_sh
        h = av * carry + bv
        h_ref[pl.ds(off, SUBLANES), :] = h
        last = h[0:1, :] if reverse else h[SUBLANES - 1:SUBLANES, :]
        return jnp.broadcast_to(last, (SUBLANES, w))

    carry_ref[...] = lax.fori_loop(0, n_groups, group, carry_ref[...])


def _rglru_scan(u, conv_w, conv_b, w_a, b_a, w_x, b_x, lam, *, reverse):
    bsz = u.shape[0]
    if reverse:
        tile_of = lambda i: N_LAT_TILES - i
    else:
        tile_of = lambda i: (i + N_LAT_TILES) % N_TILES
    tok_spec = pl.BlockSpec((None, TM, LRU_WIDTH), lambda b, i: (b, tile_of(i), 0))
    const = lambda shape: pl.BlockSpec(shape, lambda b, i: (0,) * len(shape))
    return pl.pallas_call(
        functools.partial(_scan_kernel, reverse=reverse),
        grid=(bsz, N_TILES),
        in_specs=[tok_spec, const((SUBLANES, LRU_WIDTH)), const((1, LRU_WIDTH)),
                  const((LRU_BLOCKS, LRU_BLOCK, LRU_BLOCK)), const((1, LRU_WIDTH)),
                  const((LRU_BLOCKS, LRU_BLOCK, LRU_BLOCK)), const((1, LRU_WIDTH)), const((1, LRU_WIDTH))],
        out_specs=tok_spec,
        out_shape=jax.ShapeDtypeStruct((bsz, TOK, LRU_WIDTH), F32),
        scratch_shapes=[pltpu.VMEM((SUBLANES, LRU_WIDTH), F32), pltpu.VMEM((SUBLANES, LRU_WIDTH), F32),
                        pltpu.VMEM((TM, LRU_WIDTH), F32), pltpu.VMEM((TM, LRU_WIDTH), F32)],
        compiler_params=_cparams(("parallel", "arbitrary"), 32),
        name="rglru_rev" if reverse else "rglru_fwd",
    )(u, conv_w, conv_b, w_a, b_a, w_x, b_x, lam)


def _pad_rows(a, rows):
    return jnp.pad(a, ((0, rows - a.shape[0]), (0, 0)))


def _rope_tables():
    t = jnp.arange(SEQ)
    n_freq = DA_HEAD_DIM // 4
    inv = ROPE_BASE ** (-jnp.arange(n_freq, dtype=F32) / n_freq)
    ang = jnp.concatenate([(t // GRID_W).astype(F32)[:, None] * inv,
                           (t % GRID_W).astype(F32)[:, None] * inv], axis=-1)
    cos, sin = jnp.cos(ang), jnp.sin(ang)
    cos64 = jnp.concatenate([cos, cos], axis=-1)
    sin64 = jnp.concatenate([-sin, sin], axis=-1)
    cos_t = jnp.concatenate([jnp.tile(cos64, (1, 2)), jnp.ones((CTX_LEN, LANES), F32)], axis=0)
    sin_t = jnp.concatenate([jnp.tile(sin64, (1, 2)), jnp.zeros((CTX_LEN, LANES), F32)], axis=0)
    return cos_t, sin_t


def _router_params(w_grp, b_grp, w_rt, b_rt):
    wr = jnp.concatenate([w_rt.reshape(D_MODEL, N_EXPERTS), w_grp], axis=1)
    br = jnp.concatenate([b_rt.reshape(N_EXPERTS), b_grp])
    pad = LANES - wr.shape[1]
    wr = jnp.pad(wr, ((0, 0), (0, pad)))
    wr_hi = wr.astype(BF16)
    wr_lo = (wr - wr_hi.astype(F32)).astype(BF16)
    return jnp.stack([wr_hi, wr_lo]), jnp.pad(br, (0, pad)).reshape(1, LANES)


def kernel(x, c, ctx, c_ctx, ada_w, ada_b, norm_mix, norm_ffn, ev_w_in, ev_conv_w, ev_conv_b, ev_q_norm, ev_k_norm, ev_lam_q1, ev_lam_k1, ev_lam_q2, ev_lam_k2, ev_sub_norm, ev_w_out, od_w_in, od_conv_w, od_conv_b, od_w_a, od_b_a, od_w_x, od_b_x, od_lam, od_w_out, moe_w_grp, moe_b_grp, moe_w_rt, moe_b_rt, moe_w_gate, moe_w_up, moe_w_down):
    bsz = x.shape[0]
    assert x.shape == (bsz, SEQ, D_MODEL) and ctx.shape == (bsz, CTX_LEN, D_MODEL) and bsz == 2
    depth = ada_w.shape[0]
    assert depth == 2


    cvec = _pad_rows(jnp.stack([c[0], c_ctx, c[1], c_ctx]), SUBLANES)
    mod_all = _ada_mod(cvec, ada_w, ada_b)
    mod_all = mod_all[:, :2 * bsz].reshape(depth, 2 * bsz, 6, D_MODEL)
    mod_all = jnp.pad(mod_all, ((0, 0), (0, 0), (0, SUBLANES - 6), (0, 0)))

    l = 0
    lam_init = 0.8 - 0.6 * math.exp(-0.3 * l)
    mod = mod_all[l]
    cos_t, sin_t = _rope_tables()
    blk = jnp.arange(DA_WIDTH) // DA_HEAD_DIM
    bd = jnp.where(blk[:, None] == blk[None, :], 1.0 / DA_HEAD_DIM, 0.0).astype(BF16)
    n_rep = DA_WIDTH // DA_HEAD_DIM
    cb, p, q, k, v = _inproj_even(
        x, ctx, mod, norm_mix[l].reshape(1, D_MODEL), ev_w_in[0].astype(BF16),
        jnp.tile(ev_q_norm[0], n_rep).reshape(1, DA_WIDTH), jnp.tile(ev_k_norm[0], n_rep).reshape(1, DA_WIDTH),
        cos_t, sin_t, bd)

    lamv = _pad_rows(jnp.pad(jnp.stack([ev_lam_q1[0], ev_lam_k1[0], ev_lam_q2[0], ev_lam_k2[0]]),
                             ((0, 0), (0, LANES - DA_HEAD_DIM))), SUBLANES)
    subg = ev_sub_norm[0].reshape(1, DA_V_DIM)
    o_lat = _attention(q, k, v, lamv, subg, tq=256, q_start=0, n_q=SEQ // 256,
                       kv_start=0, kv_len=TOK, tk=768, lam_init=lam_init)
    o_ctx = _attention(q, k, v, lamv, subg, tq=TM, q_start=SEQ, n_q=CTX_LEN // TM,
                       kv_start=SEQ, kv_len=CTX_LEN, tk=CTX_LEN, lam_init=lam_init)

    wr, br = _router_params(moe_w_grp[l], moe_b_grp[l], moe_w_rt[l], moe_b_rt[l])
    x1, hf, meta, rt, counts = _post_even(
        cb, p, _pad_rows(ev_conv_w[0], SUBLANES), ev_conv_b[0].reshape(1, CONV_WIDTH), o_lat, o_ctx,
        ev_w_out[0].astype(BF16), x, ctx, mod, norm_ffn[l].reshape(1, D_MODEL), wr, br)
    xs = _moe(hf, meta, rt, counts, x1, mod, moe_w_gate, moe_w_up, moe_w_down, l, tiles_per_batch=N_TILES)

    l = 1
    mod = mod_all[l]
    y, u = _inproj_odd(xs, mod, norm_mix[l].reshape(1, D_MODEL), od_w_in[0].astype(BF16))
    h_dirs = []
    for d in range(2):
        h_dirs.append(_rglru_scan(
            u, _pad_rows(od_conv_w[0, d], SUBLANES), od_conv_b[0, d].reshape(1, LRU_WIDTH),
            od_w_a[0, d].astype(BF16), od_b_a[0, d].reshape(1, LRU_WIDTH),
            od_w_x[0, d].astype(BF16), od_b_x[0, d].reshape(1, LRU_WIDTH),
            od_lam[0, d].reshape(1, LRU_WIDTH), reverse=bool(d)))
    wr, br = _router_params(moe_w_grp[l], moe_b_grp[l], moe_w_rt[l], moe_b_rt[l])
    x1, hf, meta, rt, counts = _post_odd(y, h_dirs[0], h_dirs[1], od_w_out[0].astype(BF16), xs, mod,
                                     norm_ffn[l].reshape(1, D_MODEL), wr, br)
    return _moe(hf, meta, rt, counts, x1, mod, moe_w_gate, moe_w_up, moe_w_down, l,
                tiles_per_batch=N_LAT_TILES)
```

```python
import functools
import math

import jax
import jax.numpy as jnp
from jax import lax
from jax.experimental import pallas as pl
from jax.experimental.pallas import tpu as pltpu

F32 = jnp.float32
BF16 = jnp.bfloat16
HIGHEST = lax.Precision.HIGHEST

D_MODEL = 1024
SEQ = 8192
CTX_LEN = 256
TOK = SEQ + CTX_LEN
GRID_W = 64
EPS = 1e-6

CONV_WIDTH = 512
DA_HEADS = 4
DA_HEAD_DIM = 64
DA_V_DIM = 2 * DA_HEAD_DIM
DA_WIDTH = DA_HEADS * DA_V_DIM
EVEN_IN = 3 * CONV_WIDTH + 3 * DA_WIDTH
ROPE_BASE = 10000.0

LRU_WIDTH = 1024
LRU_BLOCKS = 8
LRU_BLOCK = LRU_WIDTH // LRU_BLOCKS
LRU_CONV_K = 4
LRU_C = 8.0

N_GROUPS = 4
EXPERTS_PER_GROUP = 4
N_EXPERTS = N_GROUPS * EXPERTS_PER_GROUP
D_EXPERT = 512

LANES = 128
SUBLANES = 8
TM = 256
N_LAT_TILES = SEQ // TM
N_TILES = TOK // TM
NEG = -1e30
MIB = 2 ** 20


def _cparams(semantics, vmem_mib):
    return pltpu.CompilerParams(dimension_semantics=semantics, vmem_limit_bytes=vmem_mib * MIB)


def _sigmoid(x):
    return 0.5 * jnp.tanh(0.5 * x) + 0.5


def _norm_mod(x, g, shift, scale):
    ms = jnp.mean(x * x, axis=-1, keepdims=True)
    return (x * lax.rsqrt(ms + EPS) * g) * (1.0 + scale) + shift


ADA_TN = 1536


def _ada_kernel(c_ref, w_ref, b_ref, o_ref):
    c = c_ref[...]
    a = c * _sigmoid(c)
    o_ref[...] = jnp.dot(a, w_ref[...], precision=HIGHEST, preferred_element_type=F32) + b_ref[...]


def _ada_mod(cvec, ada_w, ada_b):
    depth, d, n = ada_w.shape
    return pl.pallas_call(
        _ada_kernel,
        grid=(depth, n // ADA_TN),
        in_specs=[
            pl.BlockSpec((SUBLANES, d), lambda l, j: (0, 0)),
            pl.BlockSpec((None, d, ADA_TN), lambda l, j: (l, 0, j)),
            pl.BlockSpec((None, 1, ADA_TN), lambda l, j: (l, 0, j)),
        ],
        out_specs=pl.BlockSpec((None, SUBLANES, ADA_TN), lambda l, j: (l, 0, j)),
        out_shape=jax.ShapeDtypeStruct((depth, SUBLANES, n), F32),
        compiler_params=_cparams(("arbitrary", "arbitrary"), 40),
        name="ada_mod",
    )(cvec, ada_w, ada_b.reshape(depth, 1, n))


def _mod_index(b, i):
    return (b * 2 + i // N_LAT_TILES, 0, 0)


def _inproj_even_kernel(xl_ref, xc_ref, mod_ref, g_ref, w_ref, qg_ref, kg_ref, cos_ref, sin_ref, bd_ref,
                        cb_ref, p_ref, q_ref, k_ref, v_ref):
    x = jnp.where(pl.program_id(1) == N_LAT_TILES, xc_ref[...], xl_ref[...])
    h = _norm_mod(x, g_ref[...], mod_ref[0:1, :], mod_ref[1:2, :])
    z = jnp.dot(h.astype(BF16), w_ref[...], preferred_element_type=F32)
    cw = CONV_WIDTH
    cb_ref[...] = z[:, :cw]
    p_ref[...] = z[:, cw:2 * cw] * z[:, 2 * cw:3 * cw]

    reps = DA_WIDTH // LANES
    cosf = jnp.concatenate([cos_ref[...]] * reps, axis=1)
    sinf = jnp.concatenate([sin_ref[...]] * reps, axis=1)
    lane = lax.broadcasted_iota(jnp.int32, (TM, DA_WIDTH), 1)
    first_half = (lane & (DA_HEAD_DIM - 1)) < DA_HEAD_DIM // 2
    bd = bd_ref[...]

    def head_norm_rope(t, gain):
        ms = jnp.dot((t * t).astype(BF16), bd, preferred_element_type=F32)
        y = t * lax.rsqrt(ms + EPS) * gain
        fwd = pltpu.roll(y, DA_WIDTH - DA_HEAD_DIM // 2, axis=1)
        bwd = pltpu.roll(y, DA_HEAD_DIM // 2, axis=1)
        return y * cosf + jnp.where(first_half, fwd, bwd) * sinf

    base = 3 * cw
    q = head_norm_rope(z[:, base:base + DA_WIDTH], qg_ref[...])
    q_ref[...] = (q * (DA_HEAD_DIM ** -0.5 * math.log2(math.e))).astype(BF16)
    k = head_norm_rope(z[:, base + DA_WIDTH:base + 2 * DA_WIDTH], kg_ref[...])
    k_ref[...] = k.astype(BF16)
    v = z[:, base + 2 * DA_WIDTH:].astype(BF16)
    ones = jnp.ones((TM, DA_V_DIM), BF16)
    v_ref[...] = jnp.concatenate(
        [blk for h in range(DA_HEADS) for blk in (v[:, h * DA_V_DIM:(h + 1) * DA_V_DIM], ones)], axis=1)


def _lat_ctx_specs(width):
    return [pl.BlockSpec((None, TM, width), lambda b, i: (b, jnp.minimum(i, N_LAT_TILES - 1), 0)),
            pl.BlockSpec((None, TM, width), lambda b, i: (b, 0, 0))]


def _inproj_even(x, ctx, mod, g, w_in, qg, kg, cos, sin, bd):
    bsz = x.shape[0]
    tok_spec = lambda width: pl.BlockSpec((None, TM, width), lambda b, i: (b, i, 0))
    const = lambda shape: pl.BlockSpec(shape, lambda b, i: (0,) * len(shape))
    return pl.pallas_call(
        _inproj_even_kernel,
        grid=(bsz, N_TILES),
        in_specs=_lat_ctx_specs(D_MODEL) + [
            pl.BlockSpec((None, SUBLANES, D_MODEL), _mod_index),
            const((1, D_MODEL)),
            const((D_MODEL, EVEN_IN)),
            const((1, DA_WIDTH)),
            const((1, DA_WIDTH)),
            pl.BlockSpec((TM, LANES), lambda b, i: (i, 0)),
            pl.BlockSpec((TM, LANES), lambda b, i: (i, 0)),
            const((DA_WIDTH, DA_WIDTH)),
        ],
        out_specs=[tok_spec(CONV_WIDTH), tok_spec(CONV_WIDTH), tok_spec(DA_WIDTH), tok_spec(DA_WIDTH),
                   tok_spec(2 * DA_WIDTH)],
        out_shape=[
            jax.ShapeDtypeStruct((bsz, TOK, CONV_WIDTH), F32),
            jax.ShapeDtypeStruct((bsz, TOK, CONV_WIDTH), F32),
            jax.ShapeDtypeStruct((bsz, TOK, DA_WIDTH), BF16),
            jax.ShapeDtypeStruct((bsz, TOK, DA_WIDTH), BF16),
            jax.ShapeDtypeStruct((bsz, TOK, 2 * DA_WIDTH), BF16),
        ],
        compiler_params=_cparams(("parallel", "arbitrary"), 48),
        name="inproj_even",
    )(x, ctx, mod, g, w_in, qg, kg, cos, sin, bd)


def _attn_kernel(lam_ref, subg_ref, q_ref, k_ref, v_ref, o_ref, qz_ref, m_ref, acc_ref,
                 sa_ref, sb_ref, *, tq, tk, nk, lam_init):
    q = q_ref[...]
    lane = lax.broadcasted_iota(jnp.int32, (tq, LANES), 1)
    zero = jnp.zeros_like(q)
    qz_ref[0:tq, :] = jnp.where(lane < DA_HEAD_DIM, q, zero)
    qz_ref[tq:2 * tq, :] = jnp.where(lane >= DA_HEAD_DIM, q, zero)
    m_ref[...] = jnp.full(m_ref.shape, NEG, F32)
    acc_ref[...] = jnp.zeros(acc_ref.shape, F32)

    def scores(j, dst_ref):
        off = pl.multiple_of(j * tk, tk)
        dst_ref[...] = lax.dot_general(qz_ref[...], k_ref[pl.ds(off, tk), :], (((1,), (1,)), ((), ())),
                                       preferred_element_type=F32)

    def update(j, src_ref):
        off = pl.multiple_of(j * tk, tk)
        s = src_ref[...]
        m_prev = m_ref[...]
        m_new = jnp.maximum(m_prev, jnp.max(s, axis=1, keepdims=True))
        alpha = jnp.exp2(m_prev - m_new)
        p = jnp.exp2((s - m_new[:, :1]).astype(BF16))
        pv = jnp.dot(p, v_ref[pl.ds(off, tk), :], preferred_element_type=F32)
        acc_ref[:, :DA_V_DIM] = alpha * acc_ref[:, :DA_V_DIM] + pv[:, :DA_V_DIM]
        acc_ref[:, DA_V_DIM:] = alpha * acc_ref[:, DA_V_DIM:] + pv[:, DA_V_DIM:]
        m_ref[...] = m_new

    bufs = (sa_ref, sb_ref)
    scores(0, bufs[0])
    for j in range(nk):
        if j + 1 < nk:
            scores(j + 1, bufs[(j + 1) % 2])
        update(j, bufs[j % 2])

    o = acc_ref[:, :DA_V_DIM] / acc_ref[:, DA_V_DIM:]
    lv = lam_ref[...]
    lam = (jnp.exp(jnp.sum(lv[0:1, :] * lv[1:2, :], axis=1, keepdims=True))
           - jnp.exp(jnp.sum(lv[2:3, :] * lv[3:4, :], axis=1, keepdims=True)) + lam_init)
    d = o[:tq] - lam * o[tq:]
    ms = jnp.mean(d * d, axis=1, keepdims=True)
    o_ref[...] = (d * lax.rsqrt(ms + EPS) * subg_ref[...] * (1.0 - lam_init)).astype(BF16)


def _attention(q, k, v, lamv, subg, *, tq, q_start, n_q, kv_start, kv_len, tk, lam_init):
    bsz = q.shape[0]
    q_blk0 = q_start // tq
    kv_blk = kv_start // kv_len
    q_spec = pl.BlockSpec((None, tq, LANES), lambda b, h, i: (b, q_blk0 + i, h))
    o_spec = pl.BlockSpec((None, tq, LANES), lambda b, h, i: (b, i, h))
    k_spec = pl.BlockSpec((None, kv_len, LANES), lambda b, h, i: (b, kv_blk, h))
    v_spec = pl.BlockSpec((None, kv_len, 2 * DA_V_DIM), lambda b, h, i: (b, kv_blk, h))
    const = lambda shape: pl.BlockSpec(shape, lambda b, h, i: (0,) * len(shape))
    return pl.pallas_call(
        functools.partial(_attn_kernel, tq=tq, tk=tk, nk=kv_len // tk, lam_init=lam_init),
        grid=(bsz, DA_HEADS, n_q),
        in_specs=[const((SUBLANES, LANES)), const((1, LANES)), q_spec, k_spec, v_spec],
        out_specs=o_spec,
        out_shape=jax.ShapeDtypeStruct((bsz, n_q * tq, DA_WIDTH), BF16),
        scratch_shapes=[
            pltpu.VMEM((2 * tq, LANES), BF16),
            pltpu.VMEM((2 * tq, LANES), F32),
            pltpu.VMEM((2 * tq, 2 * DA_V_DIM), F32),
            pltpu.VMEM((2 * tq, tk), F32),
            pltpu.VMEM((2 * tq, tk), F32),
        ],
        compiler_params=_cparams(("parallel", "parallel", "arbitrary"), 48),
        name="diff_attn",
    )(lamv, subg, q, k, v)


PAIRS_PER_GROUP = EXPERTS_PER_GROUP * (EXPERTS_PER_GROUP - 1) // 2
N_BUCKETS = N_GROUPS * PAIRS_PER_GROUP
BUCKET_ROWS = -(-N_BUCKETS // SUBLANES) * SUBLANES
META_BUCKET, META_RANK, META_W_LO, META_W_HI = 0, 1, 2, 3


def _post_tail(y, first_step, x, mod_ref, g_ref, wr_ref, br_ref, x1_ref, hf_ref, meta_ref, rt_ref, cnt_ref,
               cnt_scr):
    x1 = x + mod_ref[2:3, :] * y
    x1_ref[...] = x1
    hf = _norm_mod(x1, g_ref[...], mod_ref[3:4, :], mod_ref[4:5, :])
    hf_ref[...] = hf

    hf_hi = hf.astype(BF16)
    hf_lo = (hf - hf_hi.astype(F32)).astype(BF16)
    logits = (jnp.dot(hf_hi, wr_ref[0], preferred_element_type=F32)
              + jnp.dot(hf_lo, wr_ref[0], preferred_element_type=F32)
              + jnp.dot(hf_hi, wr_ref[1], preferred_element_type=F32)) + br_ref[...]
    tm = logits.shape[0]
    lt = jnp.transpose(logits)
    big = float(LANES)
    e_all = lt[0:N_EXPERTS, :]
    g = lt[N_EXPERTS:N_EXPERTS + N_GROUPS, :]
    row_g = lax.broadcasted_iota(jnp.int32, (N_GROUPS, tm), 0).astype(F32)
    gm = jnp.max(g, axis=0, keepdims=True)
    g_idx = jnp.min(jnp.where(g == gm, row_g, big), axis=0, keepdims=True)
    p_sel = 1.0 / jnp.sum(jnp.exp(g - gm), axis=0, keepdims=True)
    el = e_all[0:EXPERTS_PER_GROUP, :]
    for grp in range(1, N_GROUPS):
        el = jnp.where(g_idx == float(grp), e_all[grp * EXPERTS_PER_GROUP:(grp + 1) * EXPERTS_PER_GROUP, :], el)
    row_e = lax.broadcasted_iota(jnp.int32, (EXPERTS_PER_GROUP, tm), 0).astype(F32)
    v1 = jnp.max(el, axis=0, keepdims=True)
    i1 = jnp.min(jnp.where(el == v1, row_e, big), axis=0, keepdims=True)
    el2 = jnp.where(row_e == i1, NEG, el)
    v2 = jnp.max(el2, axis=0, keepdims=True)
    i2 = jnp.min(jnp.where(el2 == v2, row_e, big), axis=0, keepdims=True)
    t = jnp.exp(v2 - v1)
    w1 = p_sel / (1.0 + t)
    w2 = t * w1

    first_lower = i1 < i2
    a = jnp.minimum(i1, i2)
    b = jnp.maximum(i1, i2)
    code = a * EXPERTS_PER_GROUP + b
    pair = jnp.where(code == 1.0, 0.0, jnp.where(code == 6.0, 1.0, jnp.where(code == 2.0, 2.0, jnp.where(
        code == 3.0, 3.0, jnp.where(code == 7.0, 4.0, 5.0)))))
    bucket = g_idx * PAIRS_PER_GROUP + pair
    w_low = jnp.where(first_lower, w1, w2)
    w_high = jnp.where(first_lower, w2, w1)
    w_lo = jnp.where(code == 1.0, w_low, w_high)
    w_hi = jnp.where(code == 1.0, w_high, w_low)

    @pl.when(first_step)
    def _():
        cnt_scr[...] = jnp.zeros(cnt_scr.shape, F32)

    row_b = lax.broadcasted_iota(jnp.int32, (BUCKET_ROWS, tm), 0).astype(F32)
    onehot = jnp.where(row_b == bucket, 1.0, 0.0)
    r_i = lax.broadcasted_iota(jnp.int32, (tm, tm), 0)
    c_i = lax.broadcasted_iota(jnp.int32, (tm, tm), 1)
    earlier = jnp.where(r_i < c_i, 1.0, 0.0).astype(BF16)
    prefix = jnp.dot(onehot.astype(BF16), earlier, preferred_element_type=F32)
    counts = cnt_scr[...]
    rank = jnp.sum(onehot * (prefix + counts[:, 0:1]), axis=0, keepdims=True)
    counts = counts + jnp.sum(onehot, axis=1, keepdims=True)
    cnt_scr[...] = counts
    cnt_ref[...] = counts

    def record(rows):
        row = lax.broadcasted_iota(jnp.int32, (rows, tm), 0)
        return jnp.where(row == META_BUCKET, bucket, jnp.where(row == META_RANK, rank, jnp.where(
            row == META_W_LO, w_lo, jnp.where(row == META_W_HI, w_hi, 0.0))))

    rt_ref[...] = record(SUBLANES)
    meta_ref[...] = jnp.transpose(record(LANES))


def _post_even_kernel(cb_ref, p_ref, pprev_ref, pnext_ref, cw_ref, cbias_ref, olat_ref, octx_ref, w_ref,
                      xl_ref, xc_ref, mod_ref, g_ref, wr_ref, br_ref, x1_ref, hf_ref, meta_ref, rt_ref, cnt_ref, cnt_scr):
    i = pl.program_id(1)
    first_step = jnp.logical_and(pl.program_id(0) == 0, i == 0)
    pc = p_ref[...]
    row = lax.broadcasted_iota(jnp.int32, pc.shape, 0)
    has_prev = jnp.logical_and(i != 0, i != N_LAT_TILES)
    has_next = i < N_LAT_TILES - 1
    prev_row = jnp.where(has_prev, pprev_ref[SUBLANES - 1:SUBLANES, :], 0.0)
    next_row = jnp.where(has_next, pnext_ref[0:1, :], 0.0)
    up = jnp.where(row == 0, prev_row, pltpu.roll(pc, 1, axis=0))
    dn = jnp.where(row == TM - 1, next_row, pltpu.roll(pc, TM - 1, axis=0))
    conv = cbias_ref[...] + cw_ref[0:1, :] * up + cw_ref[1:2, :] * pc + cw_ref[2:3, :] * dn
    out_a = (cb_ref[...] * conv).astype(BF16)
    o = jnp.where(i == N_LAT_TILES, octx_ref[...], olat_ref[...])
    x = jnp.where(i == N_LAT_TILES, xc_ref[...], xl_ref[...])
    y = (jnp.dot(out_a, w_ref[0:CONV_WIDTH, :], preferred_element_type=F32)
         + jnp.dot(o, w_ref[CONV_WIDTH:, :], preferred_element_type=F32))
    _post_tail(y, first_step, x, mod_ref, g_ref, wr_ref, br_ref, x1_ref, hf_ref, meta_ref, rt_ref, cnt_ref,
               cnt_scr)


def _post_odd_kernel(y_ref, hf_in_ref, hb_in_ref, w_ref, x_ref, mod_ref, g_ref, wr_ref, br_ref,
                     x1_ref, hf_ref, meta_ref, rt_ref, cnt_ref, cnt_scr):
    first_step = jnp.logical_and(pl.program_id(0) == 0, pl.program_id(1) == 0)
    a = (y_ref[...] * (hf_in_ref[...] + hb_in_ref[...])).astype(BF16)
    y = jnp.dot(a, w_ref[...], preferred_element_type=F32)
    _post_tail(y, first_step, x_ref[...], mod_ref, g_ref, wr_ref, br_ref, x1_ref, hf_ref, meta_ref, rt_ref, cnt_ref,
               cnt_scr)


def _post_specs(bsz, rows):
    tok_spec = lambda width: pl.BlockSpec((None, TM, width), lambda b, i: (b, i, 0))
    const = lambda shape: pl.BlockSpec(shape, lambda b, i: (0,) * len(shape))
    tail_in = [pl.BlockSpec((None, SUBLANES, D_MODEL), _mod_index), const((1, D_MODEL)),
               const((2, D_MODEL, LANES)), const((1, LANES))]
    tiles = rows // TM
    out_specs = [tok_spec(D_MODEL), tok_spec(D_MODEL), tok_spec(LANES),
                 pl.BlockSpec((None, SUBLANES, TM), lambda b, i: (b * tiles + i, 0, 0)),
                 const((BUCKET_ROWS, LANES))]
    out_shape = [jax.ShapeDtypeStruct((bsz, rows, D_MODEL), F32),
                 jax.ShapeDtypeStruct((bsz, rows, D_MODEL), F32),
                 jax.ShapeDtypeStruct((bsz, rows, LANES), F32),
                 jax.ShapeDtypeStruct((bsz * tiles, SUBLANES, TM), F32),
                 jax.ShapeDtypeStruct((BUCKET_ROWS, LANES), F32)]
    scratch = [pltpu.VMEM((BUCKET_ROWS, LANES), F32)]
    return tok_spec, const, tail_in, out_specs, out_shape, scratch


def _post_even(cb, p, conv_w, conv_b, o_lat, o_ctx, w_out, x, ctx, mod, g, wr, br):
    bsz = x.shape[0]
    tok_spec, const, tail_in, out_specs, out_shape, scratch = _post_specs(bsz, TOK)
    halo_blocks = TM // SUBLANES
    last_halo = TOK // SUBLANES - 1
    prev_spec = pl.BlockSpec((None, SUBLANES, CONV_WIDTH),
                             lambda b, i: (b, jnp.maximum(i * halo_blocks - 1, 0), 0))
    next_spec = pl.BlockSpec((None, SUBLANES, CONV_WIDTH),
                             lambda b, i: (b, jnp.minimum((i + 1) * halo_blocks, last_halo), 0))
    return pl.pallas_call(
        _post_even_kernel,
        grid=(bsz, N_TILES),
        in_specs=[tok_spec(CONV_WIDTH), tok_spec(CONV_WIDTH), prev_spec, next_spec,
                  const((SUBLANES, CONV_WIDTH)), const((1, CONV_WIDTH))] + _lat_ctx_specs(DA_WIDTH)
        + [const((D_MODEL, D_MODEL))] + _lat_ctx_specs(D_MODEL) + tail_in,
        out_specs=out_specs,
        out_shape=out_shape,
        scratch_shapes=scratch,
        compiler_params=_cparams(("arbitrary", "arbitrary"), 48),
        name="post_even",
    )(cb, p, p, p, conv_w, conv_b, o_lat, o_ctx, w_out, x, ctx, mod, g, wr, br)


def _post_odd(y, hfw, hbw, w_out, xs, mod, g, wr, br):
    bsz = xs.shape[0]
    tok_spec, const, tail_in, out_specs, out_shape, scratch = _post_specs(bsz, SEQ)
    return pl.pallas_call(
        _post_odd_kernel,
        grid=(bsz, N_LAT_TILES),
        in_specs=[tok_spec(LRU_WIDTH), tok_spec(LRU_WIDTH), tok_spec(LRU_WIDTH),
                  const((LRU_WIDTH, D_MODEL)), tok_spec(D_MODEL)] + tail_in,
        out_specs=out_specs,
        out_shape=out_shape,
        scratch_shapes=scratch,
        compiler_params=_cparams(("arbitrary", "arbitrary"), 48),
        name="post_odd",
    )(y, hfw, hbw, w_out, xs, mod, g, wr, br)


_PAIR_SLOTS = [(0, 1), (2, 1), (2, 0), (3, 0), (3, 1), (3, 2)]
_BUCKET_LO = [g * EXPERTS_PER_GROUP + a for g in range(N_GROUPS) for a, _ in _PAIR_SLOTS]
_BUCKET_HI = [g * EXPERTS_PER_GROUP + b for g in range(N_GROUPS) for _, b in _PAIR_SLOTS]


def _sorted_tiles(n_tokens):
    return n_tokens // TM + N_BUCKETS


def _route_plan(rt, counts, n_tokens):
    n_tiles = _sorted_tiles(n_tokens)
    bucket = rt[:, META_BUCKET, :].astype(jnp.int32).reshape(n_tokens)
    rank = rt[:, META_RANK, :].astype(jnp.int32).reshape(n_tokens)
    cnt = counts[:N_BUCKETS, 0].astype(jnp.int32)
    tiles_per = (cnt + TM - 1) // TM
    tile_end = jnp.cumsum(tiles_per)
    row_start = (tile_end - tiles_per) * TM
    dest = (row_start[bucket] + rank).reshape(n_tokens // TM, 1, TM)
    tile_bucket = jnp.minimum(jnp.sum(jnp.arange(n_tiles)[:, None] >= tile_end[None, :], axis=1), N_BUCKETS - 1)
    e_lo = jnp.asarray(_BUCKET_LO, jnp.int32)[tile_bucket]
    e_hi = jnp.asarray(_BUCKET_HI, jnp.int32)[tile_bucket]
    return dest, e_lo, e_hi, tile_end[-1:].astype(jnp.int32)


def _dispatch_kernel(dest_ref, hf_ref, hs_in_hbm, hs_hbm, sem):
    del hs_in_hbm
    for r in range(TM):
        pltpu.make_async_copy(hf_ref.at[pl.ds(r, 1), :], hs_hbm.at[pl.ds(dest_ref[0, r], 1), :], sem).start()
    pltpu.make_async_copy(hf_ref, hs_hbm.at[pl.ds(0, TM), :], sem).wait()


def _dispatch(dest, hf, n_tokens):
    rows = _sorted_tiles(n_tokens) * TM
    return pl.pallas_call(
        _dispatch_kernel,
        grid=(n_tokens // TM,),
        in_specs=[pl.BlockSpec((None, 1, TM), lambda i: (i, 0, 0), memory_space=pltpu.SMEM),
                  pl.BlockSpec((TM, D_MODEL), lambda i: (i, 0)),
                  pl.BlockSpec(memory_space=pl.ANY)],
        out_specs=pl.BlockSpec(memory_space=pl.ANY),
        out_shape=jax.ShapeDtypeStruct((rows, D_MODEL), F32),
        scratch_shapes=[pltpu.SemaphoreType.DMA(())],
        input_output_aliases={2: 0},
        compiler_params=_cparams(("arbitrary",), 16),
        name="moe_dispatch",
    )(dest, hf, jnp.zeros((rows, D_MODEL), F32))


def _gather_rows(idx_ref, src_hbm, dst_ref, sem):
    for r in range(TM):
        pltpu.make_async_copy(src_hbm.at[pl.ds(idx_ref[0, r], 1), :], dst_ref.at[pl.ds(r, 1), :], sem).start()


def _gather_wait(src_hbm, dst_ref, sem):
    pltpu.make_async_copy(src_hbm.at[pl.ds(0, TM), :], dst_ref, sem).wait()


def _moe_routed_kernel(elo_ref, ehi_ref, nused_ref, h_ref, wg_lo, wu_lo, wd_lo, wg_hi, wu_hi, wd_hi, y_ref):
    del elo_ref, ehi_ref
    j = pl.program_id(0)
    n_used = nused_ref[0]

    @pl.when(j < n_used)
    def _():
        h = h_ref[...].astype(BF16)

        def expert(wg_ref, wu_ref, wd_ref):
            hg = jnp.dot(h, wg_ref[...].astype(BF16), preferred_element_type=F32)
            hu = jnp.dot(h, wu_ref[...].astype(BF16), preferred_element_type=F32)
            act = (hg * _sigmoid(hg)) * hu
            return jnp.dot(act.astype(BF16), wd_ref[...].astype(BF16), preferred_element_type=F32)

        y_ref[:, :D_MODEL] = expert(wg_lo, wu_lo, wd_lo)
        y_ref[:, D_MODEL:] = expert(wg_hi, wu_hi, wd_hi)

    @pl.when(j >= n_used)
    def _():
        y_ref[...] = jnp.zeros(y_ref.shape, F32)


def _moe_routed(hs, e_lo, e_hi, n_used, wg, wu, wd, layer):
    n_tiles = hs.shape[0] // TM
    up_spec = lambda tbl: pl.BlockSpec((None, None, D_MODEL, D_EXPERT),
                                       lambda j, lo, hi, nu: (layer, (lo, hi)[tbl][j], 0, 0))
    dn_spec = lambda tbl: pl.BlockSpec((None, None, D_EXPERT, D_MODEL),
                                       lambda j, lo, hi, nu: (layer, (lo, hi)[tbl][j], 0, 0))
    grid_spec = pltpu.PrefetchScalarGridSpec(
        num_scalar_prefetch=3,
        grid=(n_tiles,),
        in_specs=[pl.BlockSpec((TM, D_MODEL), lambda j, lo, hi, nu: (j, 0)),
                  up_spec(0), up_spec(0), dn_spec(0), up_spec(1), up_spec(1), dn_spec(1)],
        out_specs=pl.BlockSpec((TM, 2 * D_MODEL), lambda j, lo, hi, nu: (j, 0)),
    )
    return pl.pallas_call(
        _moe_routed_kernel,
        grid_spec=grid_spec,
        out_shape=jax.ShapeDtypeStruct((n_tiles * TM, 2 * D_MODEL), F32),
        compiler_params=_cparams(("arbitrary",), 56),
        name="moe_routed",
    )(e_lo, e_hi, n_used, hs, wg, wu, wd, wg, wu, wd)


def _combine_kernel(dest_ref, dest_next_ref, y_hbm, x_ref, meta_ref, mod_ref, o_ref, buf_ref, sems):
    step = pl.program_id(0) * pl.num_programs(1) + pl.program_id(1)
    n_steps = pl.num_programs(0) * pl.num_programs(1)
    slot = step % 2

    @pl.when(step == 0)
    def _():
        _gather_rows(dest_ref, y_hbm, buf_ref.at[0], sems.at[0])

    @pl.when(step + 1 < n_steps)
    def _():
        _gather_rows(dest_next_ref, y_hbm, buf_ref.at[1 - slot], sems.at[1 - slot])

    _gather_wait(y_hbm, buf_ref.at[slot], sems.at[slot])
    meta = meta_ref[...]
    moe = (meta[:, META_W_LO:META_W_LO + 1] * buf_ref[slot, :, :D_MODEL]
           + meta[:, META_W_HI:META_W_HI + 1] * buf_ref[slot, :, D_MODEL:])
    o_ref[...] = x_ref[...] + mod_ref[5:6, :] * moe


def _combine(dest, y, x1, meta, mod, *, tiles_per_batch):
    bsz = x1.shape[0]
    n_steps = bsz * tiles_per_batch
    tok_spec = lambda width: pl.BlockSpec((None, TM, width), lambda b, i: (b, i, 0))
    idx_spec = lambda ahead: pl.BlockSpec(
        (None, 1, TM), lambda b, i: (jnp.minimum(b * tiles_per_batch + i + ahead, n_steps - 1), 0, 0),
        memory_space=pltpu.SMEM)
    return pl.pallas_call(
        _combine_kernel,
        grid=(bsz, tiles_per_batch),
        in_specs=[idx_spec(0), idx_spec(1), pl.BlockSpec(memory_space=pl.ANY), tok_spec(D_MODEL),
                  tok_spec(LANES), pl.BlockSpec((None, SUBLANES, D_MODEL), _mod_index)],
        out_specs=tok_spec(D_MODEL),
        out_shape=jax.ShapeDtypeStruct(x1.shape, F32),
        scratch_shapes=[pltpu.VMEM((2, TM, 2 * D_MODEL), F32), pltpu.SemaphoreType.DMA((2,))],
        compiler_params=_cparams(("arbitrary", "arbitrary"), 32),
        name="moe_combine",
    )(dest, dest, y, x1, meta, mod)


def _moe(hf, meta, rt, counts, x1, mod, wg, wu, wd, layer, *, tiles_per_batch):
    n_tokens = hf.shape[0] * hf.shape[1]
    dest, e_lo, e_hi, n_used = _route_plan(rt, counts, n_tokens)
    hs = _dispatch(dest, hf.reshape(n_tokens, D_MODEL), n_tokens)
    y = _moe_routed(hs, e_lo, e_hi, n_used, wg, wu, wd, layer)
    return _combine(dest, y, x1, meta, mod, tiles_per_batch=tiles_per_batch)


def _inproj_odd_kernel(x_ref, mod_ref, g_ref, w_ref, y_ref, u_ref):
    h = _norm_mod(x_ref[...], g_ref[...], mod_ref[0:1, :], mod_ref[1:2, :])
    z = jnp.dot(h.astype(BF16), w_ref[...], preferred_element_type=F32)
    zy = z[:, :LRU_WIDTH]
    c0 = math.sqrt(2.0 / math.pi)
    y_ref[...] = 0.5 * zy * (1.0 + jnp.tanh(c0 * (zy + 0.044715 * (zy * zy * zy))))
    u_ref[...] = z[:, LRU_WIDTH:]


def _inproj_odd(xs, mod, g, w_in):
    bsz = xs.shape[0]
    tok_spec = lambda width: pl.BlockSpec((None, TM, width), lambda b, i: (b, i, 0))
    const = lambda shape: pl.BlockSpec(shape, lambda b, i: (0,) * len(shape))
    return pl.pallas_call(
        _inproj_odd_kernel,
        grid=(bsz, N_TILES),
        in_specs=[tok_spec(D_MODEL), pl.BlockSpec((None, SUBLANES, D_MODEL), _mod_index),
                  const((1, D_MODEL)), const((D_MODEL, 2 * LRU_WIDTH))],
        out_specs=[tok_spec(LRU_WIDTH), tok_spec(LRU_WIDTH)],
        out_shape=[jax.ShapeDtypeStruct((bsz, TOK, LRU_WIDTH), F32),
                   jax.ShapeDtypeStruct((bsz, TOK, LRU_WIDTH), F32)],
        compiler_params=_cparams(("parallel", "arbitrary"), 48),
        name="inproj_odd",
    )(xs, mod, g, w_in)


def _scan_kernel(u_ref, cw_ref, cbias_ref, wa_ref, ba_ref, wx_ref, bx_ref, lam_ref, h_ref,
                 halo_ref, carry_ref, a_ref, b_ref, *, reverse):
    i = pl.program_id(1)
    tt, w = u_ref.shape
    n_groups = tt // SUBLANES

    @pl.when(i <= 1)
    def _():
        halo_ref[...] = jnp.zeros(halo_ref.shape, F32)

    @pl.when(i == 0)
    def _():
        carry_ref[...] = jnp.zeros(carry_ref.shape, F32)

    u = u_ref[...]
    halo = halo_ref[...]
    row8 = lax.broadcasted_iota(jnp.int32, (SUBLANES, w), 0)
    k_self = 0 if reverse else LRU_CONV_K - 1
    uc = cbias_ref[...] + cw_ref[k_self:k_self + 1, :] * u
    for k in range(1, LRU_CONV_K):
        if reverse:
            tmp = pltpu.roll(u, tt - k, axis=0)
            hr = pltpu.roll(halo, SUBLANES - k, axis=0)
            edge = jnp.where(row8 >= SUBLANES - k, hr, tmp[tt - SUBLANES:, :])
            shifted = jnp.concatenate([tmp[:tt - SUBLANES, :], edge], axis=0)
            wk = cw_ref[k:k + 1, :]
        else:
            tmp = pltpu.roll(u, k, axis=0)
            hr = pltpu.roll(halo, k, axis=0)
            edge = jnp.where(row8 < k, hr, tmp[:SUBLANES, :])
            shifted = jnp.concatenate([edge, tmp[SUBLANES:, :]], axis=0)
            wk = cw_ref[LRU_CONV_K - 1 - k:LRU_CONV_K - k, :]
        uc = uc + wk * shifted
    halo_ref[...] = u[:SUBLANES, :] if reverse else u[tt - SUBLANES:, :]

    ucb = uc.astype(BF16)

    def block_diag(w_blocks):
        return jnp.concatenate(
            [jnp.dot(ucb[:, j * LRU_BLOCK:(j + 1) * LRU_BLOCK], w_blocks[j], preferred_element_type=F32)
             for j in range(LRU_BLOCKS)], axis=1)

    r = _sigmoid(block_diag(wa_ref) + ba_ref[...])
    gate_i = _sigmoid(block_diag(wx_ref) + bx_ref[...])
    neg_lam = -lam_ref[...]
    softplus = jnp.maximum(neg_lam, 0.0) + jnp.log1p(jnp.exp(-jnp.abs(neg_lam)))
    a = jnp.exp((-LRU_C * softplus) * r)
    a_ref[...] = a
    one_m_a2 = 1.0 - a * a
    root = jnp.where(one_m_a2 > 0.0, one_m_a2 * lax.rsqrt(one_m_a2), 0.0)
    b_ref[...] = root * (gate_i * uc)

    def group(gi, carry):
        g = n_groups - 1 - gi if reverse else gi
        off = pl.multiple_of(g * SUBLANES, SUBLANES)
        av = a_ref[pl.ds(off, SUBLANES), :]
        bv = b_ref[pl.ds(off, SUBLANES), :]
        for s in (1, 2, 4):
            if reverse:
                outside = row8 >= SUBLANES - s
                shift = SUBLANES - s
            else:
                outside = row8 < s
                shift = s
            a_sh = jnp.where(outside, 1.0, pltpu.roll(av, shift, axis=0))
            b_sh = jnp.where(outside, 0.0, pltpu.roll(bv, shift, axis=0))
            bv = av * b_sh + bv
            av = av * a_sh
        h = av * carry + bv
        h_ref[pl.ds(off, SUBLANES), :] = h
        last = h[0:1, :] if reverse else h[SUBLANES - 1:SUBLANES, :]
        return jnp.broadcast_to(last, (SUBLANES, w))

    carry_ref[...] = lax.fori_loop(0, n_groups, group, carry_ref[...])


def _rglru_scan(u, conv_w, conv_b, w_a, b_a, w_x, b_x, lam, *, reverse):
    bsz = u.shape[0]
    if reverse:
        tile_of = lambda i: N_LAT_TILES - i
    else:
        tile_of = lambda i: (i + N_LAT_TILES) % N_TILES
    tok_spec = pl.BlockSpec((None, TM, LRU_WIDTH), lambda b, i: (b, tile_of(i), 0))
    const = lambda shape: pl.BlockSpec(shape, lambda b, i: (0,) * len(shape))
    return pl.pallas_call(
        functools.partial(_scan_kernel, reverse=reverse),
        grid=(bsz, N_TILES),
        in_specs=[tok_spec, const((SUBLANES, LRU_WIDTH)), const((1, LRU_WIDTH)),
                  const((LRU_BLOCKS, LRU_BLOCK, LRU_BLOCK)), const((1, LRU_WIDTH)),
                  const((LRU_BLOCKS, LRU_BLOCK, LRU_BLOCK)), const((1, LRU_WIDTH)), const((1, LRU_WIDTH))],
        out_specs=tok_spec,
        out_shape=jax.ShapeDtypeStruct((bsz, TOK, LRU_WIDTH), F32),
        scratch_shapes=[pltpu.VMEM((SUBLANES, LRU_WIDTH), F32), pltpu.VMEM((SUBLANES, LRU_WIDTH), F32),
                        pltpu.VMEM((TM, LRU_WIDTH), F32), pltpu.VMEM((TM, LRU_WIDTH), F32)],
        compiler_params=_cparams(("parallel", "arbitrary"), 32),
        name="rglru_rev" if reverse else "rglru_fwd",
    )(u, conv_w, conv_b, w_a, b_a, w_x, b_x, lam)


def _pad_rows(a, rows):
    return jnp.pad(a, ((0, rows - a.shape[0]), (0, 0)))


def _rope_tables():
    t = jnp.arange(SEQ)
    n_freq = DA_HEAD_DIM // 4
    inv = ROPE_BASE ** (-jnp.arange(n_freq, dtype=F32) / n_freq)
    ang = jnp.concatenate([(t // GRID_W).astype(F32)[:, None] * inv,
                           (t % GRID_W).astype(F32)[:, None] * inv], axis=-1)
    cos, sin = jnp.cos(ang), jnp.sin(ang)
    cos64 = jnp.concatenate([cos, cos], axis=-1)
    sin64 = jnp.concatenate([-sin, sin], axis=-1)
    cos_t = jnp.concatenate([jnp.tile(cos64, (1, 2)), jnp.ones((CTX_LEN, LANES), F32)], axis=0)
    sin_t = jnp.concatenate([jnp.tile(sin64, (1, 2)), jnp.zeros((CTX_LEN, LANES), F32)], axis=0)
    return cos_t, sin_t


def _router_params(w_grp, b_grp, w_rt, b_rt):
    wr = jnp.concatenate([w_rt.reshape(D_MODEL, N_EXPERTS), w_grp], axis=1)
    br = jnp.concatenate([b_rt.reshape(N_EXPERTS), b_grp])
    pad = LANES - wr.shape[1]
    wr = jnp.pad(wr, ((0, 0), (0, pad)))
    wr_hi = wr.astype(BF16)
    wr_lo = (wr - wr_hi.astype(F32)).astype(BF16)
    return jnp.stack([wr_hi, wr_lo]), jnp.pad(br, (0, pad)).reshape(1, LANES)


def kernel(x, c, ctx, c_ctx, ada_w, ada_b, norm_mix, norm_ffn, ev_w_in, ev_conv_w, ev_conv_b, ev_q_norm, ev_k_norm, ev_lam_q1, ev_lam_k1, ev_lam_q2, ev_lam_k2, ev_sub_norm, ev_w_out, od_w_in, od_conv_w, od_conv_b, od_w_a, od_b_a, od_w_x, od_b_x, od_lam, od_w_out, moe_w_grp, moe_b_grp, moe_w_rt, moe_b_rt, moe_w_gate, moe_w_up, moe_w_down):
    bsz = x.shape[0]
    assert x.shape == (bsz, SEQ, D_MODEL) and ctx.shape == (bsz, CTX_LEN, D_MODEL) and bsz == 2
    depth = ada_w.shape[0]
    assert depth == 2


    cvec = _pad_rows(jnp.stack([c[0], c_ctx, c[1], c_ctx]), SUBLANES)
    mod_all = _ada_mod(cvec, ada_w, ada_b)
    mod_all = mod_all[:, :2 * bsz].reshape(depth, 2 * bsz, 6, D_MODEL)
    mod_all = jnp.pad(mod_all, ((0, 0), (0, 0), (0, SUBLANES - 6), (0, 0)))

    l = 0
    lam_init = 0.8 - 0.6 * math.exp(-0.3 * l)
    mod = mod_all[l]
    cos_t, sin_t = _rope_tables()
    blk = jnp.arange(DA_WIDTH) // DA_HEAD_DIM
    bd = jnp.where(blk[:, None] == blk[None, :], 1.0 / DA_HEAD_DIM, 0.0).astype(BF16)
    n_rep = DA_WIDTH // DA_HEAD_DIM
    cb, p, q, k, v = _inproj_even(
        x, ctx, mod, norm_mix[l].reshape(1, D_MODEL), ev_w_in[0].astype(BF16),
        jnp.tile(ev_q_norm[0], n_rep).reshape(1, DA_WIDTH), jnp.tile(ev_k_norm[0], n_rep).reshape(1, DA_WIDTH),
        cos_t, sin_t, bd)

    lamv = _pad_rows(jnp.pad(jnp.stack([ev_lam_q1[0], ev_lam_k1[0], ev_lam_q2[0], ev_lam_k2[0]]),
                             ((0, 0), (0, LANES - DA_HEAD_DIM))), SUBLANES)
    subg = ev_sub_norm[0].reshape(1, DA_V_DIM)
    o_lat = _attention(q, k, v, lamv, subg, tq=256, q_start=0, n_q=SEQ // 256,
                       kv_start=0, kv_len=TOK, tk=768, lam_init=lam_init)
    o_ctx = _attention(q, k, v, lamv, subg, tq=TM, q_start=SEQ, n_q=CTX_LEN // TM,
                       kv_start=SEQ, kv_len=CTX_LEN, tk=CTX_LEN, lam_init=lam_init)

    wr, br = _router_params(moe_w_grp[l], moe_b_grp[l], moe_w_rt[l], moe_b_rt[l])
    x1, hf, meta, rt, counts = _post_even(
        cb, p, _pad_rows(ev_conv_w[0], SUBLANES), ev_conv_b[0].reshape(1, CONV_WIDTH), o_lat, o_ctx,
        ev_w_out[0].astype(BF16), x, ctx, mod, norm_ffn[l].reshape(1, D_MODEL), wr, br)
    xs = _moe(hf, meta, rt, counts, x1, mod, moe_w_gate, moe_w_up, moe_w_down, l, tiles_per_batch=N_TILES)

    l = 1
    mod = mod_all[l]
    y, u = _inproj_odd(xs, mod, norm_mix[l].reshape(1, D_MODEL), od_w_in[0].astype(BF16))
    h_dirs = []
    for d in range(2):
        h_dirs.append(_rglru_scan(
            u, _pad_rows(od_conv_w[0, d], SUBLANES), od_conv_b[0, d].reshape(1, LRU_WIDTH),
            od_w_a[0, d].astype(BF16), od_b_a[0, d].reshape(1, LRU_WIDTH),
            od_w_x[0, d].astype(BF16), od_b_x[0, d].reshape(1, LRU_WIDTH),
            od_lam[0, d].reshape(1, LRU_WIDTH), reverse=bool(d)))
    wr, br = _router_params(moe_w_grp[l], moe_b_grp[l], moe_w_rt[l], moe_b_rt[l])
    x1, hf, meta, rt, counts = _post_odd(y, h_dirs[0], h_dirs[1], od_w_out[0].astype(BF16), xs, mod,
                                     norm_ffn[l].reshape(1, D_MODEL), wr, br)
    return _moe(hf, meta, rt, counts, x1, mod, moe_w_gate, moe_w_up, moe_w_down, l,
                tiles_per_batch=N_LAT_TILES)
```

```python
import functools
import math

import jax
import jax.numpy as jnp
from jax import lax
from jax.experimental import pallas as pl
from jax.experimental.pallas import tpu as pltpu

F32 = jnp.float32
BF16 = jnp.bfloat16
HIGHEST = lax.Precision.HIGHEST

D_MODEL = 1024
SEQ = 8192
CTX_LEN = 256
TOK = SEQ + CTX_LEN
GRID_W = 64
EPS = 1e-6

CONV_WIDTH = 512
DA_HEADS = 4
DA_HEAD_DIM = 64
DA_V_DIM = 2 * DA_HEAD_DIM
DA_WIDTH = DA_HEADS * DA_V_DIM
EVEN_IN = 3 * CONV_WIDTH + 3 * DA_WIDTH
ROPE_BASE = 10000.0

LRU_WIDTH = 1024
LRU_BLOCKS = 8
LRU_BLOCK = LRU_WIDTH // LRU_BLOCKS
LRU_CONV_K = 4
LRU_C = 8.0

N_GROUPS = 4
EXPERTS_PER_GROUP = 4
N_EXPERTS = N_GROUPS * EXPERTS_PER_GROUP
D_EXPERT = 512

LANES = 128
SUBLANES = 8
TM = 256
N_LAT_TILES = SEQ // TM
N_TILES = TOK // TM
ATTN_TQ = 512
ATTN_TK = 768
NEG = -1e30
MIB = 2 ** 20


def _cparams(semantics, vmem_mib):
    return pltpu.CompilerParams(dimension_semantics=semantics, vmem_limit_bytes=vmem_mib * MIB)


def _sigmoid(x):
    return 0.5 * jnp.tanh(0.5 * x) + 0.5


def _norm_mod(x, g, shift, scale):
    ms = jnp.mean(x * x, axis=-1, keepdims=True)
    return (x * lax.rsqrt(ms + EPS) * g) * (1.0 + scale) + shift


ADA_TN = 1536


def _ada_kernel(c_ref, w_ref, b_ref, o_ref):
    c = c_ref[...]
    a = c * _sigmoid(c)
    o_ref[...] = jnp.dot(a, w_ref[...], precision=HIGHEST, preferred_element_type=F32) + b_ref[...]


def _ada_mod(cvec, ada_w, ada_b):
    depth, d, n = ada_w.shape
    return pl.pallas_call(
        _ada_kernel,
        grid=(depth, n // ADA_TN),
        in_specs=[
            pl.BlockSpec((SUBLANES, d), lambda l, j: (0, 0)),
            pl.BlockSpec((None, d, ADA_TN), lambda l, j: (l, 0, j)),
            pl.BlockSpec((None, 1, ADA_TN), lambda l, j: (l, 0, j)),
        ],
        out_specs=pl.BlockSpec((None, SUBLANES, ADA_TN), lambda l, j: (l, 0, j)),
        out_shape=jax.ShapeDtypeStruct((depth, SUBLANES, n), F32),
        compiler_params=_cparams(("arbitrary", "arbitrary"), 40),
        name="ada_mod",
    )(cvec, ada_w, ada_b.reshape(depth, 1, n))


def _mod_index(b, i):
    return (b * 2 + i // N_LAT_TILES, 0, 0)


def _inproj_even_kernel(xl_ref, xc_ref, mod_ref, g_ref, w_ref, qg_ref, kg_ref, cos_ref, sin_ref, bd_ref,
                        cb_ref, p_ref, q_ref, k_ref, v_ref):
    x = jnp.where(pl.program_id(1) == N_LAT_TILES, xc_ref[...], xl_ref[...])
    h = _norm_mod(x, g_ref[...], mod_ref[0:1, :], mod_ref[1:2, :])
    z = jnp.dot(h.astype(BF16), w_ref[...], preferred_element_type=F32)
    cw = CONV_WIDTH
    cb_ref[...] = z[:, :cw]
    p_ref[...] = z[:, cw:2 * cw] * z[:, 2 * cw:3 * cw]

    reps = DA_WIDTH // LANES
    cosf = jnp.concatenate([cos_ref[...]] * reps, axis=1)
    sinf = jnp.concatenate([sin_ref[...]] * reps, axis=1)
    lane = lax.broadcasted_iota(jnp.int32, (TM, DA_WIDTH), 1)
    first_half = (lane & (DA_HEAD_DIM - 1)) < DA_HEAD_DIM // 2
    bd = bd_ref[...]

    def head_norm_rope(t, gain):
        ms = jnp.dot((t * t).astype(BF16), bd, preferred_element_type=F32)
        y = t * lax.rsqrt(ms + EPS) * gain
        fwd = pltpu.roll(y, DA_WIDTH - DA_HEAD_DIM // 2, axis=1)
        bwd = pltpu.roll(y, DA_HEAD_DIM // 2, axis=1)
        return y * cosf + jnp.where(first_half, fwd, bwd) * sinf

    base = 3 * cw
    q = head_norm_rope(z[:, base:base + DA_WIDTH], qg_ref[...])
    q_ref[...] = (q * (DA_HEAD_DIM ** -0.5 * math.log2(math.e))).astype(BF16)
    k = head_norm_rope(z[:, base + DA_WIDTH:base + 2 * DA_WIDTH], kg_ref[...])
    k_ref[...] = k.astype(BF16)
    v = z[:, base + 2 * DA_WIDTH:].astype(BF16)
    ones = jnp.ones((TM, DA_V_DIM), BF16)
    v_ref[...] = jnp.concatenate(
        [blk for h in range(DA_HEADS) for blk in (v[:, h * DA_V_DIM:(h + 1) * DA_V_DIM], ones)], axis=1)


def _lat_ctx_specs(width):
    return [pl.BlockSpec((None, TM, width), lambda b, i: (b, jnp.minimum(i, N_LAT_TILES - 1), 0)),
            pl.BlockSpec((None, TM, width), lambda b, i: (b, 0, 0))]


def _inproj_even(x, ctx, mod, g, w_in, qg, kg, cos, sin, bd):
    bsz = x.shape[0]
    tok_spec = lambda width: pl.BlockSpec((None, TM, width), lambda b, i: (b, i, 0))
    const = lambda shape: pl.BlockSpec(shape, lambda b, i: (0,) * len(shape))
    return pl.pallas_call(
        _inproj_even_kernel,
        grid=(bsz, N_TILES),
        in_specs=_lat_ctx_specs(D_MODEL) + [
            pl.BlockSpec((None, SUBLANES, D_MODEL), _mod_index),
            const((1, D_MODEL)),
            const((D_MODEL, EVEN_IN)),
            const((1, DA_WIDTH)),
            const((1, DA_WIDTH)),
            pl.BlockSpec((TM, LANES), lambda b, i: (i, 0)),
            pl.BlockSpec((TM, LANES), lambda b, i: (i, 0)),
            const((DA_WIDTH, DA_WIDTH)),
        ],
        out_specs=[tok_spec(CONV_WIDTH), tok_spec(CONV_WIDTH), tok_spec(DA_WIDTH), tok_spec(DA_WIDTH),
                   tok_spec(2 * DA_WIDTH)],
        out_shape=[
            jax.ShapeDtypeStruct((bsz, TOK, CONV_WIDTH), F32),
            jax.ShapeDtypeStruct((bsz, TOK, CONV_WIDTH), F32),
            jax.ShapeDtypeStruct((bsz, TOK, DA_WIDTH), BF16),
            jax.ShapeDtypeStruct((bsz, TOK, DA_WIDTH), BF16),
            jax.ShapeDtypeStruct((bsz, TOK, 2 * DA_WIDTH), BF16),
        ],
        compiler_params=_cparams(("parallel", "arbitrary"), 48),
        name="inproj_even",
    )(x, ctx, mod, g, w_in, qg, kg, cos, sin, bd)


def _attn_kernel(lam_ref, subg_ref, q_ref, k_ref, v_ref, o_ref, qz_ref, m_ref, acc_ref,
                 sa_ref, sb_ref, *, tq, tk, nk, lam_init):
    q = q_ref[...]
    lane = lax.broadcasted_iota(jnp.int32, (tq, LANES), 1)
    zero = jnp.zeros_like(q)
    qz_ref[0:tq, :] = jnp.where(lane < DA_HEAD_DIM, q, zero)
    qz_ref[tq:2 * tq, :] = jnp.where(lane >= DA_HEAD_DIM, q, zero)
    m_ref[...] = jnp.full(m_ref.shape, NEG, F32)
    acc_ref[...] = jnp.zeros(acc_ref.shape, F32)

    def scores(j, dst_ref):
        off = pl.multiple_of(j * tk, tk)
        dst_ref[...] = lax.dot_general(qz_ref[...], k_ref[pl.ds(off, tk), :], (((1,), (1,)), ((), ())),
                                       preferred_element_type=F32)

    def update(j, src_ref):
        off = pl.multiple_of(j * tk, tk)
        s = src_ref[...]
        m_prev = m_ref[...]
        m_new = jnp.maximum(m_prev, jnp.max(s, axis=1, keepdims=True))
        alpha = jnp.exp2(m_prev - m_new)
        p = jnp.exp2((s - m_new[:, :1]).astype(BF16))
        pv = jnp.dot(p, v_ref[pl.ds(off, tk), :], preferred_element_type=F32)
        acc_ref[:, :DA_V_DIM] = alpha * acc_ref[:, :DA_V_DIM] + pv[:, :DA_V_DIM]
        acc_ref[:, DA_V_DIM:] = alpha * acc_ref[:, DA_V_DIM:] + pv[:, DA_V_DIM:]
        m_ref[...] = m_new

    bufs = (sa_ref, sb_ref)
    scores(0, bufs[0])
    for j in range(nk):
        if j + 1 < nk:
            scores(j + 1, bufs[(j + 1) % 2])
        update(j, bufs[j % 2])

    o = acc_ref[:, :DA_V_DIM] / acc_ref[:, DA_V_DIM:]
    lv = lam_ref[...]
    lam = (jnp.exp(jnp.sum(lv[0:1, :] * lv[1:2, :], axis=1, keepdims=True))
           - jnp.exp(jnp.sum(lv[2:3, :] * lv[3:4, :], axis=1, keepdims=True)) + lam_init)
    d = o[:tq] - lam * o[tq:]
    ms = jnp.mean(d * d, axis=1, keepdims=True)
    o_ref[...] = (d * lax.rsqrt(ms + EPS) * subg_ref[...] * (1.0 - lam_init)).astype(BF16)


def _attention(q, k, v, lamv, subg, *, tq, q_start, n_q, kv_start, kv_len, tk, lam_init):
    bsz = q.shape[0]
    q_blk0 = q_start // tq
    kv_blk = kv_start // kv_len
    q_spec = pl.BlockSpec((None, tq, LANES), lambda b, h, i: (b, q_blk0 + i, h))
    o_spec = pl.BlockSpec((None, tq, LANES), lambda b, h, i: (b, i, h))
    k_spec = pl.BlockSpec((None, kv_len, LANES), lambda b, h, i: (b, kv_blk, h))
    v_spec = pl.BlockSpec((None, kv_len, 2 * DA_V_DIM), lambda b, h, i: (b, kv_blk, h))
    const = lambda shape: pl.BlockSpec(shape, lambda b, h, i: (0,) * len(shape))
    return pl.pallas_call(
        functools.partial(_attn_kernel, tq=tq, tk=tk, nk=kv_len // tk, lam_init=lam_init),
        grid=(bsz, DA_HEADS, n_q),
        in_specs=[const((SUBLANES, LANES)), const((1, LANES)), q_spec, k_spec, v_spec],
        out_specs=o_spec,
        out_shape=jax.ShapeDtypeStruct((bsz, n_q * tq, DA_WIDTH), BF16),
        scratch_shapes=[
            pltpu.VMEM((2 * tq, LANES), BF16),
            pltpu.VMEM((2 * tq, LANES), F32),
            pltpu.VMEM((2 * tq, 2 * DA_V_DIM), F32),
            pltpu.VMEM((2 * tq, tk), F32),
            pltpu.VMEM((2 * tq, tk), F32),
        ],
        compiler_params=_cparams(("parallel", "parallel", "arbitrary"), 48),
        name="diff_attn",
    )(lamv, subg, q, k, v)


PAIRS_PER_GROUP = EXPERTS_PER_GROUP * (EXPERTS_PER_GROUP - 1) // 2
N_BUCKETS = N_GROUPS * PAIRS_PER_GROUP
BUCKET_ROWS = -(-N_BUCKETS // SUBLANES) * SUBLANES
META_BUCKET, META_RANK, META_W_LO, META_W_HI = 0, 1, 2, 3


def _post_tail(y, first_step, x, mod_ref, g_ref, wr_ref, br_ref, x1_ref, hf_ref, meta_ref, rt_ref, cnt_ref,
               cnt_scr):
    x1 = x + mod_ref[2:3, :] * y
    x1_ref[...] = x1
    hf = _norm_mod(x1, g_ref[...], mod_ref[3:4, :], mod_ref[4:5, :])
    hf_ref[...] = hf

    hf_hi = hf.astype(BF16)
    hf_lo = (hf - hf_hi.astype(F32)).astype(BF16)
    logits = (jnp.dot(hf_hi, wr_ref[0], preferred_element_type=F32)
              + jnp.dot(hf_lo, wr_ref[0], preferred_element_type=F32)
              + jnp.dot(hf_hi, wr_ref[1], preferred_element_type=F32)) + br_ref[...]
    tm = logits.shape[0]
    lt = jnp.transpose(logits)
    big = float(LANES)
    e_all = lt[0:N_EXPERTS, :]
    g = lt[N_EXPERTS:N_EXPERTS + N_GROUPS, :]
    row_g = lax.broadcasted_iota(jnp.int32, (N_GROUPS, tm), 0).astype(F32)
    gm = jnp.max(g, axis=0, keepdims=True)
    g_idx = jnp.min(jnp.where(g == gm, row_g, big), axis=0, keepdims=True)
    p_sel = 1.0 / jnp.sum(jnp.exp(g - gm), axis=0, keepdims=True)
    el = e_all[0:EXPERTS_PER_GROUP, :]
    for grp in range(1, N_GROUPS):
        el = jnp.where(g_idx == float(grp), e_all[grp * EXPERTS_PER_GROUP:(grp + 1) * EXPERTS_PER_GROUP, :], el)
    row_e = lax.broadcasted_iota(jnp.int32, (EXPERTS_PER_GROUP, tm), 0).astype(F32)
    v1 = jnp.max(el, axis=0, keepdims=True)
    i1 = jnp.min(jnp.where(el == v1, row_e, big), axis=0, keepdims=True)
    el2 = jnp.where(row_e == i1, NEG, el)
    v2 = jnp.max(el2, axis=0, keepdims=True)
    i2 = jnp.min(jnp.where(el2 == v2, row_e, big), axis=0, keepdims=True)
    t = jnp.exp(v2 - v1)
    w1 = p_sel / (1.0 + t)
    w2 = t * w1

    first_lower = i1 < i2
    a = jnp.minimum(i1, i2)
    b = jnp.maximum(i1, i2)
    code = a * EXPERTS_PER_GROUP + b
    pair = jnp.where(code == 1.0, 0.0, jnp.where(code == 6.0, 1.0, jnp.where(code == 2.0, 2.0, jnp.where(
        code == 3.0, 3.0, jnp.where(code == 7.0, 4.0, 5.0)))))
    bucket = g_idx * PAIRS_PER_GROUP + pair
    w_low = jnp.where(first_lower, w1, w2)
    w_high = jnp.where(first_lower, w2, w1)
    w_lo = jnp.where(code == 1.0, w_low, w_high)
    w_hi = jnp.where(code == 1.0, w_high, w_low)

    @pl.when(first_step)
    def _():
        cnt_scr[...] = jnp.zeros(cnt_scr.shape, F32)

    row_b = lax.broadcasted_iota(jnp.int32, (BUCKET_ROWS, tm), 0).astype(F32)
    onehot = jnp.where(row_b == bucket, 1.0, 0.0)
    r_i = lax.broadcasted_iota(jnp.int32, (tm, tm), 0)
    c_i = lax.broadcasted_iota(jnp.int32, (tm, tm), 1)
    earlier = jnp.where(r_i < c_i, 1.0, 0.0).astype(BF16)
    prefix = jnp.dot(onehot.astype(BF16), earlier, preferred_element_type=F32)
    counts = cnt_scr[...]
    rank = jnp.sum(onehot * (prefix + counts[:, 0:1]), axis=0, keepdims=True)
    counts = counts + jnp.sum(onehot, axis=1, keepdims=True)
    cnt_scr[...] = counts
    cnt_ref[...] = counts

    def record(rows):
        row = lax.broadcasted_iota(jnp.int32, (rows, tm), 0)
        return jnp.where(row == META_BUCKET, bucket, jnp.where(row == META_RANK, rank, jnp.where(
            row == META_W_LO, w_lo, jnp.where(row == META_W_HI, w_hi, 0.0))))

    rt_ref[...] = record(SUBLANES)
    meta_ref[...] = jnp.transpose(record(LANES))


def _post_even_kernel(cb_ref, p_ref, pprev_ref, pnext_ref, cw_ref, cbias_ref, olat_ref, octx_ref, w_ref,
                      xl_ref, xc_ref, mod_ref, g_ref, wr_ref, br_ref, x1_ref, hf_ref, meta_ref, rt_ref, cnt_ref, cnt_scr):
    i = pl.program_id(1)
    first_step = jnp.logical_and(pl.program_id(0) == 0, i == 0)
    pc = p_ref[...]
    row = lax.broadcasted_iota(jnp.int32, pc.shape, 0)
    has_prev = jnp.logical_and(i != 0, i != N_LAT_TILES)
    has_next = i < N_LAT_TILES - 1
    prev_row = jnp.where(has_prev, pprev_ref[SUBLANES - 1:SUBLANES, :], 0.0)
    next_row = jnp.where(has_next, pnext_ref[0:1, :], 0.0)
    up = jnp.where(row == 0, prev_row, pltpu.roll(pc, 1, axis=0))
    dn = jnp.where(row == TM - 1, next_row, pltpu.roll(pc, TM - 1, axis=0))
    conv = cbias_ref[...] + cw_ref[0:1, :] * up + cw_ref[1:2, :] * pc + cw_ref[2:3, :] * dn
    out_a = (cb_ref[...] * conv).astype(BF16)
    o = jnp.where(i == N_LAT_TILES, octx_ref[...], olat_ref[...])
    x = jnp.where(i == N_LAT_TILES, xc_ref[...], xl_ref[...])
    y = (jnp.dot(out_a, w_ref[0:CONV_WIDTH, :], preferred_element_type=F32)
         + jnp.dot(o, w_ref[CONV_WIDTH:, :], preferred_element_type=F32))
    _post_tail(y, first_step, x, mod_ref, g_ref, wr_ref, br_ref, x1_ref, hf_ref, meta_ref, rt_ref, cnt_ref,
               cnt_scr)


def _post_odd_kernel(y_ref, hf_in_ref, hb_in_ref, w_ref, x_ref, mod_ref, g_ref, wr_ref, br_ref,
                     x1_ref, hf_ref, meta_ref, rt_ref, cnt_ref, cnt_scr):
    first_step = jnp.logical_and(pl.program_id(0) == 0, pl.program_id(1) == 0)
    a = (y_ref[...].astype(F32) * (hf_in_ref[...] + hb_in_ref[...])).astype(BF16)
    y = jnp.dot(a, w_ref[...], preferred_element_type=F32)
    _post_tail(y, first_step, x_ref[...], mod_ref, g_ref, wr_ref, br_ref, x1_ref, hf_ref, meta_ref, rt_ref, cnt_ref,
               cnt_scr)


def _post_specs(bsz, rows):
    tok_spec = lambda width: pl.BlockSpec((None, TM, width), lambda b, i: (b, i, 0))
    const = lambda shape: pl.BlockSpec(shape, lambda b, i: (0,) * len(shape))
    tail_in = [pl.BlockSpec((None, SUBLANES, D_MODEL), _mod_index), const((1, D_MODEL)),
               const((2, D_MODEL, LANES)), const((1, LANES))]
    tiles = rows // TM
    out_specs = [tok_spec(D_MODEL), tok_spec(D_MODEL), tok_spec(LANES),
                 pl.BlockSpec((None, SUBLANES, TM), lambda b, i: (b * tiles + i, 0, 0)),
                 const((BUCKET_ROWS, LANES))]
    out_shape = [jax.ShapeDtypeStruct((bsz, rows, D_MODEL), F32),
                 jax.ShapeDtypeStruct((bsz, rows, D_MODEL), F32),
                 jax.ShapeDtypeStruct((bsz, rows, LANES), F32),
                 jax.ShapeDtypeStruct((bsz * tiles, SUBLANES, TM), F32),
                 jax.ShapeDtypeStruct((BUCKET_ROWS, LANES), F32)]
    scratch = [pltpu.VMEM((BUCKET_ROWS, LANES), F32)]
    return tok_spec, const, tail_in, out_specs, out_shape, scratch


def _post_even(cb, p, conv_w, conv_b, o_lat, o_ctx, w_out, x, ctx, mod, g, wr, br):
    bsz = x.shape[0]
    tok_spec, const, tail_in, out_specs, out_shape, scratch = _post_specs(bsz, TOK)
    halo_blocks = TM // SUBLANES
    last_halo = TOK // SUBLANES - 1
    prev_spec = pl.BlockSpec((None, SUBLANES, CONV_WIDTH),
                             lambda b, i: (b, jnp.maximum(i * halo_blocks - 1, 0), 0))
    next_spec = pl.BlockSpec((None, SUBLANES, CONV_WIDTH),
                             lambda b, i: (b, jnp.minimum((i + 1) * halo_blocks, last_halo), 0))
    return pl.pallas_call(
        _post_even_kernel,
        grid=(bsz, N_TILES),
        in_specs=[tok_spec(CONV_WIDTH), tok_spec(CONV_WIDTH), prev_spec, next_spec,
                  const((SUBLANES, CONV_WIDTH)), const((1, CONV_WIDTH))] + _lat_ctx_specs(DA_WIDTH)
        + [const((D_MODEL, D_MODEL))] + _lat_ctx_specs(D_MODEL) + tail_in,
        out_specs=out_specs,
        out_shape=out_shape,
        scratch_shapes=scratch,
        compiler_params=_cparams(("arbitrary", "arbitrary"), 48),
        name="post_even",
    )(cb, p, p, p, conv_w, conv_b, o_lat, o_ctx, w_out, x, ctx, mod, g, wr, br)


def _post_odd(y, hfw, hbw, w_out, xs, mod, g, wr, br):
    bsz = xs.shape[0]
    tok_spec, const, tail_in, out_specs, out_shape, scratch = _post_specs(bsz, SEQ)
    return pl.pallas_call(
        _post_odd_kernel,
        grid=(bsz, N_LAT_TILES),
        in_specs=[tok_spec(LRU_WIDTH), tok_spec(LRU_WIDTH), tok_spec(LRU_WIDTH),
                  const((LRU_WIDTH, D_MODEL)), tok_spec(D_MODEL)] + tail_in,
        out_specs=out_specs,
        out_shape=out_shape,
        scratch_shapes=scratch,
        compiler_params=_cparams(("arbitrary", "arbitrary"), 48),
        name="post_odd",
    )(y, hfw, hbw, w_out, xs, mod, g, wr, br)


_PAIR_SLOTS = [(0, 1), (2, 1), (2, 0), (3, 0), (3, 1), (3, 2)]
_BUCKET_LO = [g * EXPERTS_PER_GROUP + a for g in range(N_GROUPS) for a, _ in _PAIR_SLOTS]
_BUCKET_HI = [g * EXPERTS_PER_GROUP + b for g in range(N_GROUPS) for _, b in _PAIR_SLOTS]


def _sorted_tiles(n_tokens):
    return n_tokens // TM + N_BUCKETS


def _route_plan(rt, counts, n_tokens):
    n_tiles = _sorted_tiles(n_tokens)
    bucket = rt[:, META_BUCKET, :].astype(jnp.int32).reshape(n_tokens)
    rank = rt[:, META_RANK, :].astype(jnp.int32).reshape(n_tokens)
    cnt = counts[:N_BUCKETS, 0].astype(jnp.int32)
    tiles_per = (cnt + TM - 1) // TM
    tile_end = jnp.cumsum(tiles_per)
    row_start = (tile_end - tiles_per) * TM
    dest = (row_start[bucket] + rank).reshape(n_tokens // TM, 1, TM)
    tile_bucket = jnp.minimum(jnp.sum(jnp.arange(n_tiles)[:, None] >= tile_end[None, :], axis=1), N_BUCKETS - 1)
    e_lo = jnp.asarray(_BUCKET_LO, jnp.int32)[tile_bucket]
    e_hi = jnp.asarray(_BUCKET_HI, jnp.int32)[tile_bucket]
    return dest, e_lo, e_hi, tile_end[-1:].astype(jnp.int32)


DISPATCH_TILES = 2


def _dispatch_kernel(dest_ref, hf_ref, hs_in_hbm, hs_hbm, sem):
    del hs_in_hbm
    for h in range(DISPATCH_TILES):
        for r in range(TM):
            pltpu.make_async_copy(hf_ref.at[pl.ds(h * TM + r, 1), :],
                                  hs_hbm.at[pl.ds(dest_ref[h, 0, r], 1), :], sem).start()
    pltpu.make_async_copy(hf_ref, hs_hbm.at[pl.ds(0, DISPATCH_TILES * TM), :], sem).wait()


def _dispatch(dest, hf, n_tokens):
    rows = _sorted_tiles(n_tokens) * TM
    block = DISPATCH_TILES * TM
    assert n_tokens % block == 0
    return pl.pallas_call(
        _dispatch_kernel,
        grid=(n_tokens // block,),
        in_specs=[pl.BlockSpec((DISPATCH_TILES, 1, TM), lambda i: (i, 0, 0), memory_space=pltpu.SMEM),
                  pl.BlockSpec((block, D_MODEL), lambda i: (i, 0)),
                  pl.BlockSpec(memory_space=pl.ANY)],
        out_specs=pl.BlockSpec(memory_space=pl.ANY),
        out_shape=jax.ShapeDtypeStruct((rows, D_MODEL), F32),
        scratch_shapes=[pltpu.SemaphoreType.DMA(())],
        input_output_aliases={2: 0},
        compiler_params=_cparams(("arbitrary",), 16),
        name="moe_dispatch",
    )(dest, hf, jnp.zeros((rows, D_MODEL), F32))


def _gather_rows(idx_ref, src_hbm, dst_ref, sem):
    for r in range(TM):
        pltpu.make_async_copy(src_hbm.at[pl.ds(idx_ref[0, r], 1), :], dst_ref.at[pl.ds(r, 1), :], sem).start()


def _gather_wait(src_hbm, dst_ref, sem):
    pltpu.make_async_copy(src_hbm.at[pl.ds(0, TM), :], dst_ref, sem).wait()


def _moe_routed_kernel(elo_ref, ehi_ref, nused_ref, h_ref, wg_lo, wu_lo, wd_lo, wg_hi, wu_hi, wd_hi, y_ref):
    del elo_ref, ehi_ref
    j = pl.program_id(0)
    n_used = nused_ref[0]

    @pl.when(j < n_used)
    def _():
        h = h_ref[...].astype(BF16)

        def expert(wg_ref, wu_ref, wd_ref):
            hg = jnp.dot(h, wg_ref[...].astype(BF16), preferred_element_type=F32)
            hu = jnp.dot(h, wu_ref[...].astype(BF16), preferred_element_type=F32)
            act = (hg * _sigmoid(hg)) * hu
            return jnp.dot(act.astype(BF16), wd_ref[...].astype(BF16), preferred_element_type=F32)

        y_ref[:, :D_MODEL] = expert(wg_lo, wu_lo, wd_lo)
        y_ref[:, D_MODEL:] = expert(wg_hi, wu_hi, wd_hi)

    @pl.when(j >= n_used)
    def _():
        y_ref[...] = jnp.zeros(y_ref.shape, F32)


def _moe_routed(hs, e_lo, e_hi, n_used, wg, wu, wd, layer):
    n_tiles = hs.shape[0] // TM
    up_spec = lambda tbl: pl.BlockSpec((None, None, D_MODEL, D_EXPERT),
                                       lambda j, lo, hi, nu: (layer, (lo, hi)[tbl][j], 0, 0))
    dn_spec = lambda tbl: pl.BlockSpec((None, None, D_EXPERT, D_MODEL),
                                       lambda j, lo, hi, nu: (layer, (lo, hi)[tbl][j], 0, 0))
    grid_spec = pltpu.PrefetchScalarGridSpec(
        num_scalar_prefetch=3,
        grid=(n_tiles,),
        in_specs=[pl.BlockSpec((TM, D_MODEL), lambda j, lo, hi, nu: (j, 0)),
                  up_spec(0), up_spec(0), dn_spec(0), up_spec(1), up_spec(1), dn_spec(1)],
        out_specs=pl.BlockSpec((TM, 2 * D_MODEL), lambda j, lo, hi, nu: (j, 0)),
    )
    return pl.pallas_call(
        _moe_routed_kernel,
        grid_spec=grid_spec,
        out_shape=jax.ShapeDtypeStruct((n_tiles * TM, 2 * D_MODEL), F32),
        compiler_params=_cparams(("arbitrary",), 56),
        name="moe_routed",
    )(e_lo, e_hi, n_used, hs, wg, wu, wd, wg, wu, wd)


def _combine_kernel(dest_ref, dest_next_ref, y_hbm, x_ref, meta_ref, mod_ref, o_ref, buf_ref, sems):
    step = pl.program_id(0) * pl.num_programs(1) + pl.program_id(1)
    n_steps = pl.num_programs(0) * pl.num_programs(1)
    slot = step % 2

    @pl.when(step == 0)
    def _():
        _gather_rows(dest_ref, y_hbm, buf_ref.at[0], sems.at[0])

    @pl.when(step + 1 < n_steps)
    def _():
        _gather_rows(dest_next_ref, y_hbm, buf_ref.at[1 - slot], sems.at[1 - slot])

    _gather_wait(y_hbm, buf_ref.at[slot], sems.at[slot])
    meta = meta_ref[...]
    moe = (meta[:, META_W_LO:META_W_LO + 1] * buf_ref[slot, :, :D_MODEL]
           + meta[:, META_W_HI:META_W_HI + 1] * buf_ref[slot, :, D_MODEL:])
    o_ref[...] = x_ref[...] + mod_ref[5:6, :] * moe


def _combine(dest, y, x1, meta, mod, *, tiles_per_batch):
    bsz = x1.shape[0]
    n_steps = bsz * tiles_per_batch
    tok_spec = lambda width: pl.BlockSpec((None, TM, width), lambda b, i: (b, i, 0))
    idx_spec = lambda ahead: pl.BlockSpec(
        (None, 1, TM), lambda b, i: (jnp.minimum(b * tiles_per_batch + i + ahead, n_steps - 1), 0, 0),
        memory_space=pltpu.SMEM)
    return pl.pallas_call(
        _combine_kernel,
        grid=(bsz, tiles_per_batch),
        in_specs=[idx_spec(0), idx_spec(1), pl.BlockSpec(memory_space=pl.ANY), tok_spec(D_MODEL),
                  tok_spec(LANES), pl.BlockSpec((None, SUBLANES, D_MODEL), _mod_index)],
        out_specs=tok_spec(D_MODEL),
        out_shape=jax.ShapeDtypeStruct(x1.shape, F32),
        scratch_shapes=[pltpu.VMEM((2, TM, 2 * D_MODEL), F32), pltpu.SemaphoreType.DMA((2,))],
        compiler_params=_cparams(("arbitrary", "arbitrary"), 32),
        name="moe_combine",
    )(dest, dest, y, x1, meta, mod)


def _moe(hf, meta, rt, counts, x1, mod, wg, wu, wd, layer, *, tiles_per_batch):
    n_tokens = hf.shape[0] * hf.shape[1]
    dest, e_lo, e_hi, n_used = _route_plan(rt, counts, n_tokens)
    hs = _dispatch(dest, hf.reshape(n_tokens, D_MODEL), n_tokens)
    y = _moe_routed(hs, e_lo, e_hi, n_used, wg, wu, wd, layer)
    return _combine(dest, y, x1, meta, mod, tiles_per_batch=tiles_per_batch)


def _inproj_odd_kernel(x_ref, mod_ref, g_ref, w_ref, y_ref, u_ref):
    h = _norm_mod(x_ref[...], g_ref[...], mod_ref[0:1, :], mod_ref[1:2, :])
    z = jnp.dot(h.astype(BF16), w_ref[...], preferred_element_type=F32)
    zy = z[:, :LRU_WIDTH]
    c0 = math.sqrt(2.0 / math.pi)
    y_ref[...] = (0.5 * zy * (1.0 + jnp.tanh(c0 * (zy + 0.044715 * (zy * zy * zy))))).astype(BF16)
    u_ref[...] = z[:, LRU_WIDTH:]


def _inproj_odd(xs, mod, g, w_in):
    bsz = xs.shape[0]
    tok_spec = lambda width: pl.BlockSpec((None, TM, width), lambda b, i: (b, i, 0))
    const = lambda shape: pl.BlockSpec(shape, lambda b, i: (0,) * len(shape))
    return pl.pallas_call(
        _inproj_odd_kernel,
        grid=(bsz, N_TILES),
        in_specs=[tok_spec(D_MODEL), pl.BlockSpec((None, SUBLANES, D_MODEL), _mod_index),
                  const((1, D_MODEL)), const((D_MODEL, 2 * LRU_WIDTH))],
        out_specs=[tok_spec(LRU_WIDTH), tok_spec(LRU_WIDTH)],
        out_shape=[jax.ShapeDtypeStruct((bsz, TOK, LRU_WIDTH), BF16),
                   jax.ShapeDtypeStruct((bsz, TOK, LRU_WIDTH), F32)],
        compiler_params=_cparams(("parallel", "arbitrary"), 48),
        name="inproj_odd",
    )(xs, mod, g, w_in)


def _scan_kernel(u_ref, cw_ref, cbias_ref, wa_ref, ba_ref, wx_ref, bx_ref, lam_ref, h_ref,
                 halo_ref, carry_ref, a_ref, b_ref, *, reverse):
    i = pl.program_id(1)
    tt, w = u_ref.shape
    n_groups = tt // SUBLANES

    @pl.when(i <= 1)
    def _():
        halo_ref[...] = jnp.zeros(halo_ref.shape, F32)

    @pl.when(i == 0)
    def _():
        carry_ref[...] = jnp.zeros(carry_ref.shape, F32)

    u = u_ref[...]
    halo = halo_ref[...]
    row8 = lax.broadcasted_iota(jnp.int32, (SUBLANES, w), 0)
    k_self = 0 if reverse else LRU_CONV_K - 1
    uc = cbias_ref[...] + cw_ref[k_self:k_self + 1, :] * u
    for k in range(1, LRU_CONV_K):
        if reverse:
            tmp = pltpu.roll(u, tt - k, axis=0)
            hr = pltpu.roll(halo, SUBLANES - k, axis=0)
            edge = jnp.where(row8 >= SUBLANES - k, hr, tmp[tt - SUBLANES:, :])
            shifted = jnp.concatenate([tmp[:tt - SUBLANES, :], edge], axis=0)
            wk = cw_ref[k:k + 1, :]
        else:
            tmp = pltpu.roll(u, k, axis=0)
            hr = pltpu.roll(halo, k, axis=0)
            edge = jnp.where(row8 < k, hr, tmp[:SUBLANES, :])
            shifted = jnp.concatenate([edge, tmp[SUBLANES:, :]], axis=0)
            wk = cw_ref[LRU_CONV_K - 1 - k:LRU_CONV_K - k, :]
        uc = uc + wk * shifted
    halo_ref[...] = u[:SUBLANES, :] if reverse else u[tt - SUBLANES:, :]

    ucb = uc.astype(BF16)

    def block_diag(w_blocks):
        return jnp.concatenate(
            [jnp.dot(ucb[:, j * LRU_BLOCK:(j + 1) * LRU_BLOCK], w_blocks[j], preferred_element_type=F32)
             for j in range(LRU_BLOCKS)], axis=1)

    gate_i = _sigmoid(block_diag(wx_ref) + bx_ref[...])
    neg_lam = -lam_ref[...]
    softplus = jnp.maximum(neg_lam, 0.0) + jnp.log1p(jnp.exp(-jnp.abs(neg_lam)))
    k = (-0.5 * LRU_C * math.log2(math.e)) * softplus
    a = jnp.exp2(k * jnp.tanh(0.5 * (block_diag(wa_ref) + ba_ref[...])) + k)
    a_ref[...] = a
    one_m_a2 = 1.0 - a * a
    root = jnp.where(one_m_a2 > 0.0, one_m_a2 * lax.rsqrt(one_m_a2), 0.0)
    b_ref[...] = root * (gate_i * uc)

    def group(gi, carry):
        g = n_groups - 1 - gi if reverse else gi
        off = pl.multiple_of(g * SUBLANES, SUBLANES)
        av = a_ref[pl.ds(off, SUBLANES), :]
        bv = b_ref[pl.ds(off, SUBLANES), :]
        for s in (1, 2, 4):
            if reverse:
                outside = row8 >= SUBLANES - s
                shift = SUBLANES - s
            else:
                outside = row8 < s
                shift = s
            a_sh = jnp.where(outside, 1.0, pltpu.roll(av, shift, axis=0))
            b_sh = jnp.where(outside, 0.0, pltpu.roll(bv, shift, axis=0))
            bv = av * b_sh + bv
            av = av * a_sh
        h = av * carry + bv
        h_ref[pl.ds(off, SUBLANES), :] = h
        last = h[0:1, :] if reverse else h[SUBLANES - 1:SUBLANES, :]
        return jnp.broadcast_to(last, (SUBLANES, w))

    carry_ref[...] = lax.fori_loop(0, n_groups, group, carry_ref[...])


def _rglru_scan(u, conv_w, conv_b, w_a, b_a, w_x, b_x, lam, *, reverse):
    bsz = u.shape[0]
    if reverse:
        tile_of = lambda i: N_LAT_TILES - i
    else:
        tile_of = lambda i: (i + N_LAT_TILES) % N_TILES
    tok_spec = pl.BlockSpec((None, TM, LRU_WIDTH), lambda b, i: (b, tile_of(i), 0))
    const = lambda shape: pl.BlockSpec(shape, lambda b, i: (0,) * len(shape))
    return pl.pallas_call(
        functools.partial(_scan_kernel, reverse=reverse),
        grid=(bsz, N_TILES),
        in_specs=[tok_spec, const((SUBLANES, LRU_WIDTH)), const((1, LRU_WIDTH)),
                  const((LRU_BLOCKS, LRU_BLOCK, LRU_BLOCK)), const((1, LRU_WIDTH)),
                  const((LRU_BLOCKS, LRU_BLOCK, LRU_BLOCK)), const((1, LRU_WIDTH)), const((1, LRU_WIDTH))],
        out_specs=tok_spec,
        out_shape=jax.ShapeDtypeStruct((bsz, TOK, LRU_WIDTH), F32),
        scratch_shapes=[pltpu.VMEM((SUBLANES, LRU_WIDTH), F32), pltpu.VMEM((SUBLANES, LRU_WIDTH), F32),
                        pltpu.VMEM((TM, LRU_WIDTH), F32), pltpu.VMEM((TM, LRU_WIDTH), F32)],
        compiler_params=_cparams(("parallel", "arbitrary"), 32),
        name="rglru_rev" if reverse else "rglru_fwd",
    )(u, conv_w, conv_b, w_a, b_a, w_x, b_x, lam)


def _pad_rows(a, rows):
    return jnp.pad(a, ((0, rows - a.shape[0]), (0, 0)))


def _rope_tables():
    t = jnp.arange(SEQ)
    n_freq = DA_HEAD_DIM // 4
    inv = ROPE_BASE ** (-jnp.arange(n_freq, dtype=F32) / n_freq)
    ang = jnp.concatenate([(t // GRID_W).astype(F32)[:, None] * inv,
                           (t % GRID_W).astype(F32)[:, None] * inv], axis=-1)
    cos, sin = jnp.cos(ang), jnp.sin(ang)
    cos64 = jnp.concatenate([cos, cos], axis=-1)
    sin64 = jnp.concatenate([-sin, sin], axis=-1)
    cos_t = jnp.concatenate([jnp.tile(cos64, (1, 2)), jnp.ones((CTX_LEN, LANES), F32)], axis=0)
    sin_t = jnp.concatenate([jnp.tile(sin64, (1, 2)), jnp.zeros((CTX_LEN, LANES), F32)], axis=0)
    return cos_t, sin_t


def _router_params(w_grp, b_grp, w_rt, b_rt):
    wr = jnp.concatenate([w_rt.reshape(D_MODEL, N_EXPERTS), w_grp], axis=1)
    br = jnp.concatenate([b_rt.reshape(N_EXPERTS), b_grp])
    pad = LANES - wr.shape[1]
    wr = jnp.pad(wr, ((0, 0), (0, pad)))
    wr_hi = wr.astype(BF16)
    wr_lo = (wr - wr_hi.astype(F32)).astype(BF16)
    return jnp.stack([wr_hi, wr_lo]), jnp.pad(br, (0, pad)).reshape(1, LANES)


def kernel(x, c, ctx, c_ctx, ada_w, ada_b, norm_mix, norm_ffn, ev_w_in, ev_conv_w, ev_conv_b, ev_q_norm, ev_k_norm, ev_lam_q1, ev_lam_k1, ev_lam_q2, ev_lam_k2, ev_sub_norm, ev_w_out, od_w_in, od_conv_w, od_conv_b, od_w_a, od_b_a, od_w_x, od_b_x, od_lam, od_w_out, moe_w_grp, moe_b_grp, moe_w_rt, moe_b_rt, moe_w_gate, moe_w_up, moe_w_down):
    bsz = x.shape[0]
    assert x.shape == (bsz, SEQ, D_MODEL) and ctx.shape == (bsz, CTX_LEN, D_MODEL) and bsz == 2
    depth = ada_w.shape[0]
    assert depth == 2


    cvec = _pad_rows(jnp.stack([c[0], c_ctx, c[1], c_ctx]), SUBLANES)
    mod_all = _ada_mod(cvec, ada_w, ada_b)
    mod_all = mod_all[:, :2 * bsz].reshape(depth, 2 * bsz, 6, D_MODEL)
    mod_all = jnp.pad(mod_all, ((0, 0), (0, 0), (0, SUBLANES - 6), (0, 0)))

    l = 0
    lam_init = 0.8 - 0.6 * math.exp(-0.3 * l)
    mod = mod_all[l]
    cos_t, sin_t = _rope_tables()
    blk = jnp.arange(DA_WIDTH) // DA_HEAD_DIM
    bd = jnp.where(blk[:, None] == blk[None, :], 1.0 / DA_HEAD_DIM, 0.0).astype(BF16)
    n_rep = DA_WIDTH // DA_HEAD_DIM
    cb, p, q, k, v = _inproj_even(
        x, ctx, mod, norm_mix[l].reshape(1, D_MODEL), ev_w_in[0].astype(BF16),
        jnp.tile(ev_q_norm[0], n_rep).reshape(1, DA_WIDTH), jnp.tile(ev_k_norm[0], n_rep).reshape(1, DA_WIDTH),
        cos_t, sin_t, bd)

    lamv = _pad_rows(jnp.pad(jnp.stack([ev_lam_q1[0], ev_lam_k1[0], ev_lam_q2[0], ev_lam_k2[0]]),
                             ((0, 0), (0, LANES - DA_HEAD_DIM))), SUBLANES)
    subg = ev_sub_norm[0].reshape(1, DA_V_DIM)
    o_lat = _attention(q, k, v, lamv, subg, tq=ATTN_TQ, q_start=0, n_q=SEQ // ATTN_TQ,
                       kv_start=0, kv_len=TOK, tk=ATTN_TK, lam_init=lam_init)
    o_ctx = _attention(q, k, v, lamv, subg, tq=TM, q_start=SEQ, n_q=CTX_LEN // TM,
                       kv_start=SEQ, kv_len=CTX_LEN, tk=CTX_LEN, lam_init=lam_init)

    wr, br = _router_params(moe_w_grp[l], moe_b_grp[l], moe_w_rt[l], moe_b_rt[l])
    x1, hf, meta, rt, counts = _post_even(
        cb, p, _pad_rows(ev_conv_w[0], SUBLANES), ev_conv_b[0].reshape(1, CONV_WIDTH), o_lat, o_ctx,
        ev_w_out[0].astype(BF16), x, ctx, mod, norm_ffn[l].reshape(1, D_MODEL), wr, br)
    xs = _moe(hf, meta, rt, counts, x1, mod, moe_w_gate, moe_w_up, moe_w_down, l, tiles_per_batch=N_TILES)

    l = 1
    mod = mod_all[l]
    y, u = _inproj_odd(xs, mod, norm_mix[l].reshape(1, D_MODEL), od_w_in[0].astype(BF16))
    h_dirs = []
    for d in range(2):
        h_dirs.append(_rglru_scan(
            u, _pad_rows(od_conv_w[0, d], SUBLANES), od_conv_b[0, d].reshape(1, LRU_WIDTH),
            od_w_a[0, d].astype(BF16), od_b_a[0, d].reshape(1, LRU_WIDTH),
            od_w_x[0, d].astype(BF16), od_b_x[0, d].reshape(1, LRU_WIDTH),
            od_lam[0, d].reshape(1, LRU_WIDTH), reverse=bool(d)))
    wr, br = _router_params(moe_w_grp[l], moe_b_grp[l], moe_w_rt[l], moe_b_rt[l])
    x1, hf, meta, rt, counts = _post_odd(y, h_dirs[0], h_dirs[1], od_w_out[0].astype(BF16), xs, mod,
                                     norm_ffn[l].reshape(1, D_MODEL), wr, br)
    return _moe(hf, meta, rt, counts, x1, mod, moe_w_gate, moe_w_up, moe_w_down, l,
                tiles_per_batch=N_LAT_TILES)
```

```python
import functools
import math

import jax
import jax.numpy as jnp
from jax import lax
from jax.experimental import pallas as pl
from jax.experimental.pallas import tpu as pltpu

F32 = jnp.float32
BF16 = jnp.bfloat16

D_MODEL = 1024
SEQ = 8192
CTX_LEN = 256
TOK = SEQ + CTX_LEN
GRID_W = 64
EPS = 1e-6

CONV_WIDTH = 512
DA_HEADS = 4
DA_HEAD_DIM = 64
DA_V_DIM = 2 * DA_HEAD_DIM
DA_WIDTH = DA_HEADS * DA_V_DIM
EVEN_IN = 3 * CONV_WIDTH + 3 * DA_WIDTH
ROPE_BASE = 10000.0

LRU_WIDTH = 1024
LRU_BLOCKS = 8
LRU_BLOCK = LRU_WIDTH // LRU_BLOCKS
LRU_CONV_K = 4
LRU_C = 8.0

N_GROUPS = 4
EXPERTS_PER_GROUP = 4
N_EXPERTS = N_GROUPS * EXPERTS_PER_GROUP
D_EXPERT = 512

LANES = 128
SUBLANES = 8
TM = 256
N_LAT_TILES = SEQ // TM
N_TILES = TOK // TM
ATTN_TQ = 512
ATTN_TK = 768
NEG = -1e30
MIB = 2 ** 20


def _cparams(semantics, vmem_mib):
    return pltpu.CompilerParams(dimension_semantics=semantics, vmem_limit_bytes=vmem_mib * MIB)


def _sigmoid(x):
    return 0.5 * jnp.tanh(0.5 * x) + 0.5


def _split_bf16(x):
    hi = x.astype(BF16)
    return hi, (x - hi.astype(F32)).astype(BF16)


def _dot_f32_grade(x, w):
    x_hi, x_lo = _split_bf16(x)
    w_hi, w_lo = _split_bf16(w)
    return (jnp.dot(x_hi, w_hi, preferred_element_type=F32) + jnp.dot(x_lo, w_hi, preferred_element_type=F32)
            + jnp.dot(x_hi, w_lo, preferred_element_type=F32))


def _norm_mod(x, g, shift, scale):
    ms = jnp.mean(x * x, axis=-1, keepdims=True)
    return (x * lax.rsqrt(ms + EPS) * g) * (1.0 + scale) + shift


ADA_TN = 1536


def _ada_kernel(c_ref, w_ref, b_ref, o_ref):
    c = c_ref[...]
    a = c * _sigmoid(c)
    o_ref[...] = _dot_f32_grade(a, w_ref[...]) + b_ref[...]


def _ada_mod(cvec, ada_w, ada_b):
    depth, d, n = ada_w.shape
    return pl.pallas_call(
        _ada_kernel,
        grid=(depth, n // ADA_TN),
        in_specs=[
            pl.BlockSpec((SUBLANES, d), lambda l, j: (0, 0)),
            pl.BlockSpec((None, d, ADA_TN), lambda l, j: (l, 0, j)),
            pl.BlockSpec((None, 1, ADA_TN), lambda l, j: (l, 0, j)),
        ],
        out_specs=pl.BlockSpec((None, SUBLANES, ADA_TN), lambda l, j: (l, 0, j)),
        out_shape=jax.ShapeDtypeStruct((depth, SUBLANES, n), F32),
        compiler_params=_cparams(("arbitrary", "arbitrary"), 40),
        name="ada_mod",
    )(cvec, ada_w, ada_b.reshape(depth, 1, n))


def _mod_index(b, i):
    return (b * 2 + i // N_LAT_TILES, 0, 0)


def _inproj_even_kernel(xl_ref, xc_ref, mod_ref, g_ref, w_ref, qg_ref, kg_ref, cos_ref, sin_ref, bd_ref,
                        cb_ref, p_ref, q_ref, k_ref, v_ref):
    x = jnp.where(pl.program_id(1) == N_LAT_TILES, xc_ref[...], xl_ref[...])
    h = _norm_mod(x, g_ref[...], mod_ref[0:1, :], mod_ref[1:2, :])
    z = jnp.dot(h.astype(BF16), w_ref[...], preferred_element_type=F32)
    cw = CONV_WIDTH
    cb_ref[...] = z[:, :cw]
    p_ref[...] = z[:, cw:2 * cw] * z[:, 2 * cw:3 * cw]

    reps = DA_WIDTH // LANES
    cosf = jnp.concatenate([cos_ref[...]] * reps, axis=1)
    sinf = jnp.concatenate([sin_ref[...]] * reps, axis=1)
    lane = lax.broadcasted_iota(jnp.int32, (TM, DA_WIDTH), 1)
    first_half = (lane & (DA_HEAD_DIM - 1)) < DA_HEAD_DIM // 2
    bd = bd_ref[...]

    def head_norm_rope(t, gain):
        ms = jnp.dot((t * t).astype(BF16), bd, preferred_element_type=F32)
        y = t * lax.rsqrt(ms + EPS) * gain
        fwd = pltpu.roll(y, DA_WIDTH - DA_HEAD_DIM // 2, axis=1)
        bwd = pltpu.roll(y, DA_HEAD_DIM // 2, axis=1)
        return y * cosf + jnp.where(first_half, fwd, bwd) * sinf

    base = 3 * cw
    q = head_norm_rope(z[:, base:base + DA_WIDTH], qg_ref[...])
    q_ref[...] = (q * (DA_HEAD_DIM ** -0.5 * math.log2(math.e))).astype(BF16)
    k = head_norm_rope(z[:, base + DA_WIDTH:base + 2 * DA_WIDTH], kg_ref[...])
    k_ref[...] = k.astype(BF16)
    v = z[:, base + 2 * DA_WIDTH:].astype(BF16)
    ones = jnp.ones((TM, DA_V_DIM), BF16)
    v_ref[...] = jnp.concatenate(
        [blk for h in range(DA_HEADS) for blk in (v[:, h * DA_V_DIM:(h + 1) * DA_V_DIM], ones)], axis=1)


def _lat_ctx_specs(width):
    return [pl.BlockSpec((None, TM, width), lambda b, i: (b, jnp.minimum(i, N_LAT_TILES - 1), 0)),
            pl.BlockSpec((None, TM, width), lambda b, i: (b, 0, 0))]


def _inproj_even(x, ctx, mod, g, w_in, qg, kg, cos, sin, bd):
    bsz = x.shape[0]
    tok_spec = lambda width: pl.BlockSpec((None, TM, width), lambda b, i: (b, i, 0))
    const = lambda shape: pl.BlockSpec(shape, lambda b, i: (0,) * len(shape))
    return pl.pallas_call(
        _inproj_even_kernel,
        grid=(bsz, N_TILES),
        in_specs=_lat_ctx_specs(D_MODEL) + [
            pl.BlockSpec((None, SUBLANES, D_MODEL), _mod_index),
            const((1, D_MODEL)),
            const((D_MODEL, EVEN_IN)),
            const((1, DA_WIDTH)),
            const((1, DA_WIDTH)),
            pl.BlockSpec((TM, LANES), lambda b, i: (i, 0)),
            pl.BlockSpec((TM, LANES), lambda b, i: (i, 0)),
            const((DA_WIDTH, DA_WIDTH)),
        ],
        out_specs=[tok_spec(CONV_WIDTH), tok_spec(CONV_WIDTH), tok_spec(DA_WIDTH), tok_spec(DA_WIDTH),
                   tok_spec(2 * DA_WIDTH)],
        out_shape=[
            jax.ShapeDtypeStruct((bsz, TOK, CONV_WIDTH), F32),
            jax.ShapeDtypeStruct((bsz, TOK, CONV_WIDTH), F32),
            jax.ShapeDtypeStruct((bsz, TOK, DA_WIDTH), BF16),
            jax.ShapeDtypeStruct((bsz, TOK, DA_WIDTH), BF16),
            jax.ShapeDtypeStruct((bsz, TOK, 2 * DA_WIDTH), BF16),
        ],
        compiler_params=_cparams(("parallel", "arbitrary"), 48),
        name="inproj_even",
    )(x, ctx, mod, g, w_in, qg, kg, cos, sin, bd)


def _attn_kernel(lam_ref, subg_ref, q_ref, k_ref, v_ref, o_ref, qz_ref, m_ref, acc_ref,
                 sa_ref, sb_ref, *, tq, tk, nk, lam_init):
    q = q_ref[...]
    lane = lax.broadcasted_iota(jnp.int32, (tq, LANES), 1)
    zero = jnp.zeros_like(q)
    qz_ref[0:tq, :] = jnp.where(lane < DA_HEAD_DIM, q, zero)
    qz_ref[tq:2 * tq, :] = jnp.where(lane >= DA_HEAD_DIM, q, zero)
    m_ref[...] = jnp.full(m_ref.shape, NEG, F32)
    acc_ref[...] = jnp.zeros(acc_ref.shape, F32)

    def scores(j, dst_ref):
        off = pl.multiple_of(j * tk, tk)
        dst_ref[...] = lax.dot_general(qz_ref[...], k_ref[pl.ds(off, tk), :], (((1,), (1,)), ((), ())),
                                       preferred_element_type=F32)

    def update(j, src_ref):
        off = pl.multiple_of(j * tk, tk)
        s = src_ref[...]
        m_prev = m_ref[...]
        m_new = jnp.maximum(m_prev, jnp.max(s, axis=1, keepdims=True))
        alpha = jnp.exp2(m_prev - m_new)
        p = jnp.exp2((s - m_new[:, :1]).astype(BF16))
        pv = jnp.dot(p, v_ref[pl.ds(off, tk), :], preferred_element_type=F32)
        acc_ref[:, :DA_V_DIM] = alpha * acc_ref[:, :DA_V_DIM] + pv[:, :DA_V_DIM]
        acc_ref[:, DA_V_DIM:] = alpha * acc_ref[:, DA_V_DIM:] + pv[:, DA_V_DIM:]
        m_ref[...] = m_new

    bufs = (sa_ref, sb_ref)
    scores(0, bufs[0])
    for j in range(nk):
        if j + 1 < nk:
            scores(j + 1, bufs[(j + 1) % 2])
        update(j, bufs[j % 2])

    o = acc_ref[:, :DA_V_DIM] / acc_ref[:, DA_V_DIM:]
    lv = lam_ref[...]
    lam = (jnp.exp(jnp.sum(lv[0:1, :] * lv[1:2, :], axis=1, keepdims=True))
           - jnp.exp(jnp.sum(lv[2:3, :] * lv[3:4, :], axis=1, keepdims=True)) + lam_init)
    d = o[:tq] - lam * o[tq:]
    ms = jnp.mean(d * d, axis=1, keepdims=True)
    o_ref[...] = (d * lax.rsqrt(ms + EPS) * subg_ref[...] * (1.0 - lam_init)).astype(BF16)


def _attention(q, k, v, lamv, subg, *, tq, q_start, n_q, kv_start, kv_len, tk, lam_init):
    bsz = q.shape[0]
    q_blk0 = q_start // tq
    kv_blk = kv_start // kv_len
    q_spec = pl.BlockSpec((None, tq, LANES), lambda b, h, i: (b, q_blk0 + i, h))
    o_spec = pl.BlockSpec((None, tq, LANES), lambda b, h, i: (b, i, h))
    k_spec = pl.BlockSpec((None, kv_len, LANES), lambda b, h, i: (b, kv_blk, h))
    v_spec = pl.BlockSpec((None, kv_len, 2 * DA_V_DIM), lambda b, h, i: (b, kv_blk, h))
    const = lambda shape: pl.BlockSpec(shape, lambda b, h, i: (0,) * len(shape))
    return pl.pallas_call(
        functools.partial(_attn_kernel, tq=tq, tk=tk, nk=kv_len // tk, lam_init=lam_init),
        grid=(bsz, DA_HEADS, n_q),
        in_specs=[const((SUBLANES, LANES)), const((1, LANES)), q_spec, k_spec, v_spec],
        out_specs=o_spec,
        out_shape=jax.ShapeDtypeStruct((bsz, n_q * tq, DA_WIDTH), BF16),
        scratch_shapes=[
            pltpu.VMEM((2 * tq, LANES), BF16),
            pltpu.VMEM((2 * tq, LANES), F32),
            pltpu.VMEM((2 * tq, 2 * DA_V_DIM), F32),
            pltpu.VMEM((2 * tq, tk), F32),
            pltpu.VMEM((2 * tq, tk), F32),
        ],
        compiler_params=_cparams(("parallel", "parallel", "arbitrary"), 48),
        name="diff_attn",
    )(lamv, subg, q, k, v)


_PAIR_SLOTS = [(0, 1), (2, 1), (2, 0), (3, 0), (3, 1), (3, 2)]
PAIRS_PER_GROUP = len(_PAIR_SLOTS)
assert PAIRS_PER_GROUP == EXPERTS_PER_GROUP * (EXPERTS_PER_GROUP - 1) // 2
N_BUCKETS = N_GROUPS * PAIRS_PER_GROUP
BUCKET_ROWS = -(-N_BUCKETS // SUBLANES) * SUBLANES
META_BUCKET, META_RANK, META_W_LO, META_W_HI = 0, 1, 2, 3


def _post_tail(y, first_step, x, mod_ref, g_ref, wr_ref, br_ref, x1_ref, hf_ref, meta_ref, rt_ref, cnt_ref,
               cnt_scr):
    x1 = x + mod_ref[2:3, :] * y
    x1_ref[...] = x1
    hf = _norm_mod(x1, g_ref[...], mod_ref[3:4, :], mod_ref[4:5, :])
    hf_ref[...] = hf

    hf_hi = hf.astype(BF16)
    hf_lo = (hf - hf_hi.astype(F32)).astype(BF16)
    logits = (jnp.dot(hf_hi, wr_ref[0], preferred_element_type=F32)
              + jnp.dot(hf_lo, wr_ref[0], preferred_element_type=F32)
              + jnp.dot(hf_hi, wr_ref[1], preferred_element_type=F32)) + br_ref[...]
    tm = logits.shape[0]
    lt = jnp.transpose(logits)
    big = float(LANES)
    e_all = lt[0:N_EXPERTS, :]
    g = lt[N_EXPERTS:N_EXPERTS + N_GROUPS, :]
    row_g = lax.broadcasted_iota(jnp.int32, (N_GROUPS, tm), 0).astype(F32)
    gm = jnp.max(g, axis=0, keepdims=True)
    g_idx = jnp.min(jnp.where(g == gm, row_g, big), axis=0, keepdims=True)
    p_sel = 1.0 / jnp.sum(jnp.exp(g - gm), axis=0, keepdims=True)
    el = e_all[0:EXPERTS_PER_GROUP, :]
    for grp in range(1, N_GROUPS):
        el = jnp.where(g_idx == float(grp), e_all[grp * EXPERTS_PER_GROUP:(grp + 1) * EXPERTS_PER_GROUP, :], el)
    row_e = lax.broadcasted_iota(jnp.int32, (EXPERTS_PER_GROUP, tm), 0).astype(F32)
    v1 = jnp.max(el, axis=0, keepdims=True)
    i1 = jnp.min(jnp.where(el == v1, row_e, big), axis=0, keepdims=True)
    el2 = jnp.where(row_e == i1, NEG, el)
    v2 = jnp.max(el2, axis=0, keepdims=True)
    i2 = jnp.min(jnp.where(el2 == v2, row_e, big), axis=0, keepdims=True)
    t = jnp.exp(v2 - v1)
    w1 = p_sel / (1.0 + t)
    w2 = t * w1

    first_lower = i1 < i2
    a = jnp.minimum(i1, i2)
    b = jnp.maximum(i1, i2)
    code = a * EXPERTS_PER_GROUP + b
    pair = jnp.zeros_like(code)
    a_is_low = code < 0.0
    for idx, (slot_a, slot_b) in enumerate(_PAIR_SLOTS):
        hit = code == float(min(slot_a, slot_b) * EXPERTS_PER_GROUP + max(slot_a, slot_b))
        pair = jnp.where(hit, float(idx), pair)
        if slot_a < slot_b:
            a_is_low = jnp.logical_or(a_is_low, hit)
    bucket = g_idx * PAIRS_PER_GROUP + pair
    w_low = jnp.where(first_lower, w1, w2)
    w_high = jnp.where(first_lower, w2, w1)
    w_lo = jnp.where(a_is_low, w_low, w_high)
    w_hi = jnp.where(a_is_low, w_high, w_low)

    @pl.when(first_step)
    def _():
        cnt_scr[...] = jnp.zeros(cnt_scr.shape, F32)

    row_b = lax.broadcasted_iota(jnp.int32, (BUCKET_ROWS, tm), 0).astype(F32)
    onehot = jnp.where(row_b == bucket, 1.0, 0.0)
    r_i = lax.broadcasted_iota(jnp.int32, (tm, tm), 0)
    c_i = lax.broadcasted_iota(jnp.int32, (tm, tm), 1)
    earlier = jnp.where(r_i < c_i, 1.0, 0.0).astype(BF16)
    prefix = jnp.dot(onehot.astype(BF16), earlier, preferred_element_type=F32)
    counts = cnt_scr[...]
    rank = jnp.sum(onehot * (prefix + counts[:, 0:1]), axis=0, keepdims=True)
    counts = counts + jnp.sum(onehot, axis=1, keepdims=True)
    cnt_scr[...] = counts
    cnt_ref[...] = counts

    def record(rows):
        row = lax.broadcasted_iota(jnp.int32, (rows, tm), 0)
        return jnp.where(row == META_BUCKET, bucket, jnp.where(row == META_RANK, rank, jnp.where(
            row == META_W_LO, w_lo, jnp.where(row == META_W_HI, w_hi, 0.0))))

    rt_ref[...] = record(SUBLANES)
    meta_ref[...] = jnp.transpose(record(LANES))


def _post_even_kernel(cb_ref, p_ref, pprev_ref, pnext_ref, cw_ref, cbias_ref, olat_ref, octx_ref, w_ref,
                      xl_ref, xc_ref, mod_ref, g_ref, wr_ref, br_ref, x1_ref, hf_ref, meta_ref, rt_ref, cnt_ref, cnt_scr):
    i = pl.program_id(1)
    first_step = jnp.logical_and(pl.program_id(0) == 0, i == 0)
    pc = p_ref[...]
    row = lax.broadcasted_iota(jnp.int32, pc.shape, 0)
    has_prev = jnp.logical_and(i != 0, i != N_LAT_TILES)
    has_next = i < N_LAT_TILES - 1
    prev_row = jnp.where(has_prev, pprev_ref[SUBLANES - 1:SUBLANES, :], 0.0)
    next_row = jnp.where(has_next, pnext_ref[0:1, :], 0.0)
    up = jnp.where(row == 0, prev_row, pltpu.roll(pc, 1, axis=0))
    dn = jnp.where(row == TM - 1, next_row, pltpu.roll(pc, TM - 1, axis=0))
    conv = cbias_ref[...] + cw_ref[0:1, :] * up + cw_ref[1:2, :] * pc + cw_ref[2:3, :] * dn
    out_a = (cb_ref[...] * conv).astype(BF16)
    o = jnp.where(i == N_LAT_TILES, octx_ref[...], olat_ref[...])
    x = jnp.where(i == N_LAT_TILES, xc_ref[...], xl_ref[...])
    y = (jnp.dot(out_a, w_ref[0:CONV_WIDTH, :], preferred_element_type=F32)
         + jnp.dot(o, w_ref[CONV_WIDTH:, :], preferred_element_type=F32))
    _post_tail(y, first_step, x, mod_ref, g_ref, wr_ref, br_ref, x1_ref, hf_ref, meta_ref, rt_ref, cnt_ref,
               cnt_scr)


def _post_odd_kernel(y_ref, hf_in_ref, hb_in_ref, w_ref, x_ref, mod_ref, g_ref, wr_ref, br_ref,
                     x1_ref, hf_ref, meta_ref, rt_ref, cnt_ref, cnt_scr):
    first_step = jnp.logical_and(pl.program_id(0) == 0, pl.program_id(1) == 0)
    a = (y_ref[...].astype(F32) * (hf_in_ref[...] + hb_in_ref[...])).astype(BF16)
    y = jnp.dot(a, w_ref[...], preferred_element_type=F32)
    _post_tail(y, first_step, x_ref[...], mod_ref, g_ref, wr_ref, br_ref, x1_ref, hf_ref, meta_ref, rt_ref, cnt_ref,
               cnt_scr)


def _post_specs(bsz, rows):
    tok_spec = lambda width: pl.BlockSpec((None, TM, width), lambda b, i: (b, i, 0))
    const = lambda shape: pl.BlockSpec(shape, lambda b, i: (0,) * len(shape))
    tail_in = [pl.BlockSpec((None, SUBLANES, D_MODEL), _mod_index), const((1, D_MODEL)),
               const((2, D_MODEL, LANES)), const((1, LANES))]
    tiles = rows // TM
    out_specs = [tok_spec(D_MODEL), tok_spec(D_MODEL), tok_spec(LANES),
                 pl.BlockSpec((None, SUBLANES, TM), lambda b, i: (b * tiles + i, 0, 0)),
                 const((BUCKET_ROWS, LANES))]
    out_shape = [jax.ShapeDtypeStruct((bsz, rows, D_MODEL), F32),
                 jax.ShapeDtypeStruct((bsz, rows, D_MODEL), F32),
                 jax.ShapeDtypeStruct((bsz, rows, LANES), F32),
                 jax.ShapeDtypeStruct((bsz * tiles, SUBLANES, TM), F32),
                 jax.ShapeDtypeStruct((BUCKET_ROWS, LANES), F32)]
    scratch = [pltpu.VMEM((BUCKET_ROWS, LANES), F32)]
    return tok_spec, const, tail_in, out_specs, out_shape, scratch


def _post_even(cb, p, conv_w, conv_b, o_lat, o_ctx, w_out, x, ctx, mod, g, wr, br):
    bsz = x.shape[0]
    tok_spec, const, tail_in, out_specs, out_shape, scratch = _post_specs(bsz, TOK)
    halo_blocks = TM // SUBLANES
    last_halo = TOK // SUBLANES - 1
    prev_spec = pl.BlockSpec((None, SUBLANES, CONV_WIDTH),
                             lambda b, i: (b, jnp.maximum(i * halo_blocks - 1, 0), 0))
    next_spec = pl.BlockSpec((None, SUBLANES, CONV_WIDTH),
                             lambda b, i: (b, jnp.minimum((i + 1) * halo_blocks, last_halo), 0))
    return pl.pallas_call(
        _post_even_kernel,
        grid=(bsz, N_TILES),
        in_specs=[tok_spec(CONV_WIDTH), tok_spec(CONV_WIDTH), prev_spec, next_spec,
                  const((SUBLANES, CONV_WIDTH)), const((1, CONV_WIDTH))] + _lat_ctx_specs(DA_WIDTH)
        + [const((D_MODEL, D_MODEL))] + _lat_ctx_specs(D_MODEL) + tail_in,
        out_specs=out_specs,
        out_shape=out_shape,
        scratch_shapes=scratch,
        compiler_params=_cparams(("arbitrary", "arbitrary"), 48),
        name="post_even",
    )(cb, p, p, p, conv_w, conv_b, o_lat, o_ctx, w_out, x, ctx, mod, g, wr, br)


def _post_odd(y, hfw, hbw, w_out, xs, mod, g, wr, br):
    bsz = xs.shape[0]
    tok_spec, const, tail_in, out_specs, out_shape, scratch = _post_specs(bsz, SEQ)
    return pl.pallas_call(
        _post_odd_kernel,
        grid=(bsz, N_LAT_TILES),
        in_specs=[tok_spec(LRU_WIDTH), tok_spec(LRU_WIDTH), tok_spec(LRU_WIDTH),
                  const((LRU_WIDTH, D_MODEL)), tok_spec(D_MODEL)] + tail_in,
        out_specs=out_specs,
        out_shape=out_shape,
        scratch_shapes=scratch,
        compiler_params=_cparams(("arbitrary", "arbitrary"), 48),
        name="post_odd",
    )(y, hfw, hbw, w_out, xs, mod, g, wr, br)


_BUCKET_LO =[g * EXPERTS_PER_GROUP + a for g in range(N_GROUPS) for a, _ in _PAIR_SLOTS]
_BUCKET_HI = [g * EXPERTS_PER_GROUP + b for g in range(N_GROUPS) for _, b in _PAIR_SLOTS]


def _sorted_tiles(n_tokens):
    return n_tokens // TM + N_BUCKETS


def _route_plan(rt, counts, n_tokens):
    n_tiles = _sorted_tiles(n_tokens)
    bucket = rt[:, META_BUCKET, :].astype(jnp.int32).reshape(n_tokens)
    rank = rt[:, META_RANK, :].astype(jnp.int32).reshape(n_tokens)
    cnt = counts[:N_BUCKETS, 0].astype(jnp.int32)
    tiles_per = (cnt + TM - 1) // TM
    tile_end = jnp.cumsum(tiles_per)
    row_start = (tile_end - tiles_per) * TM
    dest = (row_start[bucket] + rank).reshape(n_tokens // TM, 1, TM)
    tile_bucket = jnp.minimum(jnp.sum(jnp.arange(n_tiles)[:, None] >= tile_end[None, :], axis=1), N_BUCKETS - 1)
    e_lo = jnp.asarray(_BUCKET_LO, jnp.int32)[tile_bucket]
    e_hi = jnp.asarray(_BUCKET_HI, jnp.int32)[tile_bucket]
    return dest, e_lo, e_hi, tile_end[-1:].astype(jnp.int32)


DISPATCH_TILES = 2


def _dispatch_kernel(dest_ref, hf_ref, hs_in_hbm, hs_hbm, sem):
    del hs_in_hbm
    for h in range(DISPATCH_TILES):
        for r in range(TM):
            pltpu.make_async_copy(hf_ref.at[pl.ds(h * TM + r, 1), :],
                                  hs_hbm.at[pl.ds(dest_ref[h, 0, r], 1), :], sem).start()
    pltpu.make_async_copy(hf_ref, hs_hbm.at[pl.ds(0, DISPATCH_TILES * TM), :], sem).wait()


def _dispatch(dest, hf, n_tokens):
    rows = _sorted_tiles(n_tokens) * TM
    block = DISPATCH_TILES * TM
    assert n_tokens % block == 0
    return pl.pallas_call(
        _dispatch_kernel,
        grid=(n_tokens // block,),
        in_specs=[pl.BlockSpec((DISPATCH_TILES, 1, TM), lambda i: (i, 0, 0), memory_space=pltpu.SMEM),
                  pl.BlockSpec((block, D_MODEL), lambda i: (i, 0)),
                  pl.BlockSpec(memory_space=pl.ANY)],
        out_specs=pl.BlockSpec(memory_space=pl.ANY),
        out_shape=jax.ShapeDtypeStruct((rows, D_MODEL), F32),
        scratch_shapes=[pltpu.SemaphoreType.DMA(())],
        input_output_aliases={2: 0},
        compiler_params=_cparams(("arbitrary",), 16),
        name="moe_dispatch",
    )(dest, hf, jnp.zeros((rows, D_MODEL), F32))


def _gather_rows(idx_ref, src_hbm, dst_ref, sem):
    for r in range(TM):
        pltpu.make_async_copy(src_hbm.at[pl.ds(idx_ref[0, r], 1), :], dst_ref.at[pl.ds(r, 1), :], sem).start()


def _gather_wait(src_hbm, dst_ref, sem):
    pltpu.make_async_copy(src_hbm.at[pl.ds(0, TM), :], dst_ref, sem).wait()


def _moe_routed_kernel(elo_ref, ehi_ref, nused_ref, h_ref, wg_lo, wu_lo, wd_lo, wg_hi, wu_hi, wd_hi, y_ref):
    del elo_ref, ehi_ref
    j = pl.program_id(0)
    n_used = nused_ref[0]

    @pl.when(j < n_used)
    def _():
        h = h_ref[...].astype(BF16)

        def expert(wg_ref, wu_ref, wd_ref):
            hg = jnp.dot(h, wg_ref[...].astype(BF16), preferred_element_type=F32)
            hu = jnp.dot(h, wu_ref[...].astype(BF16), preferred_element_type=F32)
            act = (hg * _sigmoid(hg)) * hu
            return jnp.dot(act.astype(BF16), wd_ref[...].astype(BF16), preferred_element_type=F32)

        y_ref[:, :D_MODEL] = expert(wg_lo, wu_lo, wd_lo)
        y_ref[:, D_MODEL:] = expert(wg_hi, wu_hi, wd_hi)

    @pl.when(j >= n_used)
    def _():
        y_ref[...] = jnp.zeros(y_ref.shape, F32)


def _moe_routed(hs, e_lo, e_hi, n_used, wg, wu, wd, layer):
    n_tiles = hs.shape[0] // TM
    up_spec = lambda tbl: pl.BlockSpec((None, None, D_MODEL, D_EXPERT),
                                       lambda j, lo, hi, nu: (layer, (lo, hi)[tbl][j], 0, 0))
    dn_spec = lambda tbl: pl.BlockSpec((None, None, D_EXPERT, D_MODEL),
                                       lambda j, lo, hi, nu: (layer, (lo, hi)[tbl][j], 0, 0))
    grid_spec = pltpu.PrefetchScalarGridSpec(
        num_scalar_prefetch=3,
        grid=(n_tiles,),
        in_specs=[pl.BlockSpec((TM, D_MODEL), lambda j, lo, hi, nu: (j, 0)),
                  up_spec(0), up_spec(0), dn_spec(0), up_spec(1), up_spec(1), dn_spec(1)],
        out_specs=pl.BlockSpec((TM, 2 * D_MODEL), lambda j, lo, hi, nu: (j, 0)),
    )
    return pl.pallas_call(
        _moe_routed_kernel,
        grid_spec=grid_spec,
        out_shape=jax.ShapeDtypeStruct((n_tiles * TM, 2 * D_MODEL), F32),
        compiler_params=_cparams(("arbitrary",), 56),
        name="moe_routed",
    )(e_lo, e_hi, n_used, hs, wg, wu, wd, wg, wu, wd)


def _combine_kernel(dest_ref, dest_next_ref, y_hbm, x_ref, meta_ref, mod_ref, o_ref, buf_ref, sems):
    step = pl.program_id(0) * pl.num_programs(1) + pl.program_id(1)
    n_steps = pl.num_programs(0) * pl.num_programs(1)
    slot = step % 2

    @pl.when(step == 0)
    def _():
        _gather_rows(dest_ref, y_hbm, buf_ref.at[0], sems.at[0])

    @pl.when(step + 1 < n_steps)
    def _():
        _gather_rows(dest_next_ref, y_hbm, buf_ref.at[1 - slot], sems.at[1 - slot])

    _gather_wait(y_hbm, buf_ref.at[slot], sems.at[slot])
    meta = meta_ref[...]
    moe = (meta[:, META_W_LO:META_W_LO + 1] * buf_ref[slot, :, :D_MODEL]
           + meta[:, META_W_HI:META_W_HI + 1] * buf_ref[slot, :, D_MODEL:])
    o_ref[...] = x_ref[...] + mod_ref[5:6, :] * moe


def _combine(dest, y, x1, meta, mod, *, tiles_per_batch):
    bsz = x1.shape[0]
    n_steps = bsz * tiles_per_batch
    tok_spec = lambda width: pl.BlockSpec((None, TM, width), lambda b, i: (b, i, 0))
    idx_spec = lambda ahead: pl.BlockSpec(
        (None, 1, TM), lambda b, i: (jnp.minimum(b * tiles_per_batch + i + ahead, n_steps - 1), 0, 0),
        memory_space=pltpu.SMEM)
    return pl.pallas_call(
        _combine_kernel,
        grid=(bsz, tiles_per_batch),
        in_specs=[idx_spec(0), idx_spec(1), pl.BlockSpec(memory_space=pl.ANY), tok_spec(D_MODEL),
                  tok_spec(LANES), pl.BlockSpec((None, SUBLANES, D_MODEL), _mod_index)],
        out_specs=tok_spec(D_MODEL),
        out_shape=jax.ShapeDtypeStruct(x1.shape, F32),
        scratch_shapes=[pltpu.VMEM((2, TM, 2 * D_MODEL), F32), pltpu.SemaphoreType.DMA((2,))],
        compiler_params=_cparams(("arbitrary", "arbitrary"), 32),
        name="moe_combine",
    )(dest, dest, y, x1, meta, mod)


def _moe(hf, meta, rt, counts, x1, mod, wg, wu, wd, layer, *, tiles_per_batch):
    n_tokens = hf.shape[0] * hf.shape[1]
    dest, e_lo, e_hi, n_used = _route_plan(rt, counts, n_tokens)
    hs = _dispatch(dest, hf.reshape(n_tokens, D_MODEL), n_tokens)
    y = _moe_routed(hs, e_lo, e_hi, n_used, wg, wu, wd, layer)
    return _combine(dest, y, x1, meta, mod, tiles_per_batch=tiles_per_batch)


def _inproj_odd_kernel(x_ref, mod_ref, g_ref, w_ref, y_ref, u_ref):
    h = _norm_mod(x_ref[...], g_ref[...], mod_ref[0:1, :], mod_ref[1:2, :])
    z = jnp.dot(h.astype(BF16), w_ref[...], preferred_element_type=F32)
    zy = z[:, :LRU_WIDTH]
    c0 = math.sqrt(2.0 / math.pi)
    y_ref[...] = (0.5 * zy * (1.0 + jnp.tanh(c0 * (zy + 0.044715 * (zy * zy * zy))))).astype(BF16)
    u_ref[...] = z[:, LRU_WIDTH:]


def _inproj_odd(xs, mod, g, w_in):
    bsz = xs.shape[0]
    tok_spec = lambda width: pl.BlockSpec((None, TM, width), lambda b, i: (b, i, 0))
    const = lambda shape: pl.BlockSpec(shape, lambda b, i: (0,) * len(shape))
    return pl.pallas_call(
        _inproj_odd_kernel,
        grid=(bsz, N_TILES),
        in_specs=[tok_spec(D_MODEL), pl.BlockSpec((None, SUBLANES, D_MODEL), _mod_index),
                  const((1, D_MODEL)), const((D_MODEL, 2 * LRU_WIDTH))],
        out_specs=[tok_spec(LRU_WIDTH), tok_spec(LRU_WIDTH)],
        out_shape=[jax.ShapeDtypeStruct((bsz, TOK, LRU_WIDTH), BF16),
                   jax.ShapeDtypeStruct((bsz, TOK, LRU_WIDTH), F32)],
        compiler_params=_cparams(("parallel", "arbitrary"), 48),
        name="inproj_odd",
    )(xs, mod, g, w_in)


def _scan_kernel(u_ref, cw_ref, cbias_ref, wa_ref, ba_ref, wx_ref, bx_ref, lam_ref, h_ref,
                 halo_ref, carry_ref, a_ref, b_ref, *, reverse):
    i = pl.program_id(1)
    tt, w = u_ref.shape
    n_groups = tt // SUBLANES

    @pl.when(i <= 1)
    def _():
        halo_ref[...] = jnp.zeros(halo_ref.shape, F32)

    @pl.when(i == 0)
    def _():
        carry_ref[...] = jnp.zeros(carry_ref.shape, F32)

    u = u_ref[...]
    halo = halo_ref[...]
    row8 = lax.broadcasted_iota(jnp.int32, (SUBLANES, w), 0)
    k_self = 0 if reverse else LRU_CONV_K - 1
    uc = cbias_ref[...] + cw_ref[k_self:k_self + 1, :] * u
    for k in range(1, LRU_CONV_K):
        if reverse:
            tmp = pltpu.roll(u, tt - k, axis=0)
            hr = pltpu.roll(halo, SUBLANES - k, axis=0)
            edge = jnp.where(row8 >= SUBLANES - k, hr, tmp[tt - SUBLANES:, :])
            shifted = jnp.concatenate([tmp[:tt - SUBLANES, :], edge], axis=0)
            wk = cw_ref[k:k + 1, :]
        else:
            tmp = pltpu.roll(u, k, axis=0)
            hr = pltpu.roll(halo, k, axis=0)
            edge = jnp.where(row8 < k, hr, tmp[:SUBLANES, :])
            shifted = jnp.concatenate([edge, tmp[SUBLANES:, :]], axis=0)
            wk = cw_ref[LRU_CONV_K - 1 - k:LRU_CONV_K - k, :]
        uc = uc + wk * shifted
    halo_ref[...] = u[:SUBLANES, :] if reverse else u[tt - SUBLANES:, :]

    ucb = uc.astype(BF16)

    def block_diag(w_blocks):
        return jnp.concatenate(
            [jnp.dot(ucb[:, j * LRU_BLOCK:(j + 1) * LRU_BLOCK], w_blocks[j], preferred_element_type=F32)
             for j in range(LRU_BLOCKS)], axis=1)

    gate_i = 0.5 * jnp.tanh(block_diag(wx_ref) + bx_ref[...]) + 0.5
    neg_lam = -lam_ref[...]
    softplus = jnp.maximum(neg_lam, 0.0) + jnp.log1p(jnp.exp(-jnp.abs(neg_lam)))
    k = (-0.5 * LRU_C * math.log2(math.e)) * softplus
    a = jnp.exp2(k * jnp.tanh(block_diag(wa_ref) + ba_ref[...]) + k)
    a_ref[...] = a
    one_m_a2 = 1.0 - a * a
    root = jnp.where(one_m_a2 > 0.0, one_m_a2 * lax.rsqrt(one_m_a2), 0.0)
    b_ref[...] = root * (gate_i * uc)

    def group(gi, carry):
        g = n_groups - 1 - gi if reverse else gi
        off = pl.multiple_of(g * SUBLANES, SUBLANES)
        av = a_ref[pl.ds(off, SUBLANES), :]
        bv = b_ref[pl.ds(off, SUBLANES), :]
        for s in (1, 2, 4):
            if reverse:
                outside = row8 >= SUBLANES - s
                shift = SUBLANES - s
            else:
                outside = row8 < s
                shift = s
            a_sh = jnp.where(outside, 1.0, pltpu.roll(av, shift, axis=0))
            b_sh = jnp.where(outside, 0.0, pltpu.roll(bv, shift, axis=0))
            bv = av * b_sh + bv
            av = av * a_sh
        h = av * carry + bv
        h_ref[pl.ds(off, SUBLANES), :] = h
        last = h[0:1, :] if reverse else h[SUBLANES - 1:SUBLANES, :]
        return jnp.broadcast_to(last, (SUBLANES, w))

    carry_ref[...] = lax.fori_loop(0, n_groups, group, carry_ref[...])


def _rglru_scan(u, conv_w, conv_b, w_a, b_a, w_x, b_x, lam, *, reverse):
    bsz = u.shape[0]
    if reverse:
        tile_of = lambda i: N_LAT_TILES - i
    else:
        tile_of = lambda i: (i + N_LAT_TILES) % N_TILES
    tok_spec = pl.BlockSpec((None, TM, LRU_WIDTH), lambda b, i: (b, tile_of(i), 0))
    const = lambda shape: pl.BlockSpec(shape, lambda b, i: (0,) * len(shape))
    return pl.pallas_call(
        functools.partial(_scan_kernel, reverse=reverse),
        grid=(bsz, N_TILES),
        in_specs=[tok_spec, const((SUBLANES, LRU_WIDTH)), const((1, LRU_WIDTH)),
                  const((LRU_BLOCKS, LRU_BLOCK, LRU_BLOCK)), const((1, LRU_WIDTH)),
                  const((LRU_BLOCKS, LRU_BLOCK, LRU_BLOCK)), const((1, LRU_WIDTH)), const((1, LRU_WIDTH))],
        out_specs=tok_spec,
        out_shape=jax.ShapeDtypeStruct((bsz, TOK, LRU_WIDTH), F32),
        scratch_shapes=[pltpu.VMEM((SUBLANES, LRU_WIDTH), F32), pltpu.VMEM((SUBLANES, LRU_WIDTH), F32),
                        pltpu.VMEM((TM, LRU_WIDTH), F32), pltpu.VMEM((TM, LRU_WIDTH), F32)],
        compiler_params=_cparams(("parallel", "arbitrary"), 32),
        name="rglru_rev" if reverse else "rglru_fwd",
    )(u, conv_w, conv_b, (0.5 * w_a).astype(BF16), 0.5 * b_a, (0.5 * w_x).astype(BF16), 0.5 * b_x, lam)


def _pad_rows(a, rows):
    return jnp.pad(a, ((0, rows - a.shape[0]), (0, 0)))


def _rope_tables():
    t = jnp.arange(SEQ)
    n_freq = DA_HEAD_DIM // 4
    inv = ROPE_BASE ** (-jnp.arange(n_freq, dtype=F32) / n_freq)
    ang = jnp.concatenate([(t // GRID_W).astype(F32)[:, None] * inv,
                           (t % GRID_W).astype(F32)[:, None] * inv], axis=-1)
    cos, sin = jnp.cos(ang), jnp.sin(ang)
    cos64 = jnp.concatenate([cos, cos], axis=-1)
    sin64 = jnp.concatenate([-sin, sin], axis=-1)
    cos_t = jnp.concatenate([jnp.tile(cos64, (1, 2)), jnp.ones((CTX_LEN, LANES), F32)], axis=0)
    sin_t = jnp.concatenate([jnp.tile(sin64, (1, 2)), jnp.zeros((CTX_LEN, LANES), F32)], axis=0)
    return cos_t, sin_t


def _router_params(w_grp, b_grp, w_rt, b_rt):
    wr = jnp.concatenate([w_rt.reshape(D_MODEL, N_EXPERTS), w_grp], axis=1)
    br = jnp.concatenate([b_rt.reshape(N_EXPERTS), b_grp])
    pad = LANES - wr.shape[1]
    wr = jnp.pad(wr, ((0, 0), (0, pad)))
    wr_hi = wr.astype(BF16)
    wr_lo = (wr - wr_hi.astype(F32)).astype(BF16)
    return jnp.stack([wr_hi, wr_lo]), jnp.pad(br, (0, pad)).reshape(1, LANES)


def kernel(x, c, ctx, c_ctx, ada_w, ada_b, norm_mix, norm_ffn, ev_w_in, ev_conv_w, ev_conv_b, ev_q_norm, ev_k_norm, ev_lam_q1, ev_lam_k1, ev_lam_q2, ev_lam_k2, ev_sub_norm, ev_w_out, od_w_in, od_conv_w, od_conv_b, od_w_a, od_b_a, od_w_x, od_b_x, od_lam, od_w_out, moe_w_grp, moe_b_grp, moe_w_rt, moe_b_rt, moe_w_gate, moe_w_up, moe_w_down):
    bsz = x.shape[0]
    assert x.shape == (bsz, SEQ, D_MODEL) and ctx.shape == (bsz, CTX_LEN, D_MODEL) and bsz == 2
    depth = ada_w.shape[0]
    assert depth == 2


    cvec = _pad_rows(jnp.stack([c[0], c_ctx, c[1], c_ctx]), SUBLANES)
    mod_all = _ada_mod(cvec, ada_w, ada_b)
    mod_all = mod_all[:, :2 * bsz].reshape(depth, 2 * bsz, 6, D_MODEL)
    mod_all = jnp.pad(mod_all, ((0, 0), (0, 0), (0, SUBLANES - 6), (0, 0)))

    l = 0
    lam_init = 0.8 - 0.6 * math.exp(-0.3 * l)
    mod = mod_all[l]
    cos_t, sin_t = _rope_tables()
    blk = jnp.arange(DA_WIDTH) // DA_HEAD_DIM
    bd = jnp.where(blk[:, None] == blk[None, :], 1.0 / DA_HEAD_DIM, 0.0).astype(BF16)
    n_rep = DA_WIDTH // DA_HEAD_DIM
    cb, p, q, k, v = _inproj_even(
        x, ctx, mod, norm_mix[l].reshape(1, D_MODEL), ev_w_in[0].astype(BF16),
        jnp.tile(ev_q_norm[0], n_rep).reshape(1, DA_WIDTH), jnp.tile(ev_k_norm[0], n_rep).reshape(1, DA_WIDTH),
        cos_t, sin_t, bd)

    lamv = _pad_rows(jnp.pad(jnp.stack([ev_lam_q1[0], ev_lam_k1[0], ev_lam_q2[0], ev_lam_k2[0]]),
                             ((0, 0), (0, LANES - DA_HEAD_DIM))), SUBLANES)
    subg = ev_sub_norm[0].reshape(1, DA_V_DIM)
    o_lat = _attention(q, k, v, lamv, subg, tq=ATTN_TQ, q_start=0, n_q=SEQ // ATTN_TQ,
                       kv_start=0, kv_len=TOK, tk=ATTN_TK, lam_init=lam_init)
    o_ctx = _attention(q, k, v, lamv, subg, tq=TM, q_start=SEQ, n_q=CTX_LEN // TM,
                       kv_start=SEQ, kv_len=CTX_LEN, tk=CTX_LEN, lam_init=lam_init)

    wr, br = _router_params(moe_w_grp[l], moe_b_grp[l], moe_w_rt[l], moe_b_rt[l])
    x1, hf, meta, rt, counts = _post_even(
        cb, p, _pad_rows(ev_conv_w[0], SUBLANES), ev_conv_b[0].reshape(1, CONV_WIDTH), o_lat, o_ctx,
        ev_w_out[0].astype(BF16), x, ctx, mod, norm_ffn[l].reshape(1, D_MODEL), wr, br)
    xs = _moe(hf, meta, rt, counts, x1, mod, moe_w_gate, moe_w_up, moe_w_down, l, tiles_per_batch=N_TILES)

    l = 1
    mod = mod_all[l]
    y, u = _inproj_odd(xs, mod, norm_mix[l].reshape(1, D_MODEL), od_w_in[0].astype(BF16))
    h_dirs = []
    for d in range(2):
        h_dirs.append(_rglru_scan(
            u, _pad_rows(od_conv_w[0, d], SUBLANES), od_conv_b[0, d].reshape(1, LRU_WIDTH),
            od_w_a[0, d], od_b_a[0, d].reshape(1, LRU_WIDTH),
            od_w_x[0, d], od_b_x[0, d].reshape(1, LRU_WIDTH),
            od_lam[0, d].reshape(1, LRU_WIDTH), reverse=bool(d)))
    wr, br = _router_params(moe_w_grp[l], moe_b_grp[l], moe_w_rt[l], moe_b_rt[l])
    x1, hf, meta, rt, counts = _post_odd(y, h_dirs[0], h_dirs[1], od_w_out[0].astype(BF16), xs, mod,
                                     norm_ffn[l].reshape(1, D_MODEL), wr, br)
    return _moe(hf, meta, rt, counts, x1, mod, moe_w_gate, moe_w_up, moe_w_down, l,
                tiles_per_batch=N_LAT_TILES)
```

```python
import functools
import math

import jax
import jax.numpy as jnp
from jax import lax
from jax.experimental import pallas as pl
from jax.experimental.pallas import tpu as pltpu

F32 = jnp.float32
BF16 = jnp.bfloat16

D_MODEL = 1024
SEQ = 8192
CTX_LEN = 256
TOK = SEQ + CTX_LEN
GRID_W = 64
EPS = 1e-6

CONV_WIDTH = 512
DA_HEADS = 4
DA_HEAD_DIM = 64
DA_V_DIM = 2 * DA_HEAD_DIM
DA_WIDTH = DA_HEADS * DA_V_DIM
EVEN_IN = 3 * CONV_WIDTH + 3 * DA_WIDTH
ROPE_BASE = 10000.0

LRU_WIDTH = 1024
LRU_BLOCKS = 8
LRU_BLOCK = LRU_WIDTH // LRU_BLOCKS
LRU_CONV_K = 4
LRU_C = 8.0

N_GROUPS = 4
EXPERTS_PER_GROUP = 4
N_EXPERTS = N_GROUPS * EXPERTS_PER_GROUP
D_EXPERT = 512

LANES = 128
SUBLANES = 8
TM = 256
N_LAT_TILES = SEQ // TM
N_TILES = TOK // TM
ATTN_TQ = 512
ATTN_TK = 768
NEG = -1e30
MIB = 2 ** 20


def _cparams(semantics, vmem_mib):
    return pltpu.CompilerParams(dimension_semantics=semantics, vmem_limit_bytes=vmem_mib * MIB)


def _sigmoid(x):
    return 0.5 * jnp.tanh(0.5 * x) + 0.5


def _split_bf16(x):
    hi = x.astype(BF16)
    return hi, (x - hi.astype(F32)).astype(BF16)


def _dot_f32_grade(x, w):
    x_hi, x_lo = _split_bf16(x)
    w_hi, w_lo = _split_bf16(w)
    return (jnp.dot(x_hi, w_hi, preferred_element_type=F32) + jnp.dot(x_lo, w_hi, preferred_element_type=F32)
            + jnp.dot(x_hi, w_lo, preferred_element_type=F32))


def _norm_mod(x, g, shift, scale):
    ms = jnp.mean(x * x, axis=-1, keepdims=True)
    return (x * lax.rsqrt(ms + EPS) * g) * (1.0 + scale) + shift


ADA_TN = 1536


def _ada_kernel(c_ref, w_ref, b_ref, o_ref):
    c = c_ref[...]
    a = c * _sigmoid(c)
    o_ref[...] = _dot_f32_grade(a, w_ref[...]) + b_ref[...]


def _ada_mod(cvec, ada_w, ada_b):
    depth, d, n = ada_w.shape
    return pl.pallas_call(
        _ada_kernel,
        grid=(depth, n // ADA_TN),
        in_specs=[
            pl.BlockSpec((SUBLANES, d), lambda l, j: (0, 0)),
            pl.BlockSpec((None, d, ADA_TN), lambda l, j: (l, 0, j)),
            pl.BlockSpec((None, 1, ADA_TN), lambda l, j: (l, 0, j)),
        ],
        out_specs=pl.BlockSpec((None, SUBLANES, ADA_TN), lambda l, j: (l, 0, j)),
        out_shape=jax.ShapeDtypeStruct((depth, SUBLANES, n), F32),
        compiler_params=_cparams(("arbitrary", "arbitrary"), 40),
        name="ada_mod",
    )(cvec, ada_w, ada_b.reshape(depth, 1, n))


def _mod_index(b, i):
    return (b * 2 + i // N_LAT_TILES, 0, 0)


def _inproj_even_kernel(xl_ref, xc_ref, mod_ref, g_ref, w_ref, qg_ref, kg_ref, cos_ref, sin_ref, bd_ref,
                        cb_ref, p_ref, q_ref, k_ref, v_ref):
    x = jnp.where(pl.program_id(1) == N_LAT_TILES, xc_ref[...], xl_ref[...])
    h = _norm_mod(x, g_ref[...], mod_ref[0:1, :], mod_ref[1:2, :])
    z = jnp.dot(h.astype(BF16), w_ref[...], preferred_element_type=F32)
    cw = CONV_WIDTH
    cb_ref[...] = z[:, :cw]
    p_ref[...] = z[:, cw:2 * cw] * z[:, 2 * cw:3 * cw]

    reps = DA_WIDTH // LANES
    cosf = jnp.concatenate([cos_ref[...]] * reps, axis=1)
    sinf = jnp.concatenate([sin_ref[...]] * reps, axis=1)
    lane = lax.broadcasted_iota(jnp.int32, (TM, DA_WIDTH), 1)
    first_half = (lane & (DA_HEAD_DIM - 1)) < DA_HEAD_DIM // 2
    bd = bd_ref[...]

    def head_norm_rope(t, gain):
        ms = jnp.dot((t * t).astype(BF16), bd, preferred_element_type=F32)
        y = t * lax.rsqrt(ms + EPS) * gain
        fwd = pltpu.roll(y, DA_WIDTH - DA_HEAD_DIM // 2, axis=1)
        bwd = pltpu.roll(y, DA_HEAD_DIM // 2, axis=1)
        return y * cosf + jnp.where(first_half, fwd, bwd) * sinf

    base = 3 * cw
    q = head_norm_rope(z[:, base:base + DA_WIDTH], qg_ref[...])
    q_ref[...] = (q * (DA_HEAD_DIM ** -0.5 * math.log2(math.e))).astype(BF16)
    k = head_norm_rope(z[:, base + DA_WIDTH:base + 2 * DA_WIDTH], kg_ref[...])
    k_ref[...] = k.astype(BF16)
    v = z[:, base + 2 * DA_WIDTH:].astype(BF16)
    ones = jnp.ones((TM, DA_V_DIM), BF16)
    v_ref[...] = jnp.concatenate(
        [blk for h in range(DA_HEADS) for blk in (v[:, h * DA_V_DIM:(h + 1) * DA_V_DIM], ones)], axis=1)


def _lat_ctx_specs(width):
    return [pl.BlockSpec((None, TM, width), lambda b, i: (b, jnp.minimum(i, N_LAT_TILES - 1), 0)),
            pl.BlockSpec((None, TM, width), lambda b, i: (b, 0, 0))]


def _inproj_even(x, ctx, mod, g, w_in, qg, kg, cos, sin, bd):
    bsz = x.shape[0]
    tok_spec = lambda width: pl.BlockSpec((None, TM, width), lambda b, i: (b, i, 0))
    const = lambda shape: pl.BlockSpec(shape, lambda b, i: (0,) * len(shape))
    return pl.pallas_call(
        _inproj_even_kernel,
        grid=(bsz, N_TILES),
        in_specs=_lat_ctx_specs(D_MODEL) + [
            pl.BlockSpec((None, SUBLANES, D_MODEL), _mod_index),
            const((1, D_MODEL)),
            const((D_MODEL, EVEN_IN)),
            const((1, DA_WIDTH)),
            const((1, DA_WIDTH)),
            pl.BlockSpec((TM, LANES), lambda b, i: (i, 0)),
            pl.BlockSpec((TM, LANES), lambda b, i: (i, 0)),
            const((DA_WIDTH, DA_WIDTH)),
        ],
        out_specs=[tok_spec(CONV_WIDTH), tok_spec(CONV_WIDTH), tok_spec(DA_WIDTH), tok_spec(DA_WIDTH),
                   tok_spec(2 * DA_WIDTH)],
        out_shape=[
            jax.ShapeDtypeStruct((bsz, TOK, CONV_WIDTH), F32),
            jax.ShapeDtypeStruct((bsz, TOK, CONV_WIDTH), F32),
            jax.ShapeDtypeStruct((bsz, TOK, DA_WIDTH), BF16),
            jax.ShapeDtypeStruct((bsz, TOK, DA_WIDTH), BF16),
            jax.ShapeDtypeStruct((bsz, TOK, 2 * DA_WIDTH), BF16),
        ],
        compiler_params=_cparams(("parallel", "arbitrary"), 48),
        name="inproj_even",
    )(x, ctx, mod, g, w_in, qg, kg, cos, sin, bd)


def _attn_kernel(lam_ref, subg_ref, q_ref, k_ref, v_ref, o_ref, qz_ref, m_ref, acc_ref,
                 sa_ref, sb_ref, *, tq, tk, nk, lam_init):
    q = q_ref[...]
    lane = lax.broadcasted_iota(jnp.int32, (tq, LANES), 1)
    zero = jnp.zeros_like(q)
    qz_ref[0:tq, :] = jnp.where(lane < DA_HEAD_DIM, q, zero)
    qz_ref[tq:2 * tq, :] = jnp.where(lane >= DA_HEAD_DIM, q, zero)
    m_ref[...] = jnp.full(m_ref.shape, NEG, F32)
    acc_ref[...] = jnp.zeros(acc_ref.shape, F32)

    def scores(j, dst_ref):
        off = pl.multiple_of(j * tk, tk)
        dst_ref[...] = lax.dot_general(qz_ref[...], k_ref[pl.ds(off, tk), :], (((1,), (1,)), ((), ())),
                                       preferred_element_type=F32)

    def update(j, src_ref):
        off = pl.multiple_of(j * tk, tk)
        s = src_ref[...]
        m_prev = m_ref[...]
        m_new = jnp.maximum(m_prev, jnp.max(s, axis=1, keepdims=True))
        alpha = jnp.exp2(m_prev - m_new)
        p = jnp.exp2((s - m_new[:, :1]).astype(BF16))
        pv = jnp.dot(p, v_ref[pl.ds(off, tk), :], preferred_element_type=F32)
        acc_ref[:, :DA_V_DIM] = alpha * acc_ref[:, :DA_V_DIM] + pv[:, :DA_V_DIM]
        acc_ref[:, DA_V_DIM:] = alpha * acc_ref[:, DA_V_DIM:] + pv[:, DA_V_DIM:]
        m_ref[...] = m_new

    bufs = (sa_ref, sb_ref)
    scores(0, bufs[0])
    for j in range(nk):
        if j + 1 < nk:
            scores(j + 1, bufs[(j + 1) % 2])
        update(j, bufs[j % 2])

    o = acc_ref[:, :DA_V_DIM] / acc_ref[:, DA_V_DIM:]
    lv = lam_ref[...]
    lam = (jnp.exp(jnp.sum(lv[0:1, :] * lv[1:2, :], axis=1, keepdims=True))
           - jnp.exp(jnp.sum(lv[2:3, :] * lv[3:4, :], axis=1, keepdims=True)) + lam_init)
    d = o[:tq] - lam * o[tq:]
    ms = jnp.mean(d * d, axis=1, keepdims=True)
    o_ref[...] = (d * lax.rsqrt(ms + EPS) * subg_ref[...] * (1.0 - lam_init)).astype(BF16)


def _attention(q, k, v, lamv, subg, *, tq, q_start, n_q, kv_start, kv_len, tk, lam_init):
    bsz = q.shape[0]
    q_blk0 = q_start // tq
    kv_blk = kv_start // kv_len
    q_spec = pl.BlockSpec((None, tq, LANES), lambda b, h, i: (b, q_blk0 + i, h))
    o_spec = pl.BlockSpec((None, tq, LANES), lambda b, h, i: (b, i, h))
    k_spec = pl.BlockSpec((None, kv_len, LANES), lambda b, h, i: (b, kv_blk, h))
    v_spec = pl.BlockSpec((None, kv_len, 2 * DA_V_DIM), lambda b, h, i: (b, kv_blk, h))
    const = lambda shape: pl.BlockSpec(shape, lambda b, h, i: (0,) * len(shape))
    return pl.pallas_call(
        functools.partial(_attn_kernel, tq=tq, tk=tk, nk=kv_len // tk, lam_init=lam_init),
        grid=(bsz, DA_HEADS, n_q),
        in_specs=[const((SUBLANES, LANES)), const((1, LANES)), q_spec, k_spec, v_spec],
        out_specs=o_spec,
        out_shape=jax.ShapeDtypeStruct((bsz, n_q * tq, DA_WIDTH), BF16),
        scratch_shapes=[
            pltpu.VMEM((2 * tq, LANES), BF16),
            pltpu.VMEM((2 * tq, LANES), F32),
            pltpu.VMEM((2 * tq, 2 * DA_V_DIM), F32),
            pltpu.VMEM((2 * tq, tk), F32),
            pltpu.VMEM((2 * tq, tk), F32),
        ],
        compiler_params=_cparams(("parallel", "parallel", "arbitrary"), 48),
        name="diff_attn",
    )(lamv, subg, q, k, v)


_PAIR_SLOTS = [(0, 1), (2, 1), (2, 0), (3, 0), (3, 1), (3, 2)]
PAIRS_PER_GROUP = len(_PAIR_SLOTS)
assert PAIRS_PER_GROUP == EXPERTS_PER_GROUP * (EXPERTS_PER_GROUP - 1) // 2
N_BUCKETS = N_GROUPS * PAIRS_PER_GROUP
BUCKET_ROWS = -(-N_BUCKETS // SUBLANES) * SUBLANES
META_BUCKET, META_RANK, META_W_LO, META_W_HI = 0, 1, 2, 3


def _post_tail(y, first_step, x, mod_ref, g_ref, wr_ref, br_ref, x1_ref, hf_ref, meta_ref, rt_ref, cnt_ref,
               cnt_scr):
    x1 = x + mod_ref[2:3, :] * y
    x1_ref[...] = x1
    hf = _norm_mod(x1, g_ref[...], mod_ref[3:4, :], mod_ref[4:5, :])
    hf_ref[...] = hf

    hf_hi = hf.astype(BF16)
    hf_lo = (hf - hf_hi.astype(F32)).astype(BF16)
    logits = (jnp.dot(hf_hi, wr_ref[0], preferred_element_type=F32)
              + jnp.dot(hf_lo, wr_ref[0], preferred_element_type=F32)
              + jnp.dot(hf_hi, wr_ref[1], preferred_element_type=F32)) + br_ref[...]
    tm = logits.shape[0]
    lt = jnp.transpose(logits)
    big = float(LANES)
    e_all = lt[0:N_EXPERTS, :]
    g = lt[N_EXPERTS:N_EXPERTS + N_GROUPS, :]
    row_g = lax.broadcasted_iota(jnp.int32, (N_GROUPS, tm), 0).astype(F32)
    gm = jnp.max(g, axis=0, keepdims=True)
    g_idx = jnp.min(jnp.where(g == gm, row_g, big), axis=0, keepdims=True)
    p_sel = 1.0 / jnp.sum(jnp.exp(g - gm), axis=0, keepdims=True)
    el = e_all[0:EXPERTS_PER_GROUP, :]
    for grp in range(1, N_GROUPS):
        el = jnp.where(g_idx == float(grp), e_all[grp * EXPERTS_PER_GROUP:(grp + 1) * EXPERTS_PER_GROUP, :], el)
    row_e = lax.broadcasted_iota(jnp.int32, (EXPERTS_PER_GROUP, tm), 0).astype(F32)
    v1 = jnp.max(el, axis=0, keepdims=True)
    i1 = jnp.min(jnp.where(el == v1, row_e, big), axis=0, keepdims=True)
    el2 = jnp.where(row_e == i1, NEG, el)
    v2 = jnp.max(el2, axis=0, keepdims=True)
    i2 = jnp.min(jnp.where(el2 == v2, row_e, big), axis=0, keepdims=True)
    t = jnp.exp(v2 - v1)
    w1 = p_sel / (1.0 + t)
    w2 = t * w1

    first_lower = i1 < i2
    a = jnp.minimum(i1, i2)
    b = jnp.maximum(i1, i2)
    code = a * EXPERTS_PER_GROUP + b
    pair = jnp.zeros_like(code)
    a_is_low = code < 0.0
    for idx, (slot_a, slot_b) in enumerate(_PAIR_SLOTS):
        hit = code == float(min(slot_a, slot_b) * EXPERTS_PER_GROUP + max(slot_a, slot_b))
        pair = jnp.where(hit, float(idx), pair)
        if slot_a < slot_b:
            a_is_low = jnp.logical_or(a_is_low, hit)
    bucket = g_idx * PAIRS_PER_GROUP + pair
    w_low = jnp.where(first_lower, w1, w2)
    w_high = jnp.where(first_lower, w2, w1)
    w_lo = jnp.where(a_is_low, w_low, w_high)
    w_hi = jnp.where(a_is_low, w_high, w_low)

    @pl.when(first_step)
    def _():
        cnt_scr[...] = jnp.zeros(cnt_scr.shape, F32)

    row_b = lax.broadcasted_iota(jnp.int32, (BUCKET_ROWS, tm), 0).astype(F32)
    onehot = jnp.where(row_b == bucket, 1.0, 0.0)
    r_i = lax.broadcasted_iota(jnp.int32, (tm, tm), 0)
    c_i = lax.broadcasted_iota(jnp.int32, (tm, tm), 1)
    earlier = jnp.where(r_i < c_i, 1.0, 0.0).astype(BF16)
    prefix = jnp.dot(onehot.astype(BF16), earlier, preferred_element_type=F32)
    counts = cnt_scr[...]
    rank = jnp.sum(onehot * (prefix + counts[:, 0:1]), axis=0, keepdims=True)
    counts = counts + jnp.sum(onehot, axis=1, keepdims=True)
    cnt_scr[...] = counts
    cnt_ref[...] = counts

    def record(rows):
        row = lax.broadcasted_iota(jnp.int32, (rows, tm), 0)
        return jnp.where(row == META_BUCKET, bucket, jnp.where(row == META_RANK, rank, jnp.where(
            row == META_W_LO, w_lo, jnp.where(row == META_W_HI, w_hi, 0.0))))

    rt_ref[...] = record(SUBLANES)
    meta_ref[...] = jnp.transpose(record(LANES))


def _post_even_kernel(cb_ref, p_ref, pprev_ref, pnext_ref, cw_ref, cbias_ref, olat_ref, octx_ref, w_ref,
                      xl_ref, xc_ref, mod_ref, g_ref, wr_ref, br_ref, x1_ref, hf_ref, meta_ref, rt_ref, cnt_ref, cnt_scr):
    i = pl.program_id(1)
    first_step = jnp.logical_and(pl.program_id(0) == 0, i == 0)
    pc = p_ref[...]
    row = lax.broadcasted_iota(jnp.int32, pc.shape, 0)
    has_prev = jnp.logical_and(i != 0, i != N_LAT_TILES)
    has_next = i < N_LAT_TILES - 1
    prev_row = jnp.where(has_prev, pprev_ref[SUBLANES - 1:SUBLANES, :], 0.0)
    next_row = jnp.where(has_next, pnext_ref[0:1, :], 0.0)
    up = jnp.where(row == 0, prev_row, pltpu.roll(pc, 1, axis=0))
    dn = jnp.where(row == TM - 1, next_row, pltpu.roll(pc, TM - 1, axis=0))
    conv = cbias_ref[...] + cw_ref[0:1, :] * up + cw_ref[1:2, :] * pc + cw_ref[2:3, :] * dn
    out_a = (cb_ref[...] * conv).astype(BF16)
    o = jnp.where(i == N_LAT_TILES, octx_ref[...], olat_ref[...])
    x = jnp.where(i == N_LAT_TILES, xc_ref[...], xl_ref[...])
    y = (jnp.dot(out_a, w_ref[0:CONV_WIDTH, :], preferred_element_type=F32)
         + jnp.dot(o, w_ref[CONV_WIDTH:, :], preferred_element_type=F32))
    _post_tail(y, first_step, x, mod_ref, g_ref, wr_ref, br_ref, x1_ref, hf_ref, meta_ref, rt_ref, cnt_ref,
               cnt_scr)


def _post_odd_kernel(y_ref, hsum_ref, w_ref, x_ref, mod_ref, g_ref, wr_ref, br_ref,
                     x1_ref, hf_ref, meta_ref, rt_ref, cnt_ref, cnt_scr):
    first_step = jnp.logical_and(pl.program_id(0) == 0, pl.program_id(1) == 0)
    a = (y_ref[...].astype(F32) * hsum_ref[...]).astype(BF16)
    y = jnp.dot(a, w_ref[...], preferred_element_type=F32)
    _post_tail(y, first_step, x_ref[...], mod_ref, g_ref, wr_ref, br_ref, x1_ref, hf_ref, meta_ref, rt_ref, cnt_ref,
               cnt_scr)


def _post_specs(bsz, rows):
    tok_spec = lambda width: pl.BlockSpec((None, TM, width), lambda b, i: (b, i, 0))
    const = lambda shape: pl.BlockSpec(shape, lambda b, i: (0,) * len(shape))
    tail_in = [pl.BlockSpec((None, SUBLANES, D_MODEL), _mod_index), const((1, D_MODEL)),
               const((2, D_MODEL, LANES)), const((1, LANES))]
    tiles = rows // TM
    out_specs = [tok_spec(D_MODEL), tok_spec(D_MODEL), tok_spec(LANES),
                 pl.BlockSpec((None, SUBLANES, TM), lambda b, i: (b * tiles + i, 0, 0)),
                 const((BUCKET_ROWS, LANES))]
    out_shape = [jax.ShapeDtypeStruct((bsz, rows, D_MODEL), F32),
                 jax.ShapeDtypeStruct((bsz, rows, D_MODEL), F32),
                 jax.ShapeDtypeStruct((bsz, rows, LANES), F32),
                 jax.ShapeDtypeStruct((bsz * tiles, SUBLANES, TM), F32),
                 jax.ShapeDtypeStruct((BUCKET_ROWS, LANES), F32)]
    scratch = [pltpu.VMEM((BUCKET_ROWS, LANES), F32)]
    return tok_spec, const, tail_in, out_specs, out_shape, scratch


def _post_even(cb, p, conv_w, conv_b, o_lat, o_ctx, w_out, x, ctx, mod, g, wr, br):
    bsz = x.shape[0]
    tok_spec, const, tail_in, out_specs, out_shape, scratch = _post_specs(bsz, TOK)
    halo_blocks = TM // SUBLANES
    last_halo = TOK // SUBLANES - 1
    prev_spec = pl.BlockSpec((None, SUBLANES, CONV_WIDTH),
                             lambda b, i: (b, jnp.maximum(i * halo_blocks - 1, 0), 0))
    next_spec = pl.BlockSpec((None, SUBLANES, CONV_WIDTH),
                             lambda b, i: (b, jnp.minimum((i + 1) * halo_blocks, last_halo), 0))
    return pl.pallas_call(
        _post_even_kernel,
        grid=(bsz, N_TILES),
        in_specs=[tok_spec(CONV_WIDTH), tok_spec(CONV_WIDTH), prev_spec, next_spec,
                  const((SUBLANES, CONV_WIDTH)), const((1, CONV_WIDTH))] + _lat_ctx_specs(DA_WIDTH)
        + [const((D_MODEL, D_MODEL))] + _lat_ctx_specs(D_MODEL) + tail_in,
        out_specs=out_specs,
        out_shape=out_shape,
        scratch_shapes=scratch,
        compiler_params=_cparams(("arbitrary", "arbitrary"), 48),
        name="post_even",
    )(cb, p, p, p, conv_w, conv_b, o_lat, o_ctx, w_out, x, ctx, mod, g, wr, br)


def _post_odd(y, hsum, w_out, xs, mod, g, wr, br):
    bsz = xs.shape[0]
    tok_spec, const, tail_in, out_specs, out_shape, scratch = _post_specs(bsz, SEQ)
    return pl.pallas_call(
        _post_odd_kernel,
        grid=(bsz, N_LAT_TILES),
        in_specs=[tok_spec(LRU_WIDTH), tok_spec(LRU_WIDTH),
                  const((LRU_WIDTH, D_MODEL)), tok_spec(D_MODEL)] + tail_in,
        out_specs=out_specs,
        out_shape=out_shape,
        scratch_shapes=scratch,
        compiler_params=_cparams(("arbitrary", "arbitrary"), 48),
        name="post_odd",
    )(y, hsum, w_out, xs, mod, g, wr, br)


_BUCKET_LO =[g * EXPERTS_PER_GROUP + a for g in range(N_GROUPS) for a, _ in _PAIR_SLOTS]
_BUCKET_HI = [g * EXPERTS_PER_GROUP + b for g in range(N_GROUPS) for _, b in _PAIR_SLOTS]


def _sorted_tiles(n_tokens):
    return n_tokens // TM + N_BUCKETS


def _route_plan(rt, counts, n_tokens):
    n_tiles = _sorted_tiles(n_tokens)
    bucket = rt[:, META_BUCKET, :].astype(jnp.int32).reshape(n_tokens)
    rank = rt[:, META_RANK, :].astype(jnp.int32).reshape(n_tokens)
    cnt = counts[:N_BUCKETS, 0].astype(jnp.int32)
    tiles_per = (cnt + TM - 1) // TM
    tile_end = jnp.cumsum(tiles_per)
    row_start = (tile_end - tiles_per) * TM
    dest = (row_start[bucket] + rank).reshape(n_tokens // TM, 1, TM)
    tile_bucket = jnp.minimum(jnp.sum(jnp.arange(n_tiles)[:, None] >= tile_end[None, :], axis=1), N_BUCKETS - 1)
    e_lo = jnp.asarray(_BUCKET_LO, jnp.int32)[tile_bucket]
    e_hi = jnp.asarray(_BUCKET_HI, jnp.int32)[tile_bucket]
    return dest, e_lo, e_hi, tile_end[-1:].astype(jnp.int32)


DISPATCH_TILES = 2


def _dispatch_kernel(dest_ref, hf_ref, hs_in_hbm, hs_hbm, sem):
    del hs_in_hbm
    for h in range(DISPATCH_TILES):
        for r in range(TM):
            pltpu.make_async_copy(hf_ref.at[pl.ds(h * TM + r, 1), :],
                                  hs_hbm.at[pl.ds(dest_ref[h, 0, r], 1), :], sem).start()
    pltpu.make_async_copy(hf_ref, hs_hbm.at[pl.ds(0, DISPATCH_TILES * TM), :], sem).wait()


def _dispatch(dest, hf, n_tokens):
    rows = _sorted_tiles(n_tokens) * TM
    block = DISPATCH_TILES * TM
    assert n_tokens % block == 0
    return pl.pallas_call(
        _dispatch_kernel,
        grid=(n_tokens // block,),
        in_specs=[pl.BlockSpec((DISPATCH_TILES, 1, TM), lambda i: (i, 0, 0), memory_space=pltpu.SMEM),
                  pl.BlockSpec((block, D_MODEL), lambda i: (i, 0)),
                  pl.BlockSpec(memory_space=pl.ANY)],
        out_specs=pl.BlockSpec(memory_space=pl.ANY),
        out_shape=jax.ShapeDtypeStruct((rows, D_MODEL), F32),
        scratch_shapes=[pltpu.SemaphoreType.DMA(())],
        input_output_aliases={2: 0},
        compiler_params=_cparams(("arbitrary",), 16),
        name="moe_dispatch",
    )(dest, hf, jnp.zeros((rows, D_MODEL), F32))


def _gather_rows(idx_ref, src_hbm, dst_ref, sem):
    for r in range(TM):
        pltpu.make_async_copy(src_hbm.at[pl.ds(idx_ref[0, r], 1), :], dst_ref.at[pl.ds(r, 1), :], sem).start()


def _gather_wait(src_hbm, dst_ref, sem):
    pltpu.make_async_copy(src_hbm.at[pl.ds(0, TM), :], dst_ref, sem).wait()


def _moe_routed_kernel(elo_ref, ehi_ref, nused_ref, h_ref, wg_lo, wu_lo, wd_lo, wg_hi, wu_hi, wd_hi, y_ref):
    del elo_ref, ehi_ref
    j = pl.program_id(0)
    n_used = nused_ref[0]

    @pl.when(j < n_used)
    def _():
        h = h_ref[...].astype(BF16)

        def expert(wg_ref, wu_ref, wd_ref):
            hg = jnp.dot(h, wg_ref[...].astype(BF16), preferred_element_type=F32)
            hu = jnp.dot(h, wu_ref[...].astype(BF16), preferred_element_type=F32)
            act = (hg * _sigmoid(hg)) * hu
            return jnp.dot(act.astype(BF16), wd_ref[...].astype(BF16), preferred_element_type=F32)

        y_ref[:, :D_MODEL] = expert(wg_lo, wu_lo, wd_lo)
        y_ref[:, D_MODEL:] = expert(wg_hi, wu_hi, wd_hi)

    @pl.when(j >= n_used)
    def _():
        y_ref[...] = jnp.zeros(y_ref.shape, F32)


def _moe_routed(hs, e_lo, e_hi, n_used, wg, wu, wd, layer):
    n_tiles = hs.shape[0] // TM
    up_spec = lambda tbl: pl.BlockSpec((None, None, D_MODEL, D_EXPERT),
                                       lambda j, lo, hi, nu: (layer, (lo, hi)[tbl][j], 0, 0))
    dn_spec = lambda tbl: pl.BlockSpec((None, None, D_EXPERT, D_MODEL),
                                       lambda j, lo, hi, nu: (layer, (lo, hi)[tbl][j], 0, 0))
    grid_spec = pltpu.PrefetchScalarGridSpec(
        num_scalar_prefetch=3,
        grid=(n_tiles,),
        in_specs=[pl.BlockSpec((TM, D_MODEL), lambda j, lo, hi, nu: (j, 0)),
                  up_spec(0), up_spec(0), dn_spec(0), up_spec(1), up_spec(1), dn_spec(1)],
        out_specs=pl.BlockSpec((TM, 2 * D_MODEL), lambda j, lo, hi, nu: (j, 0)),
    )
    return pl.pallas_call(
        _moe_routed_kernel,
        grid_spec=grid_spec,
        out_shape=jax.ShapeDtypeStruct((n_tiles * TM, 2 * D_MODEL), F32),
        compiler_params=_cparams(("arbitrary",), 56),
        name="moe_routed",
    )(e_lo, e_hi, n_used, hs, wg, wu, wd, wg, wu, wd)


def _combine_kernel(dest_ref, dest_next_ref, y_hbm, x_ref, meta_ref, mod_ref, o_ref, buf_ref, sems):
    step = pl.program_id(0) * pl.num_programs(1) + pl.program_id(1)
    n_steps = pl.num_programs(0) * pl.num_programs(1)
    slot = step % 2

    @pl.when(step == 0)
    def _():
        _gather_rows(dest_ref, y_hbm, buf_ref.at[0], sems.at[0])

    @pl.when(step + 1 < n_steps)
    def _():
        _gather_rows(dest_next_ref, y_hbm, buf_ref.at[1 - slot], sems.at[1 - slot])

    _gather_wait(y_hbm, buf_ref.at[slot], sems.at[slot])
    meta = meta_ref[...]
    moe = (meta[:, META_W_LO:META_W_LO + 1] * buf_ref[slot, :, :D_MODEL]
           + meta[:, META_W_HI:META_W_HI + 1] * buf_ref[slot, :, D_MODEL:])
    o_ref[...] = x_ref[...] + mod_ref[5:6, :] * moe


def _combine(dest, y, x1, meta, mod, *, tiles_per_batch):
    bsz = x1.shape[0]
    n_steps = bsz * tiles_per_batch
    tok_spec = lambda width: pl.BlockSpec((None, TM, width), lambda b, i: (b, i, 0))
    idx_spec = lambda ahead: pl.BlockSpec(
        (None, 1, TM), lambda b, i: (jnp.minimum(b * tiles_per_batch + i + ahead, n_steps - 1), 0, 0),
        memory_space=pltpu.SMEM)
    return pl.pallas_call(
        _combine_kernel,
        grid=(bsz, tiles_per_batch),
        in_specs=[idx_spec(0), idx_spec(1), pl.BlockSpec(memory_space=pl.ANY), tok_spec(D_MODEL),
                  tok_spec(LANES), pl.BlockSpec((None, SUBLANES, D_MODEL), _mod_index)],
        out_specs=tok_spec(D_MODEL),
        out_shape=jax.ShapeDtypeStruct(x1.shape, F32),
        scratch_shapes=[pltpu.VMEM((2, TM, 2 * D_MODEL), F32), pltpu.SemaphoreType.DMA((2,))],
        compiler_params=_cparams(("arbitrary", "arbitrary"), 32),
        name="moe_combine",
    )(dest, dest, y, x1, meta, mod)


def _moe(hf, meta, rt, counts, x1, mod, wg, wu, wd, layer, *, tiles_per_batch):
    n_tokens = hf.shape[0] * hf.shape[1]
    dest, e_lo, e_hi, n_used = _route_plan(rt, counts, n_tokens)
    hs = _dispatch(dest, hf.reshape(n_tokens, D_MODEL), n_tokens)
    y = _moe_routed(hs, e_lo, e_hi, n_used, wg, wu, wd, layer)
    return _combine(dest, y, x1, meta, mod, tiles_per_batch=tiles_per_batch)


def _inproj_odd_kernel(x_ref, mod_ref, g_ref, w_ref, y_ref, u_ref):
    h = _norm_mod(x_ref[...], g_ref[...], mod_ref[0:1, :], mod_ref[1:2, :])
    z = jnp.dot(h.astype(BF16), w_ref[...], preferred_element_type=F32)
    zy = z[:, :LRU_WIDTH]
    c0 = math.sqrt(2.0 / math.pi)
    y_ref[...] = (0.5 * zy * (1.0 + jnp.tanh(c0 * (zy + 0.044715 * (zy * zy * zy))))).astype(BF16)
    u_ref[...] = z[:, LRU_WIDTH:]


def _inproj_odd(xs, mod, g, w_in):
    bsz = xs.shape[0]
    tok_spec = lambda width: pl.BlockSpec((None, TM, width), lambda b, i: (b, i, 0))
    const = lambda shape: pl.BlockSpec(shape, lambda b, i: (0,) * len(shape))
    return pl.pallas_call(
        _inproj_odd_kernel,
        grid=(bsz, N_TILES),
        in_specs=[tok_spec(D_MODEL), pl.BlockSpec((None, SUBLANES, D_MODEL), _mod_index),
                  const((1, D_MODEL)), const((D_MODEL, 2 * LRU_WIDTH))],
        out_specs=[tok_spec(LRU_WIDTH), tok_spec(LRU_WIDTH)],
        out_shape=[jax.ShapeDtypeStruct((bsz, TOK, LRU_WIDTH), BF16),
                   jax.ShapeDtypeStruct((bsz, TOK, LRU_WIDTH), F32)],
        compiler_params=_cparams(("parallel", "arbitrary"), 48),
        name="inproj_odd",
    )(xs, mod, g, w_in)


def _scan_kernel(u_ref, other_ref, cw_ref, cbias_ref, wa_ref, ba_ref, wx_ref, bx_ref, lam_ref, h_ref,
                 halo_ref, carry_ref, a_ref, b_ref, *, reverse):
    i = pl.program_id(1)
    tt, w = u_ref.shape
    n_groups = tt // SUBLANES

    @pl.when(i <= 1)
    def _():
        halo_ref[...] = jnp.zeros(halo_ref.shape, F32)

    @pl.when(i == 0)
    def _():
        carry_ref[...] = jnp.zeros(carry_ref.shape, F32)

    u = u_ref[...]
    halo = halo_ref[...]
    row8 = lax.broadcasted_iota(jnp.int32, (SUBLANES, w), 0)
    k_self = 0 if reverse else LRU_CONV_K - 1
    uc = cbias_ref[...] + cw_ref[k_self:k_self + 1, :] * u
    for k in range(1, LRU_CONV_K):
        if reverse:
            tmp = pltpu.roll(u, tt - k, axis=0)
            hr = pltpu.roll(halo, SUBLANES - k, axis=0)
            edge = jnp.where(row8 >= SUBLANES - k, hr, tmp[tt - SUBLANES:, :])
            shifted = jnp.concatenate([tmp[:tt - SUBLANES, :], edge], axis=0)
            wk = cw_ref[k:k + 1, :]
        else:
            tmp = pltpu.roll(u, k, axis=0)
            hr = pltpu.roll(halo, k, axis=0)
            edge = jnp.where(row8 < k, hr, tmp[:SUBLANES, :])
            shifted = jnp.concatenate([edge, tmp[SUBLANES:, :]], axis=0)
            wk = cw_ref[LRU_CONV_K - 1 - k:LRU_CONV_K - k, :]
        uc = uc + wk * shifted
    halo_ref[...] = u[:SUBLANES, :] if reverse else u[tt - SUBLANES:, :]

    ucb = uc.astype(BF16)

    def block_diag(w_blocks):
        return jnp.concatenate(
            [jnp.dot(ucb[:, j * LRU_BLOCK:(j + 1) * LRU_BLOCK], w_blocks[j], preferred_element_type=F32)
             for j in range(LRU_BLOCKS)], axis=1)

    gate_i = 0.5 * jnp.tanh(block_diag(wx_ref) + bx_ref[...]) + 0.5
    neg_lam = -lam_ref[...]
    softplus = jnp.maximum(neg_lam, 0.0) + jnp.log1p(jnp.exp(-jnp.abs(neg_lam)))
    k = (-0.5 * LRU_C * math.log2(math.e)) * softplus
    a = jnp.exp2(k * jnp.tanh(block_diag(wa_ref) + ba_ref[...]) + k)
    a_ref[...] = a
    one_m_a2 = 1.0 - a * a
    root = jnp.where(one_m_a2 > 0.0, one_m_a2 * lax.rsqrt(one_m_a2), 0.0)
    b_ref[...] = root * (gate_i * uc)

    def group(gi, carry):
        g = n_groups - 1 - gi if reverse else gi
        off = pl.multiple_of(g * SUBLANES, SUBLANES)
        av = a_ref[pl.ds(off, SUBLANES), :]
        bv = b_ref[pl.ds(off, SUBLANES), :]
        for s in (1, 2, 4):
            if reverse:
                outside = row8 >= SUBLANES - s
                shift = SUBLANES - s
            else:
                outside = row8 < s
                shift = s
            a_sh = jnp.where(outside, 1.0, pltpu.roll(av, shift, axis=0))
            b_sh = jnp.where(outside, 0.0, pltpu.roll(bv, shift, axis=0))
            bv = av * b_sh + bv
            av = av * a_sh
        h = av * carry + bv
        h_ref[pl.ds(off, SUBLANES), :] = h if other_ref is None else h + other_ref[pl.ds(off, SUBLANES), :]
        last = h[0:1, :] if reverse else h[SUBLANES - 1:SUBLANES, :]
        return jnp.broadcast_to(last, (SUBLANES, w))

    carry_ref[...] = lax.fori_loop(0, n_groups, group, carry_ref[...])


def _rglru_scan(u, other, conv_w, conv_b, w_a, b_a, w_x, b_x, lam, *, reverse):
    bsz = u.shape[0]
    if reverse:
        tile_of = lambda i: N_LAT_TILES - i
    else:
        tile_of = lambda i: (i + N_LAT_TILES) % N_TILES
    tok_spec = pl.BlockSpec((None, TM, LRU_WIDTH), lambda b, i: (b, tile_of(i), 0))
    const = lambda shape: pl.BlockSpec(shape, lambda b, i: (0,) * len(shape))
    if other is None:
        body = lambda u_ref, *refs: _scan_kernel(u_ref, None, *refs, reverse=reverse)
        streams, stream_specs = [u], [tok_spec]
    else:
        body = functools.partial(_scan_kernel, reverse=reverse)
        streams, stream_specs = [u, other], [tok_spec, tok_spec]
    return pl.pallas_call(
        body,
        grid=(bsz, N_TILES),
        in_specs=stream_specs + [
            const((SUBLANES, LRU_WIDTH)), const((1, LRU_WIDTH)),
            const((LRU_BLOCKS, LRU_BLOCK, LRU_BLOCK)), const((1, LRU_WIDTH)),
            const((LRU_BLOCKS, LRU_BLOCK, LRU_BLOCK)), const((1, LRU_WIDTH)), const((1, LRU_WIDTH))],
        out_specs=tok_spec,
        out_shape=jax.ShapeDtypeStruct((bsz, TOK, LRU_WIDTH), F32),
        scratch_shapes=[pltpu.VMEM((SUBLANES, LRU_WIDTH), F32), pltpu.VMEM((SUBLANES, LRU_WIDTH), F32),
                        pltpu.VMEM((TM, LRU_WIDTH), F32), pltpu.VMEM((TM, LRU_WIDTH), F32)],
        compiler_params=_cparams(("parallel", "arbitrary"), 32),
        name="rglru_rev" if reverse else "rglru_fwd",
    )(*streams, conv_w, conv_b, (0.5 * w_a).astype(BF16), 0.5 * b_a, (0.5 * w_x).astype(BF16), 0.5 * b_x, lam)


def _pad_rows(a, rows):
    return jnp.pad(a, ((0, rows - a.shape[0]), (0, 0)))


def _rope_tables():
    t = jnp.arange(SEQ)
    n_freq = DA_HEAD_DIM // 4
    inv = ROPE_BASE ** (-jnp.arange(n_freq, dtype=F32) / n_freq)
    ang = jnp.concatenate([(t // GRID_W).astype(F32)[:, None] * inv,
                           (t % GRID_W).astype(F32)[:, None] * inv], axis=-1)
    cos, sin = jnp.cos(ang), jnp.sin(ang)
    cos64 = jnp.concatenate([cos, cos], axis=-1)
    sin64 = jnp.concatenate([-sin, sin], axis=-1)
    cos_t = jnp.concatenate([jnp.tile(cos64, (1, 2)), jnp.ones((CTX_LEN, LANES), F32)], axis=0)
    sin_t = jnp.concatenate([jnp.tile(sin64, (1, 2)), jnp.zeros((CTX_LEN, LANES), F32)], axis=0)
    return cos_t, sin_t


def _router_params(w_grp, b_grp, w_rt, b_rt):
    wr = jnp.concatenate([w_rt.reshape(D_MODEL, N_EXPERTS), w_grp], axis=1)
    br = jnp.concatenate([b_rt.reshape(N_EXPERTS), b_grp])
    pad = LANES - wr.shape[1]
    wr = jnp.pad(wr, ((0, 0), (0, pad)))
    wr_hi = wr.astype(BF16)
    wr_lo = (wr - wr_hi.astype(F32)).astype(BF16)
    return jnp.stack([wr_hi, wr_lo]), jnp.pad(br, (0, pad)).reshape(1, LANES)


def kernel(x, c, ctx, c_ctx, ada_w, ada_b, norm_mix, norm_ffn, ev_w_in, ev_conv_w, ev_conv_b, ev_q_norm, ev_k_norm, ev_lam_q1, ev_lam_k1, ev_lam_q2, ev_lam_k2, ev_sub_norm, ev_w_out, od_w_in, od_conv_w, od_conv_b, od_w_a, od_b_a, od_w_x, od_b_x, od_lam, od_w_out, moe_w_grp, moe_b_grp, moe_w_rt, moe_b_rt, moe_w_gate, moe_w_up, moe_w_down):
    bsz = x.shape[0]
    assert x.shape == (bsz, SEQ, D_MODEL) and ctx.shape == (bsz, CTX_LEN, D_MODEL) and bsz == 2
    depth = ada_w.shape[0]
    assert depth == 2


    cvec = _pad_rows(jnp.stack([c[0], c_ctx, c[1], c_ctx]), SUBLANES)
    mod_all = _ada_mod(cvec, ada_w, ada_b)
    mod_all = mod_all[:, :2 * bsz].reshape(depth, 2 * bsz, 6, D_MODEL)
    mod_all = jnp.pad(mod_all, ((0, 0), (0, 0), (0, SUBLANES - 6), (0, 0)))

    l = 0
    lam_init = 0.8 - 0.6 * math.exp(-0.3 * l)
    mod = mod_all[l]
    cos_t, sin_t = _rope_tables()
    blk = jnp.arange(DA_WIDTH) // DA_HEAD_DIM
    bd = jnp.where(blk[:, None] == blk[None, :], 1.0 / DA_HEAD_DIM, 0.0).astype(BF16)
    n_rep = DA_WIDTH // DA_HEAD_DIM
    cb, p, q, k, v = _inproj_even(
        x, ctx, mod, norm_mix[l].reshape(1, D_MODEL), ev_w_in[0].astype(BF16),
        jnp.tile(ev_q_norm[0], n_rep).reshape(1, DA_WIDTH), jnp.tile(ev_k_norm[0], n_rep).reshape(1, DA_WIDTH),
        cos_t, sin_t, bd)

    lamv = _pad_rows(jnp.pad(jnp.stack([ev_lam_q1[0], ev_lam_k1[0], ev_lam_q2[0], ev_lam_k2[0]]),
                             ((0, 0), (0, LANES - DA_HEAD_DIM))), SUBLANES)
    subg = ev_sub_norm[0].reshape(1, DA_V_DIM)
    o_lat = _attention(q, k, v, lamv, subg, tq=ATTN_TQ, q_start=0, n_q=SEQ // ATTN_TQ,
                       kv_start=0, kv_len=TOK, tk=ATTN_TK, lam_init=lam_init)
    o_ctx = _attention(q, k, v, lamv, subg, tq=TM, q_start=SEQ, n_q=CTX_LEN // TM,
                       kv_start=SEQ, kv_len=CTX_LEN, tk=CTX_LEN, lam_init=lam_init)

    wr, br = _router_params(moe_w_grp[l], moe_b_grp[l], moe_w_rt[l], moe_b_rt[l])
    x1, hf, meta, rt, counts = _post_even(
        cb, p, _pad_rows(ev_conv_w[0], SUBLANES), ev_conv_b[0].reshape(1, CONV_WIDTH), o_lat, o_ctx,
        ev_w_out[0].astype(BF16), x, ctx, mod, norm_ffn[l].reshape(1, D_MODEL), wr, br)
    xs = _moe(hf, meta, rt, counts, x1, mod, moe_w_gate, moe_w_up, moe_w_down, l, tiles_per_batch=N_TILES)

    l = 1
    mod = mod_all[l]
    y, u = _inproj_odd(xs, mod, norm_mix[l].reshape(1, D_MODEL), od_w_in[0].astype(BF16))
    h_sum = None
    for d in range(2):
        h_sum = _rglru_scan(
            u, h_sum, _pad_rows(od_conv_w[0, d], SUBLANES), od_conv_b[0, d].reshape(1, LRU_WIDTH),
            od_w_a[0, d], od_b_a[0, d].reshape(1, LRU_WIDTH),
            od_w_x[0, d], od_b_x[0, d].reshape(1, LRU_WIDTH),
            od_lam[0, d].reshape(1, LRU_WIDTH), reverse=bool(d))
    wr, br = _router_params(moe_w_grp[l], moe_b_grp[l], moe_w_rt[l], moe_b_rt[l])
    x1, hf, meta, rt, counts = _post_odd(y, h_sum, od_w_out[0].astype(BF16), xs, mod,
                                         norm_ffn[l].reshape(1, D_MODEL), wr, br)
    return _moe(hf, meta, rt, counts, x1, mod, moe_w_gate, moe_w_up, moe_w_down, l,
                tiles_per_batch=N_LAT_TILES)
```
